```python
import math
import jax, jax.numpy as jnp
from jax import lax
import numpy as np


D_MODEL = 2048
BATCH = 8
SEQ = 2048
DEPTH = 2

HEAD_DIM = 128
FOX_HEADS = 8
DIFF_HEADS = 4
DIFF_V_DIM = 2 * HEAD_DIM
DIL_HEADS = D_MODEL // HEAD_DIM
DIL_PATTERNS = ((128, 1), (512, 4), (2048, 16))
Q_BLOCK = 128
N_BUCKETS = 32
BUCKET_MAX_EXACT = 16
BUCKET_MAX_DIST = 2048
N_BIAS_HEADS = DIL_HEADS
PEER_HEADS = 8
PEER_NKEYS = 128
PEER_EXPERTS = PEER_NKEYS * PEER_NKEYS
PEER_DKEY = 256
PEER_TOPK = 16
PEER_CHUNK = 128
NORM_EPS = 1e-6
FORGET_BIAS_INIT = 3.0

FOX_W = FOX_HEADS * HEAD_DIM
DIFF_QK_W = DIFF_HEADS * 2 * HEAD_DIM
DIFF_V_W = DIFF_HEADS * DIFF_V_DIM
EVEN_IN_W = 3 * FOX_W + FOX_HEADS + 2 * DIFF_QK_W + DIFF_V_W
EVEN_SPLITS = [int(v) for v in np.cumsum([FOX_W, FOX_W, FOX_W, FOX_HEADS, DIFF_QK_W, DIFF_QK_W])]
EVEN_OUT_W = FOX_W + DIFF_V_W

kernel_name = "hybrid_fox_diff_dilated_peer_adaln"


def rms_norm(x, gain):
    xf = x.astype(jnp.float32)
    y = xf * lax.rsqrt(jnp.mean(xf * xf, axis=-1, keepdims=True) + NORM_EPS)
    return (y * gain.astype(jnp.float32)).astype(x.dtype)


def t5_bucket(dist):
    n = jnp.maximum(dist, 0)
    nf = jnp.maximum(n, 1).astype(jnp.float32)
    large = BUCKET_MAX_EXACT + (jnp.log(nf / BUCKET_MAX_EXACT)
                                / math.log(BUCKET_MAX_DIST / BUCKET_MAX_EXACT)
                                * (N_BUCKETS - BUCKET_MAX_EXACT)).astype(jnp.int32)
    large = jnp.minimum(large, N_BUCKETS - 1)
    return jnp.where(n < BUCKET_MAX_EXACT, n, large)


def fox_attention(q, k, v, log_f_cum):
    b, s, h, dh = q.shape
    scale = dh ** -0.5
    kpos = jnp.arange(s)
    f_keys = log_f_cum.transpose(0, 2, 1)

    def block(i):
        t0 = i * Q_BLOCK
        qb = lax.dynamic_slice_in_dim(q, t0, Q_BLOCK, axis=1)
        fq = lax.dynamic_slice_in_dim(f_keys, t0, Q_BLOCK, axis=2)
        qpos = t0 + jnp.arange(Q_BLOCK)
        logits = jnp.einsum('bqhd,bkhd->bhqk', qb, k, preferred_element_type=jnp.float32) * scale
        logits = logits + fq[..., :, None] - f_keys[..., None, :]
        logits = jnp.where((qpos[:, None] >= kpos[None, :])[None, None], logits, -jnp.inf)
        p = jax.nn.softmax(logits, axis=-1)
        return jnp.einsum('bhqk,bkhd->bqhd', p.astype(v.dtype), v)

    out = lax.map(block, jnp.arange(s // Q_BLOCK))
    return out.transpose(1, 0, 2, 3, 4).reshape(b, s, h, dh)


def diff_attention(q, k, v, lam, bias_table):
    b, s, h, _, dh = q.shape
    scale = dh ** -0.5
    kpos = jnp.arange(s)

    def block(i):
        t0 = i * Q_BLOCK
        qb = lax.dynamic_slice_in_dim(q, t0, Q_BLOCK, axis=1)
        qpos = t0 + jnp.arange(Q_BLOCK)
        dist = qpos[:, None] - kpos[None, :]
        bias = bias_table[t5_bucket(dist)].astype(jnp.float32).transpose(2, 0, 1)
        logits = jnp.einsum('bqhmd,bkhmd->bhmqk', qb, k, preferred_element_type=jnp.float32) * scale
        logits = logits + bias[None, :, None]
        logits = jnp.where((dist >= 0)[None, None, None], logits, -jnp.inf)
        p = jax.nn.softmax(logits, axis=-1)
        attn = p[:, :, 0] - lam * p[:, :, 1]
        return jnp.einsum('bhqk,bkhe->bqhe', attn.astype(v.dtype), v)

    out = lax.map(block, jnp.arange(s // Q_BLOCK))
    return out.transpose(1, 0, 2, 3, 4).reshape(b, s, h, 2 * dh)


def dilated_attention(q, k, v, bias_table):
    b, s, h, dh = q.shape
    scale = dh ** -0.5
    outs, lses = [], []
    for window, dil in DIL_PATTERNS:
        span = window // dil
        n_sub = s // dil
        qb_len = math.gcd(Q_BLOCK, n_sub)
        nblk = n_sub // qb_len
        qs = q.reshape(b, nblk, qb_len, dil, h, dh)
        pad = ((0, 0), (span, 0), (0, 0), (0, 0), (0, 0))
        ks = jnp.pad(k.reshape(b, n_sub, dil, h, dh), pad)
        vs = jnp.pad(v.reshape(b, n_sub, dil, h, dh), pad)
        kidx = jnp.arange(nblk)[:, None] * qb_len + jnp.arange(qb_len + span)[None, :]
        kb = ks[:, kidx]
        vb = vs[:, kidx]
        step = jnp.arange(qb_len)[:, None] + span - jnp.arange(qb_len + span)[None, :]
        valid = ((step >= 0) & (step <= span))[None] & ((kidx - span) >= 0)[:, None, :]
        bias = bias_table[t5_bucket(step * dil)].astype(jnp.float32).transpose(2, 0, 1)
        logits = jnp.einsum('bnqrhd,bnkrhd->bnrhqk', qs, kb, preferred_element_type=jnp.float32) * scale
        logits = logits + bias[None, None, None]
        logits = jnp.where(valid[None, :, None, None], logits, -jnp.inf)
        lse = jax.nn.logsumexp(logits, axis=-1)
        probs = jnp.exp(logits - lse[..., None])
        o = jnp.einsum('bnrhqk,bnkrhd->bnqrhd', probs.astype(v.dtype), vb)
        outs.append(o.reshape(b, s, h, dh))
        lses.append(lse.transpose(0, 1, 4, 2, 3).reshape(b, s, h))
    w = jax.nn.softmax(jnp.stack(lses, axis=-1), axis=-1)
    return jnp.einsum('bshp,pbshd->bshd', w.astype(v.dtype), jnp.stack(outs))


def even_mixer(h, w_in, b_forget, fox_qk_gain, diff_qk_gain, diff_lambda,
               diff_subln_gain, w_out, diff_bias, lam_init):
    b, s, _ = h.shape
    proj = h @ w_in
    fq, fk, fv, ff, dq, dk, dv = jnp.split(proj, EVEN_SPLITS, axis=-1)
    fq = rms_norm(fq.reshape(b, s, FOX_HEADS, HEAD_DIM), fox_qk_gain[0])
    fk = rms_norm(fk.reshape(b, s, FOX_HEADS, HEAD_DIM), fox_qk_gain[1])
    fv = fv.reshape(b, s, FOX_HEADS, HEAD_DIM)
    log_f = jax.nn.log_sigmoid(ff.astype(jnp.float32) + b_forget.astype(jnp.float32))
    log_f_cum = jnp.cumsum(log_f, axis=1)
    fox_o = fox_attention(fq, fk, fv, log_f_cum)
    dq = rms_norm(dq.reshape(b, s, DIFF_HEADS, 2, HEAD_DIM), diff_qk_gain[0])
    dk = rms_norm(dk.reshape(b, s, DIFF_HEADS, 2, HEAD_DIM), diff_qk_gain[1])
    dv = dv.reshape(b, s, DIFF_HEADS, DIFF_V_DIM)
    lam_f = diff_lambda.astype(jnp.float32)
    lam = (jnp.exp(jnp.sum(lam_f[0] * lam_f[1])) - jnp.exp(jnp.sum(lam_f[2] * lam_f[3]))
           + lam_init)
    diff_o = diff_attention(dq, dk, dv, lam, diff_bias)
    diff_o = rms_norm(diff_o, diff_subln_gain) * (1.0 - lam_init)
    mixed = jnp.concatenate([fox_o.reshape(b, s, FOX_W),
                             diff_o.reshape(b, s, DIFF_V_W).astype(fox_o.dtype)], axis=-1)
    return mixed @ w_out


def odd_mixer(h, w_qkv, qk_gain, w_out, bias_table):
    b, s, _ = h.shape
    qkv = (h @ w_qkv).reshape(b, s, 3, DIL_HEADS, HEAD_DIM)
    q = rms_norm(qkv[:, :, 0], qk_gain[0])
    k = rms_norm(qkv[:, :, 1], qk_gain[1])
    o = dilated_attention(q, k, qkv[:, :, 2], bias_table)
    return o.reshape(b, s, DIL_HEADS * HEAD_DIM) @ w_out


def peer_ffn(h, w_query, sub_keys, expert_u, expert_v):
    b, s, d = h.shape
    t = b * s
    hf = h.reshape(t, d)
    q = (hf @ w_query).reshape(t, PEER_HEADS, 2, PEER_DKEY // 2)
    scores = jnp.einsum('thpd,pnd->thpn', q, sub_keys, preferred_element_type=jnp.float32)
    top_s, top_i = lax.top_k(scores, PEER_TOPK)
    cand_s = (top_s[:, :, 0, :, None] + top_s[:, :, 1, None, :]).reshape(t, PEER_HEADS, -1)
    cand_i = (top_i[:, :, 0, :, None] * PEER_NKEYS + top_i[:, :, 1, None, :]).reshape(t, PEER_HEADS, -1)
    best_s, best_pos = lax.top_k(cand_s, PEER_TOPK)
    expert_idx = jnp.take_along_axis(cand_i, best_pos, axis=-1)
    gates = jax.nn.softmax(best_s, axis=-1).astype(h.dtype)
    n_chunk = t // PEER_CHUNK

    def chunk(args):
        x_c, idx_c, g_c = args
        act = jax.nn.gelu(jnp.einsum('cd,chkd->chk', x_c, expert_u[idx_c]), approximate=False)
        return jnp.einsum('chk,chkd->cd', g_c * act, expert_v[idx_c])

    y = lax.map(chunk, (hf.reshape(n_chunk, PEER_CHUNK, d),
                        expert_idx.reshape(n_chunk, PEER_CHUNK, PEER_HEADS, PEER_TOPK),
                        gates.reshape(n_chunk, PEER_CHUNK, PEER_HEADS, PEER_TOPK)))
    return y.reshape(b, s, d)


def modulate(h, shift, scale):
    return h * (1.0 + scale[:, None, :]) + shift[:, None, :]


def setup_inputs(seed: int = 0) -> dict:
    key = jax.random.key(seed)
    ks = jax.random.split(key, 20)
    n_even = (DEPTH + 1) // 2
    n_odd = DEPTH // 2
    nrm = jax.random.normal
    f32 = jnp.float32
    sd = D_MODEL ** -0.5
    return {
        "x": nrm(ks[0], (BATCH, SEQ, D_MODEL), f32),
        "c": nrm(ks[1], (BATCH, D_MODEL), f32),
        "rel_bias": 0.3 * nrm(ks[2], (N_BUCKETS, N_BIAS_HEADS), f32),
        "norm_gain": 1.0 + 0.1 * nrm(ks[3], (DEPTH, 2, D_MODEL), f32),
        "w_ada": 0.5 * sd * nrm(ks[4], (DEPTH, D_MODEL, 6 * D_MODEL), f32),
        "b_ada": 0.02 * nrm(ks[5], (DEPTH, 6 * D_MODEL), f32),
        "even_w_in": sd * nrm(ks[6], (n_even, D_MODEL, EVEN_IN_W), f32),
        "even_b_forget": FORGET_BIAS_INIT + 0.5 * nrm(ks[7], (n_even, FOX_HEADS), f32),
        "even_fox_qk_gain": 1.0 + 0.1 * nrm(ks[8], (n_even, 2, HEAD_DIM), f32),
        "even_diff_qk_gain": 1.0 + 0.1 * nrm(ks[9], (n_even, 2, HEAD_DIM), f32),
        "even_diff_lambda": 0.1 * nrm(ks[10], (n_even, 4, HEAD_DIM), f32),
        "even_diff_subln_gain": 1.0 + 0.1 * nrm(ks[11], (n_even, DIFF_V_DIM), f32),
        "even_w_out": EVEN_OUT_W ** -0.5 * nrm(ks[12], (n_even, EVEN_OUT_W, D_MODEL), f32),
        "odd_w_qkv": sd * nrm(ks[13], (n_odd, D_MODEL, 3 * DIL_HEADS * HEAD_DIM), f32),
        "odd_qk_gain": 1.0 + 0.1 * nrm(ks[14], (n_odd, 2, HEAD_DIM), f32),
        "odd_w_out": sd * nrm(ks[15], (n_odd, DIL_HEADS * HEAD_DIM, D_MODEL), f32),
        "peer_w_query": sd * nrm(ks[16], (DEPTH, D_MODEL, PEER_HEADS * PEER_DKEY), f32),
        "peer_sub_keys": (PEER_DKEY // 2) ** -0.5 * nrm(ks[17], (DEPTH, 2, PEER_NKEYS, PEER_DKEY // 2), f32),
        "peer_u": sd * nrm(ks[18], (DEPTH, PEER_EXPERTS, D_MODEL), f32),
        "peer_v": PEER_TOPK ** -0.5 * nrm(ks[19], (DEPTH, PEER_EXPERTS, D_MODEL), f32),
    }


def reference(x, c, rel_bias, norm_gain, w_ada, b_ada,
              even_w_in, even_b_forget, even_fox_qk_gain, even_diff_qk_gain,
              even_diff_lambda, even_diff_subln_gain, even_w_out,
              odd_w_qkv, odd_qk_gain, odd_w_out,
              peer_w_query, peer_sub_keys, peer_u, peer_v):
    cond = jax.nn.silu(c)
    for i in range(DEPTH):
        mod = cond @ w_ada[i] + b_ada[i]
        sh1, sc1, g1, sh2, sc2, g2 = jnp.split(mod, 6, axis=-1)
        h = modulate(rms_norm(x, norm_gain[i, 0]), sh1, sc1)
        j = i // 2
        if i % 2 == 0:
            lam_init = 0.8 - 0.6 * math.exp(-0.3 * i)
            y = even_mixer(h, even_w_in[j], even_b_forget[j], even_fox_qk_gain[j],
                           even_diff_qk_gain[j], even_diff_lambda[j], even_diff_subln_gain[j],
                           even_w_out[j], rel_bias[:, :DIFF_HEADS], lam_init)
        else:
            y = odd_mixer(h, odd_w_qkv[j], odd_qk_gain[j], odd_w_out[j], rel_bias)
        x = x + g1[:, None, :] * y.astype(x.dtype)
        h = modulate(rms_norm(x, norm_gain[i, 1]), sh2, sc2)
        y = peer_ffn(h, peer_w_query[i], peer_sub_keys[i], peer_u[i], peer_v[i])
        x = x + g2[:, None, :] * y.astype(x.dtype)
    return x
```

```python
import functools
import math

import numpy as np
import jax
import jax.numpy as jnp
from jax import lax
from jax.experimental import pallas as pl
from jax.experimental.pallas import tpu as pltpu

F32 = jnp.float32
BF16 = jnp.bfloat16

D_MODEL = 2048
DEPTH = 2
HEAD_DIM = 128
FOX_HEADS = 8
DIFF_HEADS = 4
DIFF_V_DIM = 2 * HEAD_DIM
DIL_HEADS = D_MODEL // HEAD_DIM
DIL_PATTERNS = ((128, 1), (512, 4), (2048, 16))
DIL_BLOCK = 128
N_BUCKETS = 32
BUCKET_MAX_EXACT = 16
BUCKET_MAX_DIST = 2048
PEER_HEADS = 8
PEER_NKEYS = 128
PEER_EXPERTS = PEER_NKEYS * PEER_NKEYS
PEER_DKEY = 256
PEER_TOPK = 16
PEER_PICKS = PEER_HEADS * PEER_TOPK
NORM_EPS = 1e-6
FOX_W = FOX_HEADS * HEAD_DIM
DIFF_QK_W = DIFF_HEADS * 2 * HEAD_DIM
DIFF_V_W = DIFF_HEADS * DIFF_V_DIM
ATTN_SCALE = HEAD_DIM ** -0.5
MASKED = -1e30

LANES = 128
SUBLANES = 8
VMEM_LIMIT = 48 * 1024 * 1024

PROJ_TM = 512
PROJ_TN = 1024
ATTN_T = 512
ROUTE_TM = 256
PEER_TP = 8


def _cparams(*sem):
    return pltpu.CompilerParams(dimension_semantics=sem, vmem_limit_bytes=VMEM_LIMIT)


def _adaln_kernel(c_ref, w_ref, b_ref, o_ref):
    c = c_ref[...]
    cond = c * jax.nn.sigmoid(c)
    o_ref[...] = jnp.dot(cond.astype(BF16), w_ref[...].astype(BF16),
                         preferred_element_type=F32) + b_ref[...]


def _adaln(c, w_ada, b_ada):
    depth, d, n = w_ada.shape
    bsz = c.shape[0]
    tn = 1024
    return pl.pallas_call(
        _adaln_kernel,
        grid=(depth, n // tn),
        in_specs=[
            pl.BlockSpec((bsz, d), lambda i, j: (0, 0)),
            pl.BlockSpec((None, d, tn), lambda i, j: (i, 0, j)),
            pl.BlockSpec((None, 1, tn), lambda i, j: (i, 0, j)),
        ],
        out_specs=pl.BlockSpec((None, bsz, tn), lambda i, j: (i, 0, j)),
        out_shape=jax.ShapeDtypeStruct((depth, bsz, n), F32),
        compiler_params=_cparams("parallel", "parallel"),
        name="adaln",
    )(c, w_ada, b_ada.reshape(depth, 1, n))


def _modulated_norm(x, gain, sc, sh):
    ms = jnp.mean(x * x, axis=-1, keepdims=True)
    y = x * lax.rsqrt(ms + NORM_EPS) * gain
    return y * (1.0 + sc) + sh


def _norm_proj_kernel(*refs, norm_tiles, has_side):
    if has_side:
        x_ref, gain_ref, sc_ref, sh_ref, w_ref, hg_ref, ws_ref, o_ref, side_ref, h_ref = refs
    else:
        x_ref, gain_ref, sc_ref, sh_ref, w_ref, hg_ref, o_ref, h_ref = refs
    j = pl.program_id(1)

    @pl.when(j == 0)
    def _():
        h = _modulated_norm(x_ref[...], gain_ref[...], sc_ref[...], sh_ref[...])
        h_ref[...] = h.astype(BF16)
        if has_side:
            side_ref[...] = jnp.dot(h_ref[...], ws_ref[...], preferred_element_type=F32)

    acc = jnp.dot(h_ref[...], w_ref[...], preferred_element_type=F32)
    tn = acc.shape[1]

    def write_normed():
        for g in range(tn // HEAD_DIM):
            sl = slice(g * HEAD_DIM, (g + 1) * HEAD_DIM)
            blk = acc[:, sl]
            ms = jnp.mean(blk * blk, axis=-1, keepdims=True)
            o_ref[:, sl] = (blk * lax.rsqrt(ms + NORM_EPS) * hg_ref[:, sl]).astype(o_ref.dtype)

    def write_raw():
        o_ref[...] = acc.astype(o_ref.dtype)

    if not norm_tiles:
        write_raw()
    else:
        is_norm = functools.reduce(jnp.logical_or, [j == t for t in norm_tiles])
        pl.when(is_norm)(write_normed)
        pl.when(jnp.logical_not(is_norm))(write_raw)


def _norm_proj(x2, gain, sc, sh, w, head_gain, norm_tiles, seq, w_side=None):
    t, d = x2.shape
    n = w.shape[1]
    tm, tn = PROJ_TM, PROJ_TN
    per_b = seq // tm
    has_side = w_side is not None
    in_specs = [
        pl.BlockSpec((tm, d), lambda i, j: (i, 0)),
        pl.BlockSpec((1, d), lambda i, j: (0, 0)),
        pl.BlockSpec((None, 1, d), lambda i, j: (i // per_b, 0, 0)),
        pl.BlockSpec((None, 1, d), lambda i, j: (i // per_b, 0, 0)),
        pl.BlockSpec((d, tn), lambda i, j: (0, j)),
        pl.BlockSpec((1, tn), lambda i, j: (0, j)),
    ]
    args = [x2, gain.reshape(1, d), sc, sh, w, head_gain]
    out_specs = pl.BlockSpec((tm, tn), lambda i, j: (i, j))
    out_shape = jax.ShapeDtypeStruct((t, n), BF16)
    if has_side:
        ns = w_side.shape[1]
        in_specs.append(pl.BlockSpec((d, ns), lambda i, j: (0, 0)))
        args.append(w_side)
        out_specs = [out_specs, pl.BlockSpec((tm, ns), lambda i, j: (i, 0))]
        out_shape = [out_shape, jax.ShapeDtypeStruct((t, ns), F32)]
    return pl.pallas_call(
        functools.partial(_norm_proj_kernel, norm_tiles=tuple(norm_tiles), has_side=has_side),
        grid=(t // tm, n // tn),
        in_specs=in_specs,
        out_specs=out_specs,
        out_shape=out_shape,
        scratch_shapes=[pltpu.VMEM((tm, d), BF16)],
        compiler_params=_cparams("parallel", "arbitrary"),
        name="norm_proj",
    )(*args)


def _out_proj_kernel(a_ref, w_ref, x_ref, g_ref, o_ref):
    y = jnp.dot(a_ref[...], w_ref[...], preferred_element_type=F32)
    o_ref[...] = x_ref[...] + g_ref[...] * y


def _out_proj(a, w, x2, g, seq):
    t, k = a.shape
    n = w.shape[1]
    tm, tn = PROJ_TM, PROJ_TN
    per_b = seq // tm
    return pl.pallas_call(
        _out_proj_kernel,
        grid=(t // tm, n // tn),
        in_specs=[
            pl.BlockSpec((tm, k), lambda i, j: (i, 0)),
            pl.BlockSpec((k, tn), lambda i, j: (0, j)),
            pl.BlockSpec((tm, tn), lambda i, j: (i, j)),
            pl.BlockSpec((None, 1, tn), lambda i, j: (i // per_b, 0, j)),
        ],
        out_specs=pl.BlockSpec((tm, tn), lambda i, j: (i, j)),
        out_shape=jax.ShapeDtypeStruct((t, n), F32),
        compiler_params=_cparams("parallel", "parallel"),
        name="out_proj",
    )(a, w, x2, g)


def _forget_kernel(f_ref, b_ref, col_ref, row_ref):
    z = f_ref[...] + b_ref[...]
    lf = jnp.minimum(z, 0.0) - jnp.log1p(jnp.exp(-jnp.abs(z)))
    s = lf.shape[0]
    pos = lax.broadcasted_iota(jnp.int32, lf.shape, 0)
    d = 1
    while d < s:
        lf = lf + jnp.where(pos >= d, pltpu.roll(lf, d, axis=0), 0.0)
        d *= 2
    col_ref[...] = lf
    row_ref[...] = lf.T[:FOX_HEADS, :]


def _forget_cumsum(f_side, b_forget, bsz, seq):
    bpad = jnp.zeros((1, LANES), F32).at[0, :FOX_HEADS].set(b_forget)
    col, row = pl.pallas_call(
        _forget_kernel,
        grid=(bsz,),
        in_specs=[
            pl.BlockSpec((None, seq, LANES), lambda b: (b, 0, 0)),
            pl.BlockSpec((1, LANES), lambda b: (0, 0)),
        ],
        out_specs=[
            pl.BlockSpec((None, seq, LANES), lambda b: (b, 0, 0)),
            pl.BlockSpec((None, FOX_HEADS, seq), lambda b: (b, 0, 0)),
        ],
        out_shape=[
            jax.ShapeDtypeStruct((bsz, seq, LANES), F32),
            jax.ShapeDtypeStruct((bsz, FOX_HEADS, seq), F32),
        ],
        compiler_params=_cparams("parallel"),
        name="forget_cumsum",
    )(f_side.reshape(bsz, seq, LANES), bpad)
    return col, row.reshape(bsz, FOX_HEADS, 1, seq)


def _online_softmax_step(s, v, m_ref, l_ref, acc_ref):
    m_prev = m_ref[...]
    m_new = jnp.maximum(m_prev, jnp.max(s, axis=-1, keepdims=True))
    alpha = jnp.exp(m_prev - m_new)
    p = jnp.exp(s - m_new)
    l_ref[...] = alpha * l_ref[...] + jnp.sum(p, axis=-1, keepdims=True)
    acc_ref[...] = alpha * acc_ref[...] + jnp.dot(p.astype(v.dtype), v, preferred_element_type=F32)
    m_ref[...] = m_new


def _fox_kernel(q_ref, k_ref, v_ref, frow_ref, fcol_ref, o_ref, m_ref, l_ref, acc_ref, fq_ref):
    h = pl.program_id(1)
    qi = pl.program_id(2)
    ki = pl.program_id(3)
    tq, tk = q_ref.shape[0], k_ref.shape[0]

    @pl.when(ki == 0)
    def _():
        m_ref[...] = jnp.full_like(m_ref, MASKED)
        l_ref[...] = jnp.zeros_like(l_ref)
        acc_ref[...] = jnp.zeros_like(acc_ref)
        lane = lax.broadcasted_iota(jnp.int32, fcol_ref.shape, 1)
        fq_ref[...] = jnp.sum(jnp.where(lane == h, fcol_ref[...], 0.0), axis=-1, keepdims=True)

    @pl.when(ki <= qi)
    def _():
        s = lax.dot_general(q_ref[...], k_ref[...], (((1,), (1,)), ((), ())),
                            preferred_element_type=F32)
        s = s + fq_ref[...] - frow_ref[...]
        qpos = qi * tq + lax.broadcasted_iota(jnp.int32, s.shape, 0)
        kpos = ki * tk + lax.broadcasted_iota(jnp.int32, s.shape, 1)
        s = jnp.where(qpos >= kpos, s, MASKED)
        _online_softmax_step(s, v_ref[...], m_ref, l_ref, acc_ref)

    @pl.when(ki == qi)
    def _():
        o_ref[...] = (acc_ref[...] / l_ref[...]).astype(o_ref.dtype)


def _fox_attention(proj, frow, fcol, bsz, seq):
    t = proj.shape[0]
    tt = ATTN_T
    nb = seq // tt
    kcol = FOX_W // HEAD_DIM
    return pl.pallas_call(
        _fox_kernel,
        grid=(bsz, FOX_HEADS, nb, nb),
        in_specs=[
            pl.BlockSpec((tt, HEAD_DIM), lambda b, h, qi, ki: (b * nb + qi, h)),
            pl.BlockSpec((tt, HEAD_DIM), lambda b, h, qi, ki: (b * nb + jnp.minimum(ki, qi), kcol + h)),
            pl.BlockSpec((tt, HEAD_DIM), lambda b, h, qi, ki: (b * nb + jnp.minimum(ki, qi), 2 * kcol + h)),
            pl.BlockSpec((None, None, 1, tt), lambda b, h, qi, ki: (b, h, 0, jnp.minimum(ki, qi))),
            pl.BlockSpec((None, tt, LANES), lambda b, h, qi, ki: (b, qi, 0)),
        ],
        out_specs=pl.BlockSpec((tt, HEAD_DIM), lambda b, h, qi, ki: (b * nb + qi, h)),
        out_shape=jax.ShapeDtypeStruct((t, FOX_W), BF16),
        scratch_shapes=[
            pltpu.VMEM((tt, 1), F32), pltpu.VMEM((tt, 1), F32),
            pltpu.VMEM((tt, HEAD_DIM), F32), pltpu.VMEM((tt, 1), F32),
        ],
        compiler_params=_cparams("parallel", "parallel", "parallel", "arbitrary"),
        name="fox_attention",
    )(proj, proj, proj, frow, fcol)


def _diff_kernel(q_ref, k_ref, v_ref, bias_ref, lam_ref, sg_ref, o_ref,
                 m0, l0, a0, m1, l1, a1, *, lam_init):
    qi = pl.program_id(2)
    ki = pl.program_id(3)

    @pl.when(ki == 0)
    def _():
        for m_ref, l_ref, a_ref in ((m0, l0, a0), (m1, l1, a1)):
            m_ref[...] = jnp.full_like(m_ref, MASKED)
            l_ref[...] = jnp.zeros_like(l_ref)
            a_ref[...] = jnp.zeros_like(a_ref)

    @pl.when(ki <= qi)
    def _():
        bias = bias_ref[...]
        v = v_ref[...]
        for mi, (m_ref, l_ref, a_ref) in enumerate(((m0, l0, a0), (m1, l1, a1))):
            sl = slice(mi * HEAD_DIM, (mi + 1) * HEAD_DIM)
            s = lax.dot_general(q_ref[:, sl], k_ref[:, sl], (((1,), (1,)), ((), ())),
                                preferred_element_type=F32) + bias
            _online_softmax_step(s, v, m_ref, l_ref, a_ref)

    @pl.when(ki == qi)
    def _():
        lf = lam_ref[...]
        lam = (jnp.exp(jnp.sum(lf[0:1] * lf[1:2], axis=-1, keepdims=True))
               - jnp.exp(jnp.sum(lf[2:3] * lf[3:4], axis=-1, keepdims=True)) + lam_init)
        o = a0[...] / l0[...] - lam * (a1[...] / l1[...])
        ms = jnp.mean(o * o, axis=-1, keepdims=True)
        o = o * lax.rsqrt(ms + NORM_EPS) * sg_ref[...] * (1.0 - lam_init)
        o_ref[...] = o.astype(o_ref.dtype)


def _diff_attention(proj, bias_tiles, lam_param, subln_gain, lam_init, bsz, seq):
    t = proj.shape[0]
    tt = ATTN_T
    nb = seq // tt
    qcol = 3 * FOX_W // DIFF_V_DIM
    kcol = qcol + DIFF_QK_W // DIFF_V_DIM
    vcol = kcol + DIFF_QK_W // DIFF_V_DIM
    return pl.pallas_call(
        functools.partial(_diff_kernel, lam_init=lam_init),
        grid=(bsz, DIFF_HEADS, nb, nb),
        in_specs=[
            pl.BlockSpec((tt, DIFF_V_DIM), lambda b, h, qi, ki: (b * nb + qi, qcol + h)),
            pl.BlockSpec((tt, DIFF_V_DIM), lambda b, h, qi, ki: (b * nb + jnp.minimum(ki, qi), kcol + h)),
            pl.BlockSpec((tt, DIFF_V_DIM), lambda b, h, qi, ki: (b * nb + jnp.minimum(ki, qi), vcol + h)),
            pl.BlockSpec((None, None, tt, tt), lambda b, h, qi, ki: (h, jnp.maximum(qi - ki, 0), 0, 0)),
            pl.BlockSpec((4, HEAD_DIM), lambda b, h, qi, ki: (0, 0)),
            pl.BlockSpec((1, DIFF_V_DIM), lambda b, h, qi, ki: (0, 0)),
        ],
        out_specs=pl.BlockSpec((tt, DIFF_V_DIM), lambda b, h, qi, ki: (b * nb + qi, h)),
        out_shape=jax.ShapeDtypeStruct((t, DIFF_V_W), BF16),
        scratch_shapes=[
            pltpu.VMEM((tt, 1), F32), pltpu.VMEM((tt, 1), F32), pltpu.VMEM((tt, DIFF_V_DIM), F32),
            pltpu.VMEM((tt, 1), F32), pltpu.VMEM((tt, 1), F32), pltpu.VMEM((tt, DIFF_V_DIM), F32),
        ],
        compiler_params=_cparams("parallel", "parallel", "parallel", "arbitrary"),
        name="diff_attention",
    )(proj, proj, proj, bias_tiles, lam_param, subln_gain.reshape(1, DIFF_V_DIM))


def _t5_bucket_np(dist):
    n = np.maximum(dist, 0)
    nf = np.maximum(n, 1).astype(np.float32)
    large = BUCKET_MAX_EXACT + (np.log(nf / np.float32(BUCKET_MAX_EXACT))
                                / np.float32(math.log(BUCKET_MAX_DIST / BUCKET_MAX_EXACT))
                                * np.float32(N_BUCKETS - BUCKET_MAX_EXACT)).astype(np.int32)
    large = np.minimum(large, N_BUCKETS - 1)
    return np.where(n < BUCKET_MAX_EXACT, n, large).astype(np.int32)


def _causal_bias_tiles(table, seq, tt):
    nb = seq // tt
    i = np.arange(tt)[:, None]
    j = np.arange(tt)[None, :]
    dist = np.arange(nb)[:, None, None] * tt + i - j
    tiles = table.T[:, _t5_bucket_np(dist)]
    return jnp.where(jnp.asarray(dist >= 0)[None], tiles, MASKED).astype(F32)


def _dilated_bias_tile(table, dil):
    span = DIL_BLOCK
    qi = np.arange(DIL_BLOCK)[:, None]
    kk = np.arange(2 * DIL_BLOCK)[None, :]
    step = qi + span - kk
    valid = (step >= 0) & (step <= span)
    tile = table.T[:, _t5_bucket_np(step * dil)]
    return jnp.where(jnp.asarray(valid)[None], tile, MASKED).astype(F32)


def _dilated_kernel(q_ref, kp_ref, kc_ref, vp_ref, vc_ref, bias_ref, o_ref, lse_ref):
    jb = pl.program_id(3)
    q = q_ref[...]
    dn = (((1,), (1,)), ((), ()))
    s_prev = lax.dot_general(q, kp_ref[...], dn, preferred_element_type=F32) + bias_ref[:, :DIL_BLOCK]
    s_prev = jnp.where(jb > 0, s_prev, MASKED)
    s_cur = lax.dot_general(q, kc_ref[...], dn, preferred_element_type=F32) + bias_ref[:, DIL_BLOCK:]
    m = jnp.maximum(jnp.max(s_prev, axis=-1, keepdims=True), jnp.max(s_cur, axis=-1, keepdims=True))
    p_prev = jnp.exp(s_prev - m)
    p_cur = jnp.exp(s_cur - m)
    l = jnp.sum(p_prev, axis=-1, keepdims=True) + jnp.sum(p_cur, axis=-1, keepdims=True)
    acc = (jnp.dot(p_prev.astype(BF16), vp_ref[...], preferred_element_type=F32)
           + jnp.dot(p_cur.astype(BF16), vc_ref[...], preferred_element_type=F32))
    o_ref[...] = acc / l
    lse_ref[...] = jnp.broadcast_to(m + jnp.log(l), lse_ref.shape)


def _dilated_pattern(qkv, bias_tile, dil, bsz, seq):
    t, w3 = qkv.shape
    n_sub = seq // dil
    nblk = n_sub // DIL_BLOCK
    cpr = w3 // HEAD_DIM
    view = qkv.reshape(bsz, n_sub, dil * w3)
    h16 = DIL_HEADS

    def prev(jb):
        return jnp.maximum(jb - 1, 0)

    blk = (None, DIL_BLOCK, HEAD_DIM)
    o, lse = pl.pallas_call(
        _dilated_kernel,
        grid=(bsz, DIL_HEADS, dil, nblk),
        in_specs=[
            pl.BlockSpec(blk, lambda b, h, r, jb: (b, jb, r * cpr + h)),
            pl.BlockSpec(blk, lambda b, h, r, jb: (b, prev(jb), r * cpr + h16 + h)),
            pl.BlockSpec(blk, lambda b, h, r, jb: (b, jb, r * cpr + h16 + h)),
            pl.BlockSpec(blk, lambda b, h, r, jb: (b, prev(jb), r * cpr + 2 * h16 + h)),
            pl.BlockSpec(blk, lambda b, h, r, jb: (b, jb, r * cpr + 2 * h16 + h)),
            pl.BlockSpec((None, DIL_BLOCK, 2 * DIL_BLOCK), lambda b, h, r, jb: (h, 0, 0)),
        ],
        out_specs=[
            pl.BlockSpec(blk, lambda b, h, r, jb: (b, jb, r * h16 + h)),
            pl.BlockSpec(blk, lambda b, h, r, jb: (b, jb, r * h16 + h)),
        ],
        out_shape=[
            jax.ShapeDtypeStruct((bsz, n_sub, dil * D_MODEL), F32),
            jax.ShapeDtypeStruct((bsz, n_sub, dil * D_MODEL), F32),
        ],
        compiler_params=_cparams("parallel", "parallel", "parallel", "parallel"),
        name=f"dilated_attention_d{dil}",
    )(view, view, view, view, view, bias_tile)
    return o.reshape(t, D_MODEL), lse.reshape(t, D_MODEL)


def _dilated_mix_kernel(o0, o1, o2, l0, l1, l2, out_ref):
    a, b, c = l0[...], l1[...], l2[...]
    m = jnp.maximum(jnp.maximum(a, b), c)
    ea, eb, ec = jnp.exp(a - m), jnp.exp(b - m), jnp.exp(c - m)
    den = ea + eb + ec
    out = (ea / den) * o0[...] + (eb / den) * o1[...] + (ec / den) * o2[...]
    out_ref[...] = out.astype(out_ref.dtype)


def _dilated_mix(outs, lses):
    t, d = outs[0].shape
    tm = 128
    spec = pl.BlockSpec((tm, d), lambda i: (i, 0))
    return pl.pallas_call(
        _dilated_mix_kernel,
        grid=(t // tm,),
        in_specs=[spec] * 6,
        out_specs=spec,
        out_shape=jax.ShapeDtypeStruct((t, d), BF16),
        compiler_params=_cparams("parallel"),
        name="dilated_mix",
    )(*outs, *lses)


def _topk_rows(s, val_ref, pick_ref, payload=None):
    n = s.shape[0]
    rows = lax.broadcasted_iota(jnp.int32, s.shape, 0)
    for k in range(PEER_TOPK):
        m = jnp.max(s, axis=0, keepdims=True)
        am = jnp.min(jnp.where(s == m, rows, n), axis=0, keepdims=True)
        hit = rows == am
        val_ref[k:k + 1, :] = m
        if payload is None:
            pick_ref[k:k + 1, :] = am
        else:
            pick_ref[k:k + 1, :] = jnp.sum(jnp.where(hit, payload, 0), axis=0, keepdims=True)
        s = jnp.where(hit, -jnp.inf, s)


def _route_kernel(q_ref, keys_ref, idx_ref, gate_ref, v0, i0, v1, i1, best_ref):
    dn = (((1,), (1,)), ((), ()))
    half = PEER_DKEY // 2
    s0 = lax.dot_general(keys_ref[0], q_ref[:, :half], dn, preferred_element_type=F32)
    s1 = lax.dot_general(keys_ref[1], q_ref[:, half:], dn, preferred_element_type=F32)
    _topk_rows(s0, v0, i0)
    _topk_rows(s1, v1, i1)
    cand_s = jnp.concatenate([v0[a:a + 1, :] + v1[...] for a in range(PEER_TOPK)], axis=0)
    cand_i = jnp.concatenate([i0[a:a + 1, :] * PEER_NKEYS + i1[...] for a in range(PEER_TOPK)], axis=0)
    _topk_rows(cand_s, best_ref, idx_ref, cand_i)
    best_s = best_ref[...]
    e = jnp.exp(best_s - best_s[0:1, :])
    gate_ref[...] = e / jnp.sum(e, axis=0, keepdims=True)


def _peer_route(qp, sub_keys):
    t = qp.shape[0]
    tm = ROUTE_TM
    out_spec = pl.BlockSpec((None, PEER_TOPK, tm), lambda i, h: (h, 0, i))
    return pl.pallas_call(
        _route_kernel,
        grid=(t // tm, PEER_HEADS),
        in_specs=[
            pl.BlockSpec((tm, PEER_DKEY), lambda i, h: (i, h)),
            pl.BlockSpec((2, PEER_NKEYS, PEER_DKEY // 2), lambda i, h: (0, 0, 0)),
        ],
        out_specs=[out_spec, out_spec],
        out_shape=[
            jax.ShapeDtypeStruct((PEER_HEADS, PEER_TOPK, t), jnp.int32),
            jax.ShapeDtypeStruct((PEER_HEADS, PEER_TOPK, t), F32),
        ],
        scratch_shapes=[
            pltpu.VMEM((PEER_TOPK, tm), F32), pltpu.VMEM((PEER_TOPK, tm), jnp.int32),
            pltpu.VMEM((PEER_TOPK, tm), F32), pltpu.VMEM((PEER_TOPK, tm), jnp.int32),
            pltpu.VMEM((PEER_TOPK, tm), F32),
        ],
        compiler_params=_cparams("parallel", "parallel"),
        name="peer_route",
    )(qp, sub_keys)


def _gelu_exact(a):
    return 0.5 * a * (1.0 + lax.erf(a * (2.0 ** -0.5)))


def _expert_kernel(idx_ref, idx_next_ref, x_ref, gain_ref, sc_ref, sh_ref, g2_ref, gate_ref,
                   u_hbm, v_hbm, o_ref, ubuf, vbuf, sems):
    s = pl.program_id(0)
    n_steps = pl.num_programs(0)
    tp = x_ref.shape[0]
    rows = tp * PEER_PICKS
    slot = s % 2

    def start_gather(iref, to_slot):
        groups_per_token = PEER_PICKS // SUBLANES

        def body(r8, carry):
            ti = lax.shift_right_logical(r8, groups_per_token.bit_length() - 1)
            col0 = (r8 & (groups_per_token - 1)) * SUBLANES
            for k in range(SUBLANES):
                r = r8 * SUBLANES + k
                e = iref[0, ti, col0 + k]
                pltpu.make_async_copy(u_hbm.at[pl.ds(e, 1)], ubuf.at[to_slot, pl.ds(r, 1)],
                                      sems.at[0, to_slot]).start()
                pltpu.make_async_copy(v_hbm.at[pl.ds(e, 1)], vbuf.at[to_slot, pl.ds(r, 1)],
                                      sems.at[1, to_slot]).start()
            return carry
        lax.fori_loop(0, rows // SUBLANES, body, 0)

    @pl.when(s == 0)
    def _():
        start_gather(idx_ref, 0)

    @pl.when(s + 1 < n_steps)
    def _():
        start_gather(idx_next_ref, 1 - slot)

    pltpu.make_async_copy(u_hbm.at[pl.ds(0, rows)], ubuf.at[slot], sems.at[0, slot]).wait()
    pltpu.make_async_copy(v_hbm.at[pl.ds(0, rows)], vbuf.at[slot], sems.at[1, slot]).wait()

    x = x_ref[...]
    h = _modulated_norm(x, gain_ref[...], sc_ref[...], sh_ref[...])
    d = x.shape[1]
    nchunk = d // LANES
    ngroup = PEER_PICKS // SUBLANES
    eye = (lax.broadcasted_iota(jnp.int32, (PEER_PICKS, PEER_PICKS), 0)
           == lax.broadcasted_iota(jnp.int32, (PEER_PICKS, PEER_PICKS), 1))
    ys = []
    for ti in range(tp):
        base = ti * PEER_PICKS
        hb = jnp.broadcast_to(h[ti:ti + 1, :], (SUBLANES, d))
        parts = []
        for g in range(ngroup):
            u = ubuf[slot, pl.ds(base + g * SUBLANES, SUBLANES), :]
            p = u * hb
            acc = p[:, 0:LANES]
            for c in range(1, nchunk):
                acc = acc + p[:, c * LANES:(c + 1) * LANES]
            parts.append(acc)
        act = jnp.sum(jnp.concatenate(parts, axis=0), axis=-1, keepdims=True)
        gate_col = jnp.sum(jnp.where(eye, gate_ref[ti:ti + 1, :], 0.0), axis=-1, keepdims=True)
        a = gate_col * _gelu_exact(act)
        yacc = jnp.zeros((SUBLANES, d), F32)
        for g in range(ngroup):
            v = vbuf[slot, pl.ds(base + g * SUBLANES, SUBLANES), :]
            yacc = yacc + v * a[g * SUBLANES:(g + 1) * SUBLANES, :]
        ys.append(jnp.sum(yacc, axis=0, keepdims=True))
    y = jnp.concatenate(ys, axis=0)
    o_ref[...] = x + g2_ref[...] * y


def _peer_experts(x2, gain, sc, sh, g2, idx, gates, table_u, table_v, seq):
    t, d = x2.shape
    tp = PEER_TP
    n_steps = t // tp
    idx3 = idx.reshape(n_steps, tp, PEER_PICKS)
    per_b = seq // tp
    smem_blk = (1, tp, PEER_PICKS)
    return pl.pallas_call(
        _expert_kernel,
        grid=(n_steps,),
        in_specs=[
            pl.BlockSpec(smem_blk, lambda s: (s, 0, 0), memory_space=pltpu.SMEM),
            pl.BlockSpec(smem_blk, lambda s: (jnp.minimum(s + 1, n_steps - 1), 0, 0),
                         memory_space=pltpu.SMEM),
            pl.BlockSpec((tp, d), lambda s: (s, 0)),
            pl.BlockSpec((1, d), lambda s: (0, 0)),
            pl.BlockSpec((None, 1, d), lambda s: (s // per_b, 0, 0)),
            pl.BlockSpec((None, 1, d), lambda s: (s // per_b, 0, 0)),
            pl.BlockSpec((None, 1, d), lambda s: (s // per_b, 0, 0)),
            pl.BlockSpec((tp, PEER_PICKS), lambda s: (s, 0)),
            pl.BlockSpec(memory_space=pl.ANY),
            pl.BlockSpec(memory_space=pl.ANY),
        ],
        out_specs=pl.BlockSpec((tp, d), lambda s: (s, 0)),
        out_shape=jax.ShapeDtypeStruct((t, d), F32),
        scratch_shapes=[
            pltpu.VMEM((2, tp * PEER_PICKS, d), F32),
            pltpu.VMEM((2, tp * PEER_PICKS, d), F32),
            pltpu.SemaphoreType.DMA((2, 2)),
        ],
        compiler_params=_cparams("arbitrary"),
        name="peer_experts",
    )(idx3, idx3, x2, gain.reshape(1, d), sc, sh, g2, gates, table_u, table_v)


def _peer_ffn(x2, gain, sc, sh, g2, w_query, sub_keys, table_u, table_v, seq):
    t, d = x2.shape
    no_gain = jnp.ones((1, w_query.shape[1]), F32)
    qp = _norm_proj(x2, gain, sc, sh, w_query.astype(BF16), no_gain, (), seq)
    idx, gates = _peer_route(qp, sub_keys.astype(BF16))
    idx = idx.transpose(2, 0, 1).reshape(t, PEER_PICKS)
    gates = gates.transpose(2, 0, 1).reshape(t, PEER_PICKS)
    return _peer_experts(x2, gain, sc, sh, g2, idx, gates, table_u, table_v, seq)


def _tile_gain(gain, heads, scale=1.0):
    return jnp.tile(gain * scale, heads)


def _even_layer(x2, mods, norm_gain, w_in, b_forget, fox_qk_gain, diff_qk_gain, diff_lambda,
                diff_subln_gain, w_out, diff_bias, lam_init, bsz, seq):
    sh1, sc1, g1 = mods
    w_main = jnp.concatenate([w_in[:, :3 * FOX_W], w_in[:, 3 * FOX_W + FOX_HEADS:]], axis=1).astype(BF16)
    w_forget = jnp.zeros((D_MODEL, LANES), F32).at[:, :FOX_HEADS].set(
        w_in[:, 3 * FOX_W:3 * FOX_W + FOX_HEADS]).astype(BF16)
    ones = jnp.ones((FOX_W,), F32)
    head_gain = jnp.concatenate([
        _tile_gain(fox_qk_gain[0], FOX_HEADS, ATTN_SCALE), _tile_gain(fox_qk_gain[1], FOX_HEADS), ones,
        _tile_gain(diff_qk_gain[0], 2 * DIFF_HEADS, ATTN_SCALE), _tile_gain(diff_qk_gain[1], 2 * DIFF_HEADS),
        ones]).reshape(1, -1)
    proj, f_side = _norm_proj(x2, norm_gain, sc1, sh1, w_main, head_gain, (0, 1, 3, 4), seq,
                              w_side=w_forget)
    fcol, frow = _forget_cumsum(f_side, b_forget, bsz, seq)
    fox_o = _fox_attention(proj, frow, fcol, bsz, seq)
    bias_tiles = _causal_bias_tiles(diff_bias, seq, ATTN_T)
    diff_o = _diff_attention(proj, bias_tiles, diff_lambda, diff_subln_gain, lam_init, bsz, seq)
    mixed = jnp.concatenate([fox_o, diff_o], axis=-1)
    return _out_proj(mixed, w_out.astype(BF16), x2, g1, seq)


def _odd_layer(x2, mods, norm_gain, w_qkv, qk_gain, w_out, bias_table, bsz, seq):
    sh1, sc1, g1 = mods
    ones = jnp.ones((D_MODEL,), F32)
    head_gain = jnp.concatenate([
        _tile_gain(qk_gain[0], DIL_HEADS, ATTN_SCALE), _tile_gain(qk_gain[1], DIL_HEADS), ones]).reshape(1, -1)
    qkv = _norm_proj(x2, norm_gain, sc1, sh1, w_qkv.astype(BF16), head_gain, (0, 1, 2, 3), seq)
    outs, lses = [], []
    for _, dil in DIL_PATTERNS:
        o, lse = _dilated_pattern(qkv, _dilated_bias_tile(bias_table, dil), dil, bsz, seq)
        outs.append(o)
        lses.append(lse)
    mixed = _dilated_mix(outs, lses)
    return _out_proj(mixed, w_out.astype(BF16), x2, g1, seq)


def kernel(x, c, rel_bias, norm_gain, w_ada, b_ada, even_w_in, even_b_forget, even_fox_qk_gain,
           even_diff_qk_gain, even_diff_lambda, even_diff_subln_gain, even_w_out, odd_w_qkv,
           odd_qk_gain, odd_w_out, peer_w_query, peer_sub_keys, peer_u, peer_v):
    bsz, seq, d = x.shape
    x2 = x.reshape(bsz * seq, d)
    mod = _adaln(c, w_ada, b_ada)
    for i in range(DEPTH):
        sh1, sc1, g1, sh2, sc2, g2 = [m.reshape(bsz, 1, d) for m in jnp.split(mod[i], 6, axis=-1)]
        j = i // 2
        if i % 2 == 0:
            lam_init = 0.8 - 0.6 * math.exp(-0.3 * i)
            x2 = _even_layer(x2, (sh1, sc1, g1), norm_gain[i, 0], even_w_in[j], even_b_forget[j],
                             even_fox_qk_gain[j], even_diff_qk_gain[j], even_diff_lambda[j],
                             even_diff_subln_gain[j], even_w_out[j], rel_bias[:, :DIFF_HEADS],
                             lam_init, bsz, seq)
        else:
            x2 = _odd_layer(x2, (sh1, sc1, g1), norm_gain[i, 0], odd_w_qkv[j], odd_qk_gain[j],
                            odd_w_out[j], rel_bias, bsz, seq)
        x2 = _peer_ffn(x2, norm_gain[i, 1], sc2, sh2, g2, peer_w_query[i], peer_sub_keys[i],
                       peer_u[i], peer_v[i], seq)
    return x2.reshape(bsz, seq, d)
```

```python
import functools
import math

import numpy as np
import jax
import jax.numpy as jnp
from jax import lax
from jax.experimental import pallas as pl
from jax.experimental.pallas import tpu as pltpu

F32 = jnp.float32
BF16 = jnp.bfloat16

D_MODEL = 2048
DEPTH = 2
HEAD_DIM = 128
FOX_HEADS = 8
DIFF_HEADS = 4
DIFF_V_DIM = 2 * HEAD_DIM
DIL_HEADS = D_MODEL // HEAD_DIM
DIL_PATTERNS = ((128, 1), (512, 4), (2048, 16))
DIL_BLOCK = 128
N_BUCKETS = 32
BUCKET_MAX_EXACT = 16
BUCKET_MAX_DIST = 2048
PEER_HEADS = 8
PEER_NKEYS = 128
PEER_EXPERTS = PEER_NKEYS * PEER_NKEYS
PEER_DKEY = 256
PEER_TOPK = 16
PEER_PICKS = PEER_HEADS * PEER_TOPK
NORM_EPS = 1e-6
FOX_W = FOX_HEADS * HEAD_DIM
DIFF_QK_W = DIFF_HEADS * 2 * HEAD_DIM
DIFF_V_W = DIFF_HEADS * DIFF_V_DIM
ATTN_SCALE = HEAD_DIM ** -0.5
MASKED = -1e30

LANES = 128
SUBLANES = 8
VMEM_LIMIT = 48 * 1024 * 1024

PROJ_TM = 512
PROJ_TN = 1024
ATTN_T = 512
ROUTE_TM = 256
PEER_TP = 16


def _cparams(*sem):
    return pltpu.CompilerParams(dimension_semantics=sem, vmem_limit_bytes=VMEM_LIMIT)


def _adaln_kernel(c_ref, w_ref, b_ref, o_ref):
    c = c_ref[...]
    cond = c * jax.nn.sigmoid(c)
    o_ref[...] = jnp.dot(cond.astype(BF16), w_ref[...].astype(BF16),
                         preferred_element_type=F32) + b_ref[...]


def _adaln(c, w_ada, b_ada):
    depth, d, n = w_ada.shape
    bsz = c.shape[0]
    tn = 1024
    return pl.pallas_call(
        _adaln_kernel,
        grid=(depth, n // tn),
        in_specs=[
            pl.BlockSpec((bsz, d), lambda i, j: (0, 0)),
            pl.BlockSpec((None, d, tn), lambda i, j: (i, 0, j)),
            pl.BlockSpec((None, 1, tn), lambda i, j: (i, 0, j)),
        ],
        out_specs=pl.BlockSpec((None, bsz, tn), lambda i, j: (i, 0, j)),
        out_shape=jax.ShapeDtypeStruct((depth, bsz, n), F32),
        compiler_params=_cparams("parallel", "parallel"),
        name="adaln",
    )(c, w_ada, b_ada.reshape(depth, 1, n))


def _modulated_norm(x, gain, sc, sh):
    ms = jnp.mean(x * x, axis=-1, keepdims=True)
    y = x * lax.rsqrt(ms + NORM_EPS) * gain
    return y * (1.0 + sc) + sh


def _norm_proj_kernel(*refs, norm_tiles, has_side):
    if has_side:
        x_ref, gain_ref, sc_ref, sh_ref, w_ref, hg_ref, ws_ref, o_ref, side_ref, h_ref = refs
    else:
        x_ref, gain_ref, sc_ref, sh_ref, w_ref, hg_ref, o_ref, h_ref = refs
    j = pl.program_id(1)

    @pl.when(j == 0)
    def _():
        h = _modulated_norm(x_ref[...], gain_ref[...], sc_ref[...], sh_ref[...])
        h_ref[...] = h.astype(BF16)
        if has_side:
            side_ref[...] = jnp.dot(h_ref[...], ws_ref[...], preferred_element_type=F32)

    acc = jnp.dot(h_ref[...], w_ref[...], preferred_element_type=F32)
    tn = acc.shape[1]

    def write_normed():
        for g in range(tn // HEAD_DIM):
            sl = slice(g * HEAD_DIM, (g + 1) * HEAD_DIM)
            blk = acc[:, sl]
            ms = jnp.mean(blk * blk, axis=-1, keepdims=True)
            o_ref[:, sl] = (blk * lax.rsqrt(ms + NORM_EPS) * hg_ref[:, sl]).astype(o_ref.dtype)

    def write_raw():
        o_ref[...] = acc.astype(o_ref.dtype)

    if not norm_tiles:
        write_raw()
    else:
        is_norm = functools.reduce(jnp.logical_or, [j == t for t in norm_tiles])
        pl.when(is_norm)(write_normed)
        pl.when(jnp.logical_not(is_norm))(write_raw)


def _norm_proj(x2, gain, sc, sh, w, head_gain, norm_tiles, seq, w_side=None):
    t, d = x2.shape
    n = w.shape[1]
    tm, tn = PROJ_TM, PROJ_TN
    per_b = seq // tm
    has_side = w_side is not None
    in_specs = [
        pl.BlockSpec((tm, d), lambda i, j: (i, 0)),
        pl.BlockSpec((1, d), lambda i, j: (0, 0)),
        pl.BlockSpec((None, 1, d), lambda i, j: (i // per_b, 0, 0)),
        pl.BlockSpec((None, 1, d), lambda i, j: (i // per_b, 0, 0)),
        pl.BlockSpec((d, tn), lambda i, j: (0, j)),
        pl.BlockSpec((1, tn), lambda i, j: (0, j)),
    ]
    args = [x2, gain.reshape(1, d), sc, sh, w, head_gain]
    out_specs = pl.BlockSpec((tm, tn), lambda i, j: (i, j))
    out_shape = jax.ShapeDtypeStruct((t, n), BF16)
    if has_side:
        ns = w_side.shape[1]
        in_specs.append(pl.BlockSpec((d, ns), lambda i, j: (0, 0)))
        args.append(w_side)
        out_specs = [out_specs, pl.BlockSpec((tm, ns), lambda i, j: (i, 0))]
        out_shape = [out_shape, jax.ShapeDtypeStruct((t, ns), F32)]
    return pl.pallas_call(
        functools.partial(_norm_proj_kernel, norm_tiles=tuple(norm_tiles), has_side=has_side),
        grid=(t // tm, n // tn),
        in_specs=in_specs,
        out_specs=out_specs,
        out_shape=out_shape,
        scratch_shapes=[pltpu.VMEM((tm, d), BF16)],
        compiler_params=_cparams("parallel", "arbitrary"),
        name="norm_proj",
    )(*args)


def _out_proj_kernel(a_ref, w_ref, x_ref, g_ref, o_ref):
    y = jnp.dot(a_ref[...], w_ref[...], preferred_element_type=F32)
    o_ref[...] = x_ref[...] + g_ref[...] * y


def _out_proj(a, w, x2, g, seq):
    t, k = a.shape
    n = w.shape[1]
    tm, tn = PROJ_TM, PROJ_TN
    per_b = seq // tm
    return pl.pallas_call(
        _out_proj_kernel,
        grid=(t // tm, n // tn),
        in_specs=[
            pl.BlockSpec((tm, k), lambda i, j: (i, 0)),
            pl.BlockSpec((k, tn), lambda i, j: (0, j)),
            pl.BlockSpec((tm, tn), lambda i, j: (i, j)),
            pl.BlockSpec((None, 1, tn), lambda i, j: (i // per_b, 0, j)),
        ],
        out_specs=pl.BlockSpec((tm, tn), lambda i, j: (i, j)),
        out_shape=jax.ShapeDtypeStruct((t, n), F32),
        compiler_params=_cparams("parallel", "parallel"),
        name="out_proj",
    )(a, w, x2, g)


def _forget_kernel(f_ref, b_ref, col_ref, row_ref):
    z = f_ref[...] + b_ref[...]
    lf = jnp.minimum(z, 0.0) - jnp.log1p(jnp.exp(-jnp.abs(z)))
    s = lf.shape[0]
    pos = lax.broadcasted_iota(jnp.int32, lf.shape, 0)
    d = 1
    while d < s:
        lf = lf + jnp.where(pos >= d, pltpu.roll(lf, d, axis=0), 0.0)
        d *= 2
    col_ref[...] = lf
    row_ref[...] = lf.T[:FOX_HEADS, :]


def _forget_cumsum(f_side, b_forget, bsz, seq):
    bpad = jnp.zeros((1, LANES), F32).at[0, :FOX_HEADS].set(b_forget)
    col, row = pl.pallas_call(
        _forget_kernel,
        grid=(bsz,),
        in_specs=[
            pl.BlockSpec((None, seq, LANES), lambda b: (b, 0, 0)),
            pl.BlockSpec((1, LANES), lambda b: (0, 0)),
        ],
        out_specs=[
            pl.BlockSpec((None, seq, LANES), lambda b: (b, 0, 0)),
            pl.BlockSpec((None, FOX_HEADS, seq), lambda b: (b, 0, 0)),
        ],
        out_shape=[
            jax.ShapeDtypeStruct((bsz, seq, LANES), F32),
            jax.ShapeDtypeStruct((bsz, FOX_HEADS, seq), F32),
        ],
        compiler_params=_cparams("parallel"),
        name="forget_cumsum",
    )(f_side.reshape(bsz, seq, LANES), bpad)
    return col, row.reshape(bsz, FOX_HEADS, 1, seq)


def _online_softmax_step(s, v, m_ref, l_ref, acc_ref):
    m_prev = m_ref[...]
    m_new = jnp.maximum(m_prev, jnp.max(s, axis=-1, keepdims=True))
    alpha = jnp.exp(m_prev - m_new)
    p = jnp.exp(s - m_new)
    l_ref[...] = alpha * l_ref[...] + jnp.sum(p, axis=-1, keepdims=True)
    acc_ref[...] = alpha * acc_ref[...] + jnp.dot(p.astype(v.dtype), v, preferred_element_type=F32)
    m_ref[...] = m_new


def _fox_kernel(q_ref, k_ref, v_ref, frow_ref, fcol_ref, o_ref, m_ref, l_ref, acc_ref, fq_ref):
    h = pl.program_id(1)
    qi = pl.program_id(2)
    ki = pl.program_id(3)
    tq, tk = q_ref.shape[0], k_ref.shape[0]

    @pl.when(ki == 0)
    def _():
        m_ref[...] = jnp.full_like(m_ref, MASKED)
        l_ref[...] = jnp.zeros_like(l_ref)
        acc_ref[...] = jnp.zeros_like(acc_ref)
        lane = lax.broadcasted_iota(jnp.int32, fcol_ref.shape, 1)
        fq_ref[...] = jnp.sum(jnp.where(lane == h, fcol_ref[...], 0.0), axis=-1, keepdims=True)

    @pl.when(ki <= qi)
    def _():
        s = lax.dot_general(q_ref[...], k_ref[...], (((1,), (1,)), ((), ())),
                            preferred_element_type=F32)
        s = s + fq_ref[...] - frow_ref[...]
        qpos = qi * tq + lax.broadcasted_iota(jnp.int32, s.shape, 0)
        kpos = ki * tk + lax.broadcasted_iota(jnp.int32, s.shape, 1)
        s = jnp.where(qpos >= kpos, s, MASKED)
        _online_softmax_step(s, v_ref[...], m_ref, l_ref, acc_ref)

    @pl.when(ki == qi)
    def _():
        o_ref[...] = (acc_ref[...] / l_ref[...]).astype(o_ref.dtype)


def _fox_attention(proj, frow, fcol, bsz, seq):
    t = proj.shape[0]
    tt = ATTN_T
    nb = seq // tt
    kcol = FOX_W // HEAD_DIM
    return pl.pallas_call(
        _fox_kernel,
        grid=(bsz, FOX_HEADS, nb, nb),
        in_specs=[
            pl.BlockSpec((tt, HEAD_DIM), lambda b, h, qi, ki: (b * nb + qi, h)),
            pl.BlockSpec((tt, HEAD_DIM), lambda b, h, qi, ki: (b * nb + jnp.minimum(ki, qi), kcol + h)),
            pl.BlockSpec((tt, HEAD_DIM), lambda b, h, qi, ki: (b * nb + jnp.minimum(ki, qi), 2 * kcol + h)),
            pl.BlockSpec((None, None, 1, tt), lambda b, h, qi, ki: (b, h, 0, jnp.minimum(ki, qi))),
            pl.BlockSpec((None, tt, LANES), lambda b, h, qi, ki: (b, qi, 0)),
        ],
        out_specs=pl.BlockSpec((tt, HEAD_DIM), lambda b, h, qi, ki: (b * nb + qi, h)),
        out_shape=jax.ShapeDtypeStruct((t, FOX_W), BF16),
        scratch_shapes=[
            pltpu.VMEM((tt, 1), F32), pltpu.VMEM((tt, 1), F32),
            pltpu.VMEM((tt, HEAD_DIM), F32), pltpu.VMEM((tt, 1), F32),
        ],
        compiler_params=_cparams("parallel", "parallel", "parallel", "arbitrary"),
        name="fox_attention",
    )(proj, proj, proj, frow, fcol)


def _diff_kernel(q_ref, k_ref, v_ref, bias_ref, lam_ref, sg_ref, o_ref,
                 m0, l0, a0, m1, l1, a1, *, lam_init):
    qi = pl.program_id(2)
    ki = pl.program_id(3)

    @pl.when(ki == 0)
    def _():
        for m_ref, l_ref, a_ref in ((m0, l0, a0), (m1, l1, a1)):
            m_ref[...] = jnp.full_like(m_ref, MASKED)
            l_ref[...] = jnp.zeros_like(l_ref)
            a_ref[...] = jnp.zeros_like(a_ref)

    @pl.when(ki <= qi)
    def _():
        bias = bias_ref[...]
        v = v_ref[...]
        for mi, (m_ref, l_ref, a_ref) in enumerate(((m0, l0, a0), (m1, l1, a1))):
            sl = slice(mi * HEAD_DIM, (mi + 1) * HEAD_DIM)
            s = lax.dot_general(q_ref[:, sl], k_ref[:, sl], (((1,), (1,)), ((), ())),
                                preferred_element_type=F32) + bias
            _online_softmax_step(s, v, m_ref, l_ref, a_ref)

    @pl.when(ki == qi)
    def _():
        lf = lam_ref[...]
        lam = (jnp.exp(jnp.sum(lf[0:1] * lf[1:2], axis=-1, keepdims=True))
               - jnp.exp(jnp.sum(lf[2:3] * lf[3:4], axis=-1, keepdims=True)) + lam_init)
        o = a0[...] / l0[...] - lam * (a1[...] / l1[...])
        ms = jnp.mean(o * o, axis=-1, keepdims=True)
        o = o * lax.rsqrt(ms + NORM_EPS) * sg_ref[...] * (1.0 - lam_init)
        o_ref[...] = o.astype(o_ref.dtype)


def _diff_attention(proj, bias_tiles, lam_param, subln_gain, lam_init, bsz, seq):
    t = proj.shape[0]
    tt = ATTN_T
    nb = seq // tt
    qcol = 3 * FOX_W // DIFF_V_DIM
    kcol = qcol + DIFF_QK_W // DIFF_V_DIM
    vcol = kcol + DIFF_QK_W // DIFF_V_DIM
    return pl.pallas_call(
        functools.partial(_diff_kernel, lam_init=lam_init),
        grid=(bsz, DIFF_HEADS, nb, nb),
        in_specs=[
            pl.BlockSpec((tt, DIFF_V_DIM), lambda b, h, qi, ki: (b * nb + qi, qcol + h)),
            pl.BlockSpec((tt, DIFF_V_DIM), lambda b, h, qi, ki: (b * nb + jnp.minimum(ki, qi), kcol + h)),
            pl.BlockSpec((tt, DIFF_V_DIM), lambda b, h, qi, ki: (b * nb + jnp.minimum(ki, qi), vcol + h)),
            pl.BlockSpec((None, None, tt, tt), lambda b, h, qi, ki: (h, jnp.maximum(qi - ki, 0), 0, 0)),
            pl.BlockSpec((4, HEAD_DIM), lambda b, h, qi, ki: (0, 0)),
            pl.BlockSpec((1, DIFF_V_DIM), lambda b, h, qi, ki: (0, 0)),
        ],
        out_specs=pl.BlockSpec((tt, DIFF_V_DIM), lambda b, h, qi, ki: (b * nb + qi, h)),
        out_shape=jax.ShapeDtypeStruct((t, DIFF_V_W), BF16),
        scratch_shapes=[
            pltpu.VMEM((tt, 1), F32), pltpu.VMEM((tt, 1), F32), pltpu.VMEM((tt, DIFF_V_DIM), F32),
            pltpu.VMEM((tt, 1), F32), pltpu.VMEM((tt, 1), F32), pltpu.VMEM((tt, DIFF_V_DIM), F32),
        ],
        compiler_params=_cparams("parallel", "parallel", "parallel", "arbitrary"),
        name="diff_attention",
    )(proj, proj, proj, bias_tiles, lam_param, subln_gain.reshape(1, DIFF_V_DIM))


def _t5_bucket_np(dist):
    n = np.maximum(dist, 0)
    nf = np.maximum(n, 1).astype(np.float32)
    large = BUCKET_MAX_EXACT + (np.log(nf / np.float32(BUCKET_MAX_EXACT))
                                / np.float32(math.log(BUCKET_MAX_DIST / BUCKET_MAX_EXACT))
                                * np.float32(N_BUCKETS - BUCKET_MAX_EXACT)).astype(np.int32)
    large = np.minimum(large, N_BUCKETS - 1)
    return np.where(n < BUCKET_MAX_EXACT, n, large).astype(np.int32)


def _bucket_thresholds():
    buckets = _t5_bucket_np(np.arange(BUCKET_MAX_DIST + 1))
    assert np.all(np.diff(buckets) >= 0)
    return [int(np.argmax(buckets >= b)) for b in range(N_BUCKETS)]


_BUCKET_THRESHOLDS = _bucket_thresholds()


def _bias_from_distance(dist, valid, table_ref, h):
    val = jnp.full(dist.shape, table_ref[0, h], F32)
    for b in range(1, N_BUCKETS):
        val = jnp.where(dist >= _BUCKET_THRESHOLDS[b], table_ref[b, h], val)
    return jnp.where(valid, val, MASKED)


def _causal_bias_kernel(table_ref, o_ref):
    h, delta = pl.program_id(0), pl.program_id(1)
    tt = o_ref.shape[0]
    dist = (delta * tt + lax.broadcasted_iota(jnp.int32, o_ref.shape, 0)
            - lax.broadcasted_iota(jnp.int32, o_ref.shape, 1))
    o_ref[...] = _bias_from_distance(dist, dist >= 0, table_ref, h)


def _causal_bias_tiles(table, seq, tt):
    nb = seq // tt
    heads = table.shape[1]
    return pl.pallas_call(
        _causal_bias_kernel,
        grid=(heads, nb),
        in_specs=[pl.BlockSpec(memory_space=pltpu.SMEM)],
        out_specs=pl.BlockSpec((None, None, tt, tt), lambda h, dlt: (h, dlt, 0, 0)),
        out_shape=jax.ShapeDtypeStruct((heads, nb, tt, tt), F32),
        compiler_params=_cparams("parallel", "parallel"),
        name="causal_bias_tiles",
    )(table)


def _dilated_bias_kernel(table_ref, o_ref, *, dil):
    h = pl.program_id(0)
    span = DIL_BLOCK
    step = (lax.broadcasted_iota(jnp.int32, o_ref.shape, 0) + span
            - lax.broadcasted_iota(jnp.int32, o_ref.shape, 1))
    valid = jnp.logical_and(step >= 0, step <= span)
    o_ref[...] = _bias_from_distance(step * dil, valid, table_ref, h)


def _dilated_bias_tile(table, dil):
    heads = table.shape[1]
    return pl.pallas_call(
        functools.partial(_dilated_bias_kernel, dil=dil),
        grid=(heads,),
        in_specs=[pl.BlockSpec(memory_space=pltpu.SMEM)],
        out_specs=pl.BlockSpec((None, DIL_BLOCK, 2 * DIL_BLOCK), lambda h: (h, 0, 0)),
        out_shape=jax.ShapeDtypeStruct((heads, DIL_BLOCK, 2 * DIL_BLOCK), F32),
        compiler_params=_cparams("parallel"),
        name=f"dilated_bias_d{dil}",
    )(table)


def _dilated_kernel(q_ref, kp_ref, kc_ref, vp_ref, vc_ref, bias_ref, o_ref, lse_ref):
    jb = pl.program_id(3)
    q = q_ref[...]
    dn = (((1,), (1,)), ((), ()))
    s_prev = lax.dot_general(q, kp_ref[...], dn, preferred_element_type=F32) + bias_ref[:, :DIL_BLOCK]
    s_prev = jnp.where(jb > 0, s_prev, MASKED)
    s_cur = lax.dot_general(q, kc_ref[...], dn, preferred_element_type=F32) + bias_ref[:, DIL_BLOCK:]
    m = jnp.maximum(jnp.max(s_prev, axis=-1, keepdims=True), jnp.max(s_cur, axis=-1, keepdims=True))
    p_prev = jnp.exp(s_prev - m)
    p_cur = jnp.exp(s_cur - m)
    l = jnp.sum(p_prev, axis=-1, keepdims=True) + jnp.sum(p_cur, axis=-1, keepdims=True)
    acc = (jnp.dot(p_prev.astype(BF16), vp_ref[...], preferred_element_type=F32)
           + jnp.dot(p_cur.astype(BF16), vc_ref[...], preferred_element_type=F32))
    o_ref[...] = acc / l
    lse_ref[...] = jnp.broadcast_to(m + jnp.log(l), lse_ref.shape)


def _dilated_pattern(qkv, bias_tile, dil, bsz, seq):
    t, w3 = qkv.shape
    n_sub = seq // dil
    nblk = n_sub // DIL_BLOCK
    cpr = w3 // HEAD_DIM
    view = qkv.reshape(bsz, n_sub, dil * w3)
    h16 = DIL_HEADS

    def prev(jb):
        return jnp.maximum(jb - 1, 0)

    blk = (None, DIL_BLOCK, HEAD_DIM)
    o, lse = pl.pallas_call(
        _dilated_kernel,
        grid=(bsz, DIL_HEADS, dil, nblk),
        in_specs=[
            pl.BlockSpec(blk, lambda b, h, r, jb: (b, jb, r * cpr + h)),
            pl.BlockSpec(blk, lambda b, h, r, jb: (b, prev(jb), r * cpr + h16 + h)),
            pl.BlockSpec(blk, lambda b, h, r, jb: (b, jb, r * cpr + h16 + h)),
            pl.BlockSpec(blk, lambda b, h, r, jb: (b, prev(jb), r * cpr + 2 * h16 + h)),
            pl.BlockSpec(blk, lambda b, h, r, jb: (b, jb, r * cpr + 2 * h16 + h)),
            pl.BlockSpec((None, DIL_BLOCK, 2 * DIL_BLOCK), lambda b, h, r, jb: (h, 0, 0)),
        ],
        out_specs=[
            pl.BlockSpec(blk, lambda b, h, r, jb: (b, jb, r * h16 + h)),
            pl.BlockSpec(blk, lambda b, h, r, jb: (b, jb, r * h16 + h)),
        ],
        out_shape=[
            jax.ShapeDtypeStruct((bsz, n_sub, dil * D_MODEL), F32),
            jax.ShapeDtypeStruct((bsz, n_sub, dil * D_MODEL), F32),
        ],
        compiler_params=_cparams("parallel", "parallel", "parallel", "parallel"),
        name=f"dilated_attention_d{dil}",
    )(view, view, view, view, view, bias_tile)
    return o.reshape(t, D_MODEL), lse.reshape(t, D_MODEL)


def _dilated_mix_kernel(o0, o1, o2, l0, l1, l2, out_ref):
    a, b, c = l0[...], l1[...], l2[...]
    m = jnp.maximum(jnp.maximum(a, b), c)
    ea, eb, ec = jnp.exp(a - m), jnp.exp(b - m), jnp.exp(c - m)
    den = ea + eb + ec
    out = (ea / den) * o0[...] + (eb / den) * o1[...] + (ec / den) * o2[...]
    out_ref[...] = out.astype(out_ref.dtype)


def _dilated_mix(outs, lses):
    t, d = outs[0].shape
    tm = 128
    spec = pl.BlockSpec((tm, d), lambda i: (i, 0))
    return pl.pallas_call(
        _dilated_mix_kernel,
        grid=(t // tm,),
        in_specs=[spec] * 6,
        out_specs=spec,
        out_shape=jax.ShapeDtypeStruct((t, d), BF16),
        compiler_params=_cparams("parallel"),
        name="dilated_mix",
    )(*outs, *lses)


def _topk_rows(s, val_ref, pick_ref, payload=None):
    n = s.shape[0]
    rows = lax.broadcasted_iota(jnp.int32, s.shape, 0)
    for k in range(PEER_TOPK):
        m = jnp.max(s, axis=0, keepdims=True)
        am = jnp.min(jnp.where(s == m, rows, n), axis=0, keepdims=True)
        hit = rows == am
        val_ref[k:k + 1, :] = m
        if payload is None:
            pick_ref[k:k + 1, :] = am
        else:
            pick_ref[k:k + 1, :] = jnp.sum(jnp.where(hit, payload, 0), axis=0, keepdims=True)
        s = jnp.where(hit, -jnp.inf, s)


def _route_kernel(q_ref, keys_ref, idx_ref, gate_ref, v0, i0, v1, i1, best_ref):
    dn = (((1,), (1,)), ((), ()))
    half = PEER_DKEY // 2
    s0 = lax.dot_general(keys_ref[0], q_ref[:, :half], dn, preferred_element_type=F32)
    s1 = lax.dot_general(keys_ref[1], q_ref[:, half:], dn, preferred_element_type=F32)
    _topk_rows(s0, v0, i0)
    _topk_rows(s1, v1, i1)
    cand_s = jnp.concatenate([v0[a:a + 1, :] + v1[...] for a in range(PEER_TOPK)], axis=0)
    cand_i = jnp.concatenate([i0[a:a + 1, :] * PEER_NKEYS + i1[...] for a in range(PEER_TOPK)], axis=0)
    _topk_rows(cand_s, best_ref, idx_ref, cand_i)
    best_s = best_ref[...]
    e = jnp.exp(best_s - best_s[0:1, :])
    gate_ref[...] = e / jnp.sum(e, axis=0, keepdims=True)


def _peer_route(qp, sub_keys):
    t = qp.shape[0]
    tm = ROUTE_TM
    out_spec = pl.BlockSpec((None, PEER_TOPK, tm), lambda i, h: (h, 0, i))
    return pl.pallas_call(
        _route_kernel,
        grid=(t // tm, PEER_HEADS),
        in_specs=[
            pl.BlockSpec((tm, PEER_DKEY), lambda i, h: (i, h)),
            pl.BlockSpec((2, PEER_NKEYS, PEER_DKEY // 2), lambda i, h: (0, 0, 0)),
        ],
        out_specs=[out_spec, out_spec],
        out_shape=[
            jax.ShapeDtypeStruct((PEER_HEADS, PEER_TOPK, t), jnp.int32),
            jax.ShapeDtypeStruct((PEER_HEADS, PEER_TOPK, t), F32),
        ],
        scratch_shapes=[
            pltpu.VMEM((PEER_TOPK, tm), F32), pltpu.VMEM((PEER_TOPK, tm), jnp.int32),
            pltpu.VMEM((PEER_TOPK, tm), F32), pltpu.VMEM((PEER_TOPK, tm), jnp.int32),
            pltpu.VMEM((PEER_TOPK, tm), F32),
        ],
        compiler_params=_cparams("parallel", "parallel"),
        name="peer_route",
    )(qp, sub_keys)


def _gelu_exact(a):
    return 0.5 * a * (1.0 + lax.erf(a * (2.0 ** -0.5)))


def _pack_expert_tables(table_u, table_v):
    half = table_u.shape[1] // 2

    def pack(tbl):
        bits = lax.bitcast_convert_type(tbl.astype(BF16), jnp.uint16).astype(jnp.uint32)
        return (bits[:, half:] << 16) | bits[:, :half]

    return jnp.concatenate([pack(table_u), pack(table_v)], axis=1)


def _unpack_words(w):
    lo = lax.bitcast_convert_type(w << 16, F32)
    hi = lax.bitcast_convert_type(w & jnp.uint32(0xFFFF0000), F32)
    return lo, hi


def _expert_kernel(idx_ref, idx_next_ref, x_ref, gain_ref, sc_ref, sh_ref, g2_ref, gate_ref,
                   tab_hbm, o_ref, buf_a, buf_b, sems):
    s = pl.program_id(0)
    n_steps = pl.num_programs(0)
    ht = x_ref.shape[0] // 2
    d = x_ref.shape[1]
    ngroup = PEER_PICKS // SUBLANES
    nword = d // 2 // LANES
    gain, sc, sh, g2 = gain_ref[...], sc_ref[...], sh_ref[...], g2_ref[...]
    eye = (lax.broadcasted_iota(jnp.int32, (PEER_PICKS, PEER_PICKS), 0)
           == lax.broadcasted_iota(jnp.int32, (PEER_PICKS, PEER_PICKS), 1))

    def start_token(iref, tok, buf, sem_row, ti):
        for g in range(ngroup):
            for k in range(SUBLANES):
                e = iref[0, tok, g * SUBLANES + k]
                pltpu.make_async_copy(
                    tab_hbm.at[pl.ds(e, 1)],
                    buf.at[pl.ds((ti * ngroup + g) * SUBLANES + k, 1)],
                    sems.at[sem_row, ti]).start(priority=k % 2)

    def wait_token(buf, sem_row, ti):
        pltpu.make_async_copy(tab_hbm.at[pl.ds(0, PEER_PICKS)], buf.at[pl.ds(ti * PEER_PICKS, PEER_PICKS)],
                              sems.at[sem_row, ti]).wait()

    def eval_token(buf, ti, tok):
        x = x_ref[pl.ds(tok, 1), :]
        h = _modulated_norm(x, gain, sc, sh)
        hb = jnp.broadcast_to(h, (SUBLANES, d))
        parts = []
        for g in range(ngroup):
            acc = None
            for c in range(nword):
                lo, hi = _unpack_words(buf[pl.ds((ti * ngroup + g) * SUBLANES, SUBLANES),
                                           c * LANES:(c + 1) * LANES])
                term = (lo * hb[:, c * LANES:(c + 1) * LANES]
                        + hi * hb[:, (nword + c) * LANES:(nword + c + 1) * LANES])
                acc = term if acc is None else acc + term
            parts.append(acc)
        act = jnp.sum(jnp.concatenate(parts, axis=0), axis=-1, keepdims=True)
        gate_col = jnp.sum(jnp.where(eye, gate_ref[pl.ds(tok, 1), :], 0.0), axis=-1, keepdims=True)
        a = gate_col * _gelu_exact(act)
        ylo = [None] * nword
        yhi = [None] * nword
        for g in range(ngroup):
            ag = a[g * SUBLANES:(g + 1) * SUBLANES, :]
            for c in range(nword):
                lo, hi = _unpack_words(buf[pl.ds((ti * ngroup + g) * SUBLANES, SUBLANES),
                                           (nword + c) * LANES:(nword + c + 1) * LANES])
                ylo[c] = lo * ag if ylo[c] is None else ylo[c] + lo * ag
                yhi[c] = hi * ag if yhi[c] is None else yhi[c] + hi * ag
        y = jnp.sum(jnp.concatenate(ylo + yhi, axis=1), axis=0, keepdims=True)
        o_ref[pl.ds(tok, 1), :] = x + g2 * y

    @pl.when(s == 0)
    def _():
        def body(ti, carry):
            start_token(idx_ref, ti, buf_a, 0, ti)
            return carry
        lax.fori_loop(0, ht, body, 0)

    for ti in range(ht):
        wait_token(buf_a, 0, ti)
        start_token(idx_ref, ht + ti, buf_b, 1, ti)
        eval_token(buf_a, ti, ti)
    for ti in range(ht):
        wait_token(buf_b, 1, ti)
        start_token(idx_next_ref, ti, buf_a, 0, ti)
        eval_token(buf_b, ti, ht + ti)

    @pl.when(s == n_steps - 1)
    def _():
        def body(ti, carry):
            wait_token(buf_a, 0, ti)
            return carry
        lax.fori_loop(0, ht, body, 0)


def _peer_experts(x2, gain, sc, sh, g2, idx, gates, packed, seq):
    t, d = x2.shape
    tp = PEER_TP
    ht = tp // 2
    n_steps = t // tp
    idx3 = idx.reshape(n_steps, tp, PEER_PICKS)
    per_b = seq // tp
    smem_blk = (1, tp, PEER_PICKS)
    ngroup = PEER_PICKS // SUBLANES
    return pl.pallas_call(
        _expert_kernel,
        grid=(n_steps,),
        in_specs=[
            pl.BlockSpec(smem_blk, lambda s: (s, 0, 0), memory_space=pltpu.SMEM),
            pl.BlockSpec(smem_blk, lambda s: (jnp.minimum(s + 1, n_steps - 1), 0, 0),
                         memory_space=pltpu.SMEM),
            pl.BlockSpec((tp, d), lambda s: (s, 0)),
            pl.BlockSpec((1, d), lambda s: (0, 0)),
            pl.BlockSpec((None, 1, d), lambda s: (s // per_b, 0, 0)),
            pl.BlockSpec((None, 1, d), lambda s: (s // per_b, 0, 0)),
            pl.BlockSpec((None, 1, d), lambda s: (s // per_b, 0, 0)),
            pl.BlockSpec((tp, PEER_PICKS), lambda s: (s, 0)),
            pl.BlockSpec(memory_space=pl.ANY),
        ],
        out_specs=pl.BlockSpec((tp, d), lambda s: (s, 0)),
        out_shape=jax.ShapeDtypeStruct((t, d), F32),
        scratch_shapes=[
            pltpu.VMEM((ht * PEER_PICKS, d), jnp.uint32),
            pltpu.VMEM((ht * PEER_PICKS, d), jnp.uint32),
            pltpu.SemaphoreType.DMA((2, ht)),
        ],
        compiler_params=_cparams("arbitrary"),
        name="peer_experts",
    )(idx3, idx3, x2, gain.reshape(1, d), sc, sh, g2, gates, packed)


def _peer_ffn(x2, gain, sc, sh, g2, w_query, sub_keys, table_u, table_v, seq):
    t, d = x2.shape
    no_gain = jnp.ones((1, w_query.shape[1]), F32)
    qp = _norm_proj(x2, gain, sc, sh, w_query.astype(BF16), no_gain, (), seq)
    idx, gates = _peer_route(qp, sub_keys.astype(BF16))
    idx = idx.transpose(2, 0, 1).reshape(t, PEER_PICKS)
    gates = gates.transpose(2, 0, 1).reshape(t, PEER_PICKS)
    packed = _pack_expert_tables(table_u, table_v)
    return _peer_experts(x2, gain, sc, sh, g2, idx, gates, packed, seq)


def _tile_gain(gain, heads, scale=1.0):
    return jnp.tile(gain * scale, heads)


def _even_layer(x2, mods, norm_gain, w_in, b_forget, fox_qk_gain, diff_qk_gain, diff_lambda,
                diff_subln_gain, w_out, diff_bias, lam_init, bsz, seq):
    sh1, sc1, g1 = mods
    w_main = jnp.concatenate([w_in[:, :3 * FOX_W], w_in[:, 3 * FOX_W + FOX_HEADS:]], axis=1).astype(BF16)
    w_forget = jnp.zeros((D_MODEL, LANES), F32).at[:, :FOX_HEADS].set(
        w_in[:, 3 * FOX_W:3 * FOX_W + FOX_HEADS]).astype(BF16)
    ones = jnp.ones((FOX_W,), F32)
    head_gain = jnp.concatenate([
        _tile_gain(fox_qk_gain[0], FOX_HEADS, ATTN_SCALE), _tile_gain(fox_qk_gain[1], FOX_HEADS), ones,
        _tile_gain(diff_qk_gain[0], 2 * DIFF_HEADS, ATTN_SCALE), _tile_gain(diff_qk_gain[1], 2 * DIFF_HEADS),
        ones]).reshape(1, -1)
    proj, f_side = _norm_proj(x2, norm_gain, sc1, sh1, w_main, head_gain, (0, 1, 3, 4), seq,
                              w_side=w_forget)
    fcol, frow = _forget_cumsum(f_side, b_forget, bsz, seq)
    fox_o = _fox_attention(proj, frow, fcol, bsz, seq)
    bias_tiles = _causal_bias_tiles(diff_bias, seq, ATTN_T)
    diff_o = _diff_attention(proj, bias_tiles, diff_lambda, diff_subln_gain, lam_init, bsz, seq)
    mixed = jnp.concatenate([fox_o, diff_o], axis=-1)
    return _out_proj(mixed, w_out.astype(BF16), x2, g1, seq)


def _odd_layer(x2, mods, norm_gain, w_qkv, qk_gain, w_out, bias_table, bsz, seq):
    sh1, sc1, g1 = mods
    ones = jnp.ones((D_MODEL,), F32)
    head_gain = jnp.concatenate([
        _tile_gain(qk_gain[0], DIL_HEADS, ATTN_SCALE), _tile_gain(qk_gain[1], DIL_HEADS), ones]).reshape(1, -1)
    qkv = _norm_proj(x2, norm_gain, sc1, sh1, w_qkv.astype(BF16), head_gain, (0, 1, 2, 3), seq)
    outs, lses = [], []
    for _, dil in DIL_PATTERNS:
        o, lse = _dilated_pattern(qkv, _dilated_bias_tile(bias_table, dil), dil, bsz, seq)
        outs.append(o)
        lses.append(lse)
    mixed = _dilated_mix(outs, lses)
    return _out_proj(mixed, w_out.astype(BF16), x2, g1, seq)


def kernel(x, c, rel_bias, norm_gain, w_ada, b_ada, even_w_in, even_b_forget, even_fox_qk_gain,
           even_diff_qk_gain, even_diff_lambda, even_diff_subln_gain, even_w_out, odd_w_qkv,
           odd_qk_gain, odd_w_out, peer_w_query, peer_sub_keys, peer_u, peer_v):
    bsz, seq, d = x.shape
    x2 = x.reshape(bsz * seq, d)
    mod = _adaln(c, w_ada, b_ada)
    for i in range(DEPTH):
        sh1, sc1, g1, sh2, sc2, g2 = [m.reshape(bsz, 1, d) for m in jnp.split(mod[i], 6, axis=-1)]
        j = i // 2
        if i % 2 == 0:
            lam_init = 0.8 - 0.6 * math.exp(-0.3 * i)
            x2 = _even_layer(x2, (sh1, sc1, g1), norm_gain[i, 0], even_w_in[j], even_b_forget[j],
                             even_fox_qk_gain[j], even_diff_qk_gain[j], even_diff_lambda[j],
                             even_diff_subln_gain[j], even_w_out[j], rel_bias[:, :DIFF_HEADS],
                             lam_init, bsz, seq)
        else:
            x2 = _odd_layer(x2, (sh1, sc1, g1), norm_gain[i, 0], odd_w_qkv[j], odd_qk_gain[j],
                            odd_w_out[j], rel_bias, bsz, seq)
        x2 = _peer_ffn(x2, norm_gain[i, 1], sc2, sh2, g2, peer_w_query[i], peer_sub_keys[i],
                       peer_u[i], peer_v[i], seq)
    return x2.reshape(bsz, seq, d)
```

```python
import functools
import math

import numpy as np
import jax
import jax.numpy as jnp
from jax import lax
from jax.experimental import pallas as pl
from jax.experimental.pallas import tpu as pltpu

F32 = jnp.float32
BF16 = jnp.bfloat16

D_MODEL = 2048
DEPTH = 2
HEAD_DIM = 128
FOX_HEADS = 8
DIFF_HEADS = 4
DIFF_V_DIM = 2 * HEAD_DIM
DIL_HEADS = D_MODEL // HEAD_DIM
DIL_PATTERNS = ((128, 1), (512, 4), (2048, 16))
DIL_BLOCK = 128
N_BUCKETS = 32
BUCKET_MAX_EXACT = 16
BUCKET_MAX_DIST = 2048
PEER_HEADS = 8
PEER_NKEYS = 128
PEER_EXPERTS = PEER_NKEYS * PEER_NKEYS
PEER_DKEY = 256
PEER_TOPK = 16
PEER_PICKS = PEER_HEADS * PEER_TOPK
NORM_EPS = 1e-6
FOX_W = FOX_HEADS * HEAD_DIM
DIFF_QK_W = DIFF_HEADS * 2 * HEAD_DIM
DIFF_V_W = DIFF_HEADS * DIFF_V_DIM
ATTN_SCALE = HEAD_DIM ** -0.5
MASKED = -1e30

LANES = 128
SUBLANES = 8
VMEM_LIMIT = 48 * 1024 * 1024

PROJ_TM = 512
PROJ_TN = 1024
ATTN_T = 512
ROUTE_TM = 256
PEER_TP = 16


def _cparams(*sem):
    return pltpu.CompilerParams(dimension_semantics=sem, vmem_limit_bytes=VMEM_LIMIT)


def _adaln_kernel(c_ref, w_ref, b_ref, o_ref):
    c = c_ref[...]
    cond = c * jax.nn.sigmoid(c)
    o_ref[...] = jnp.dot(cond.astype(BF16), w_ref[...].astype(BF16),
                         preferred_element_type=F32) + b_ref[...]


def _adaln(c, w_ada, b_ada):
    depth, d, n = w_ada.shape
    bsz = c.shape[0]
    tn = 1024
    return pl.pallas_call(
        _adaln_kernel,
        grid=(depth, n // tn),
        in_specs=[
            pl.BlockSpec((bsz, d), lambda i, j: (0, 0)),
            pl.BlockSpec((None, d, tn), lambda i, j: (i, 0, j)),
            pl.BlockSpec((None, 1, tn), lambda i, j: (i, 0, j)),
        ],
        out_specs=pl.BlockSpec((None, bsz, tn), lambda i, j: (i, 0, j)),
        out_shape=jax.ShapeDtypeStruct((depth, bsz, n), F32),
        compiler_params=_cparams("parallel", "parallel"),
        name="adaln",
    )(c, w_ada, b_ada.reshape(depth, 1, n))


def _modulated_norm(x, gain, sc, sh):
    ms = jnp.mean(x * x, axis=-1, keepdims=True)
    y = x * lax.rsqrt(ms + NORM_EPS) * gain
    return y * (1.0 + sc) + sh


def _norm_proj_kernel(*refs, norm_tiles, has_side):
    if has_side:
        x_ref, gain_ref, sc_ref, sh_ref, w_ref, hg_ref, ws_ref, o_ref, side_ref, h_ref = refs
    else:
        x_ref, gain_ref, sc_ref, sh_ref, w_ref, hg_ref, o_ref, h_ref = refs
    j = pl.program_id(1)

    @pl.when(j == 0)
    def _():
        h = _modulated_norm(x_ref[...], gain_ref[...], sc_ref[...], sh_ref[...])
        h_ref[...] = h.astype(BF16)
        if has_side:
            side_ref[...] = jnp.dot(h_ref[...], ws_ref[...], preferred_element_type=F32)

    acc = jnp.dot(h_ref[...], w_ref[...], preferred_element_type=F32)
    tn = acc.shape[1]

    def write_normed():
        for g in range(tn // HEAD_DIM):
            sl = slice(g * HEAD_DIM, (g + 1) * HEAD_DIM)
            blk = acc[:, sl]
            ms = jnp.mean(blk * blk, axis=-1, keepdims=True)
            o_ref[:, sl] = (blk * lax.rsqrt(ms + NORM_EPS) * hg_ref[:, sl]).astype(o_ref.dtype)

    def write_raw():
        o_ref[...] = acc.astype(o_ref.dtype)

    if not norm_tiles:
        write_raw()
    else:
        is_norm = functools.reduce(jnp.logical_or, [j == t for t in norm_tiles])
        pl.when(is_norm)(write_normed)
        pl.when(jnp.logical_not(is_norm))(write_raw)


def _norm_proj(x2, gain, sc, sh, w, head_gain, norm_tiles, seq, w_side=None):
    t, d = x2.shape
    n = w.shape[1]
    tm, tn = PROJ_TM, PROJ_TN
    per_b = seq // tm
    has_side = w_side is not None
    in_specs = [
        pl.BlockSpec((tm, d), lambda i, j: (i, 0)),
        pl.BlockSpec((1, d), lambda i, j: (0, 0)),
        pl.BlockSpec((None, 1, d), lambda i, j: (i // per_b, 0, 0)),
        pl.BlockSpec((None, 1, d), lambda i, j: (i // per_b, 0, 0)),
        pl.BlockSpec((d, tn), lambda i, j: (0, j)),
        pl.BlockSpec((1, tn), lambda i, j: (0, j)),
    ]
    args = [x2, gain.reshape(1, d), sc, sh, w, head_gain]
    out_specs = pl.BlockSpec((tm, tn), lambda i, j: (i, j))
    out_shape = jax.ShapeDtypeStruct((t, n), BF16)
    if has_side:
        ns = w_side.shape[1]
        in_specs.append(pl.BlockSpec((d, ns), lambda i, j: (0, 0)))
        args.append(w_side)
        out_specs = [out_specs, pl.BlockSpec((tm, ns), lambda i, j: (i, 0))]
        out_shape = [out_shape, jax.ShapeDtypeStruct((t, ns), F32)]
    return pl.pallas_call(
        functools.partial(_norm_proj_kernel, norm_tiles=tuple(norm_tiles), has_side=has_side),
        grid=(t // tm, n // tn),
        in_specs=in_specs,
        out_specs=out_specs,
        out_shape=out_shape,
        scratch_shapes=[pltpu.VMEM((tm, d), BF16)],
        compiler_params=_cparams("parallel", "arbitrary"),
        name="norm_proj",
    )(*args)


def _out_proj_kernel(a_ref, w_ref, x_ref, g_ref, o_ref):
    y = jnp.dot(a_ref[...], w_ref[...], preferred_element_type=F32)
    o_ref[...] = x_ref[...] + g_ref[...] * y


def _out_proj(a, w, x2, g, seq):
    t, k = a.shape
    n = w.shape[1]
    tm, tn = PROJ_TM, PROJ_TN
    per_b = seq // tm
    return pl.pallas_call(
        _out_proj_kernel,
        grid=(t // tm, n // tn),
        in_specs=[
            pl.BlockSpec((tm, k), lambda i, j: (i, 0)),
            pl.BlockSpec((k, tn), lambda i, j: (0, j)),
            pl.BlockSpec((tm, tn), lambda i, j: (i, j)),
            pl.BlockSpec((None, 1, tn), lambda i, j: (i // per_b, 0, j)),
        ],
        out_specs=pl.BlockSpec((tm, tn), lambda i, j: (i, j)),
        out_shape=jax.ShapeDtypeStruct((t, n), F32),
        compiler_params=_cparams("parallel", "parallel"),
        name="out_proj",
    )(a, w, x2, g)


def _forget_kernel(f_ref, b_ref, col_ref, row_ref):
    z = f_ref[...] + b_ref[...]
    lf = jnp.minimum(z, 0.0) - jnp.log1p(jnp.exp(-jnp.abs(z)))
    s = lf.shape[0]
    pos = lax.broadcasted_iota(jnp.int32, lf.shape, 0)
    d = 1
    while d < s:
        lf = lf + jnp.where(pos >= d, pltpu.roll(lf, d, axis=0), 0.0)
        d *= 2
    col_ref[...] = lf
    row_ref[...] = lf.T[:FOX_HEADS, :]


def _forget_cumsum(f_side, b_forget, bsz, seq):
    bpad = jnp.zeros((1, LANES), F32).at[0, :FOX_HEADS].set(b_forget)
    col, row = pl.pallas_call(
        _forget_kernel,
        grid=(bsz,),
        in_specs=[
            pl.BlockSpec((None, seq, LANES), lambda b: (b, 0, 0)),
            pl.BlockSpec((1, LANES), lambda b: (0, 0)),
        ],
        out_specs=[
            pl.BlockSpec((None, seq, LANES), lambda b: (b, 0, 0)),
            pl.BlockSpec((None, FOX_HEADS, seq), lambda b: (b, 0, 0)),
        ],
        out_shape=[
            jax.ShapeDtypeStruct((bsz, seq, LANES), F32),
            jax.ShapeDtypeStruct((bsz, FOX_HEADS, seq), F32),
        ],
        compiler_params=_cparams("parallel"),
        name="forget_cumsum",
    )(f_side.reshape(bsz, seq, LANES), bpad)
    return col, row.reshape(bsz, FOX_HEADS, 1, seq)


def _online_softmax_step(s, v, m_ref, l_ref, acc_ref):
    m_prev = m_ref[...]
    m_new = jnp.maximum(m_prev, jnp.max(s, axis=-1, keepdims=True))
    alpha = jnp.exp(m_prev - m_new)
    p = jnp.exp(s - m_new)
    l_ref[...] = alpha * l_ref[...] + jnp.sum(p, axis=-1, keepdims=True)
    acc_ref[...] = alpha * acc_ref[...] + jnp.dot(p.astype(v.dtype), v, preferred_element_type=F32)
    m_ref[...] = m_new


def _fox_kernel(q_ref, k_ref, v_ref, frow_ref, fcol_ref, o_ref, m_ref, l_ref, acc_ref, fq_ref):
    h = pl.program_id(1)
    qi = pl.program_id(2)
    ki = pl.program_id(3)
    tq, tk = q_ref.shape[0], k_ref.shape[0]

    @pl.when(ki == 0)
    def _():
        m_ref[...] = jnp.full_like(m_ref, MASKED)
        l_ref[...] = jnp.zeros_like(l_ref)
        acc_ref[...] = jnp.zeros_like(acc_ref)
        lane = lax.broadcasted_iota(jnp.int32, fcol_ref.shape, 1)
        fq_ref[...] = jnp.sum(jnp.where(lane == h, fcol_ref[...], 0.0), axis=-1, keepdims=True)

    @pl.when(ki <= qi)
    def _():
        s = lax.dot_general(q_ref[...], k_ref[...], (((1,), (1,)), ((), ())),
                            preferred_element_type=F32)
        s = s + fq_ref[...] - frow_ref[...]
        qpos = qi * tq + lax.broadcasted_iota(jnp.int32, s.shape, 0)
        kpos = ki * tk + lax.broadcasted_iota(jnp.int32, s.shape, 1)
        s = jnp.where(qpos >= kpos, s, MASKED)
        _online_softmax_step(s, v_ref[...], m_ref, l_ref, acc_ref)

    @pl.when(ki == qi)
    def _():
        o_ref[...] = (acc_ref[...] / l_ref[...]).astype(o_ref.dtype)


def _fox_attention(proj, frow, fcol, bsz, seq):
    t = proj.shape[0]
    tt = ATTN_T
    nb = seq // tt
    kcol = FOX_W // HEAD_DIM
    return pl.pallas_call(
        _fox_kernel,
        grid=(bsz, FOX_HEADS, nb, nb),
        in_specs=[
            pl.BlockSpec((tt, HEAD_DIM), lambda b, h, qi, ki: (b * nb + qi, h)),
            pl.BlockSpec((tt, HEAD_DIM), lambda b, h, qi, ki: (b * nb + jnp.minimum(ki, qi), kcol + h)),
            pl.BlockSpec((tt, HEAD_DIM), lambda b, h, qi, ki: (b * nb + jnp.minimum(ki, qi), 2 * kcol + h)),
            pl.BlockSpec((None, None, 1, tt), lambda b, h, qi, ki: (b, h, 0, jnp.minimum(ki, qi))),
            pl.BlockSpec((None, tt, LANES), lambda b, h, qi, ki: (b, qi, 0)),
        ],
        out_specs=pl.BlockSpec((tt, HEAD_DIM), lambda b, h, qi, ki: (b * nb + qi, h)),
        out_shape=jax.ShapeDtypeStruct((t, FOX_W), BF16),
        scratch_shapes=[
            pltpu.VMEM((tt, 1), F32), pltpu.VMEM((tt, 1), F32),
            pltpu.VMEM((tt, HEAD_DIM), F32), pltpu.VMEM((tt, 1), F32),
        ],
        compiler_params=_cparams("parallel", "parallel", "parallel", "arbitrary"),
        name="fox_attention",
    )(proj, proj, proj, frow, fcol)


def _diff_kernel(q_ref, k_ref, v_ref, bias_ref, lam_ref, sg_ref, o_ref,
                 m0, l0, a0, m1, l1, a1, *, lam_init):
    qi = pl.program_id(2)
    ki = pl.program_id(3)

    @pl.when(ki == 0)
    def _():
        for m_ref, l_ref, a_ref in ((m0, l0, a0), (m1, l1, a1)):
            m_ref[...] = jnp.full_like(m_ref, MASKED)
            l_ref[...] = jnp.zeros_like(l_ref)
            a_ref[...] = jnp.zeros_like(a_ref)

    @pl.when(ki <= qi)
    def _():
        bias = bias_ref[...]
        v = v_ref[...]
        for mi, (m_ref, l_ref, a_ref) in enumerate(((m0, l0, a0), (m1, l1, a1))):
            sl = slice(mi * HEAD_DIM, (mi + 1) * HEAD_DIM)
            s = lax.dot_general(q_ref[:, sl], k_ref[:, sl], (((1,), (1,)), ((), ())),
                                preferred_element_type=F32) + bias
            _online_softmax_step(s, v, m_ref, l_ref, a_ref)

    @pl.when(ki == qi)
    def _():
        lf = lam_ref[...]
        lam = (jnp.exp(jnp.sum(lf[0:1] * lf[1:2], axis=-1, keepdims=True))
               - jnp.exp(jnp.sum(lf[2:3] * lf[3:4], axis=-1, keepdims=True)) + lam_init)
        o = a0[...] / l0[...] - lam * (a1[...] / l1[...])
        ms = jnp.mean(o * o, axis=-1, keepdims=True)
        o = o * lax.rsqrt(ms + NORM_EPS) * sg_ref[...] * (1.0 - lam_init)
        o_ref[...] = o.astype(o_ref.dtype)


def _diff_attention(proj, bias_tiles, lam_param, subln_gain, lam_init, bsz, seq):
    t = proj.shape[0]
    tt = ATTN_T
    nb = seq // tt
    qcol = 3 * FOX_W // DIFF_V_DIM
    kcol = qcol + DIFF_QK_W // DIFF_V_DIM
    vcol = kcol + DIFF_QK_W // DIFF_V_DIM
    return pl.pallas_call(
        functools.partial(_diff_kernel, lam_init=lam_init),
        grid=(bsz, DIFF_HEADS, nb, nb),
        in_specs=[
            pl.BlockSpec((tt, DIFF_V_DIM), lambda b, h, qi, ki: (b * nb + qi, qcol + h)),
            pl.BlockSpec((tt, DIFF_V_DIM), lambda b, h, qi, ki: (b * nb + jnp.minimum(ki, qi), kcol + h)),
            pl.BlockSpec((tt, DIFF_V_DIM), lambda b, h, qi, ki: (b * nb + jnp.minimum(ki, qi), vcol + h)),
            pl.BlockSpec((None, None, tt, tt), lambda b, h, qi, ki: (h, jnp.maximum(qi - ki, 0), 0, 0)),
            pl.BlockSpec((4, HEAD_DIM), lambda b, h, qi, ki: (0, 0)),
            pl.BlockSpec((1, DIFF_V_DIM), lambda b, h, qi, ki: (0, 0)),
        ],
        out_specs=pl.BlockSpec((tt, DIFF_V_DIM), lambda b, h, qi, ki: (b * nb + qi, h)),
        out_shape=jax.ShapeDtypeStruct((t, DIFF_V_W), BF16),
        scratch_shapes=[
            pltpu.VMEM((tt, 1), F32), pltpu.VMEM((tt, 1), F32), pltpu.VMEM((tt, DIFF_V_DIM), F32),
            pltpu.VMEM((tt, 1), F32), pltpu.VMEM((tt, 1), F32), pltpu.VMEM((tt, DIFF_V_DIM), F32),
        ],
        compiler_params=_cparams("parallel", "parallel", "parallel", "arbitrary"),
        name="diff_attention",
    )(proj, proj, proj, bias_tiles, lam_param, subln_gain.reshape(1, DIFF_V_DIM))


def _t5_bucket_np(dist):
    n = np.maximum(dist, 0)
    nf = np.maximum(n, 1).astype(np.float32)
    large = BUCKET_MAX_EXACT + (np.log(nf / np.float32(BUCKET_MAX_EXACT))
                                / np.float32(math.log(BUCKET_MAX_DIST / BUCKET_MAX_EXACT))
                                * np.float32(N_BUCKETS - BUCKET_MAX_EXACT)).astype(np.int32)
    large = np.minimum(large, N_BUCKETS - 1)
    return np.where(n < BUCKET_MAX_EXACT, n, large).astype(np.int32)


def _bucket_thresholds():
    buckets = _t5_bucket_np(np.arange(BUCKET_MAX_DIST + 1))
    assert np.all(np.diff(buckets) >= 0)
    return [int(np.argmax(buckets >= b)) for b in range(N_BUCKETS)]


_BUCKET_THRESHOLDS = _bucket_thresholds()


def _bias_from_distance(dist, valid, table_ref, h):
    val = jnp.full(dist.shape, table_ref[0, h], F32)
    for b in range(1, N_BUCKETS):
        val = jnp.where(dist >= _BUCKET_THRESHOLDS[b], table_ref[b, h], val)
    return jnp.where(valid, val, MASKED)


def _causal_bias_kernel(table_ref, o_ref):
    h, delta = pl.program_id(0), pl.program_id(1)
    tt = o_ref.shape[0]
    dist = (delta * tt + lax.broadcasted_iota(jnp.int32, o_ref.shape, 0)
            - lax.broadcasted_iota(jnp.int32, o_ref.shape, 1))
    o_ref[...] = _bias_from_distance(dist, dist >= 0, table_ref, h)


def _causal_bias_tiles(table, seq, tt):
    nb = seq // tt
    heads = table.shape[1]
    return pl.pallas_call(
        _causal_bias_kernel,
        grid=(heads, nb),
        in_specs=[pl.BlockSpec(memory_space=pltpu.SMEM)],
        out_specs=pl.BlockSpec((None, None, tt, tt), lambda h, dlt: (h, dlt, 0, 0)),
        out_shape=jax.ShapeDtypeStruct((heads, nb, tt, tt), F32),
        compiler_params=_cparams("parallel", "parallel"),
        name="causal_bias_tiles",
    )(table)


def _dilated_bias_kernel(table_ref, o_ref):
    h = pl.program_id(0)
    span = DIL_BLOCK
    shape = o_ref.shape[1:]
    step = lax.broadcasted_iota(jnp.int32, shape, 0) + span - lax.broadcasted_iota(jnp.int32, shape, 1)
    valid = jnp.logical_and(step >= 0, step <= span)
    for p, (_, dil) in enumerate(DIL_PATTERNS):
        o_ref[p] = _bias_from_distance(step * dil, valid, table_ref, h)


def _dilated_bias_tiles(table):
    heads = table.shape[1]
    npat = len(DIL_PATTERNS)
    return pl.pallas_call(
        _dilated_bias_kernel,
        grid=(heads,),
        in_specs=[pl.BlockSpec(memory_space=pltpu.SMEM)],
        out_specs=pl.BlockSpec((npat, None, DIL_BLOCK, 2 * DIL_BLOCK), lambda h: (0, h, 0, 0)),
        out_shape=jax.ShapeDtypeStruct((npat, heads, DIL_BLOCK, 2 * DIL_BLOCK), F32),
        compiler_params=_cparams("parallel"),
        name="dilated_bias_tiles",
    )(table)


DIL_GROUP = 4


def _dilated_kernel(q_ref, k_ref, v_ref, bias_ref, o_ref, qf, kf, vf, o_scr, lse_scr):
    seq = q_ref.shape[0]
    blk = DIL_BLOCK
    qf[...] = q_ref[...].astype(F32)
    for src, dst in ((k_ref, kf), (v_ref, vf)):
        dst[0:blk, :] = jnp.zeros((blk, HEAD_DIM), F32)
        dst[blk:, :] = src[...].astype(F32)
    dn_qk = (((2,), (2,)), ((0,), (0,)))
    dn_pv = (((2,), (1,)), ((0,), (0,)))

    def rows(ref, start, dil):
        if dil == 1:
            return ref[pl.ds(start, blk), :]
        return ref[pl.ds(start, blk, stride=dil), :]

    for p, (_, dil) in enumerate(DIL_PATTERNS):
        nblk = seq // dil // blk
        blocks = [(jb, r) for jb in range(nblk) for r in range(dil)]
        bias = bias_ref[p]
        for g0 in range(0, len(blocks), DIL_GROUP):
            group = blocks[g0:g0 + DIL_GROUP]
            starts = [jb * blk * dil + r for jb, r in group]
            has_prev = [jb > 0 for jb, _ in group]
            q = jnp.stack([rows(qf, st, dil) for st in starts]).astype(BF16)
            kc = jnp.stack([rows(kf, blk + st, dil) for st in starts]).astype(BF16)
            vc = jnp.stack([rows(vf, blk + st, dil) for st in starts]).astype(BF16)
            s_cur = lax.dot_general(q, kc, dn_qk, preferred_element_type=F32) + bias[:, blk:]
            m = jnp.max(s_cur, axis=-1, keepdims=True)
            if any(has_prev):
                prev_starts = [max(blk + st - blk * dil, 0) for st in starts]
                kp = jnp.stack([rows(kf, st, dil) for st in prev_starts]).astype(BF16)
                vp = jnp.stack([rows(vf, st, dil) for st in prev_starts]).astype(BF16)
                s_prev = lax.dot_general(q, kp, dn_qk, preferred_element_type=F32) + bias[:, :blk]
                which = lax.broadcasted_iota(jnp.int32, s_prev.shape, 0)
                for gi, ok in enumerate(has_prev):
                    if not ok:
                        s_prev = jnp.where(which == gi, MASKED, s_prev)
                m = jnp.maximum(m, jnp.max(s_prev, axis=-1, keepdims=True))
                p_prev = jnp.exp(s_prev - m)
            p_cur = jnp.exp(s_cur - m)
            l = jnp.sum(p_cur, axis=-1, keepdims=True)
            acc = lax.dot_general(p_cur.astype(BF16), vc, dn_pv, preferred_element_type=F32)
            if any(has_prev):
                l = l + jnp.sum(p_prev, axis=-1, keepdims=True)
                acc = acc + lax.dot_general(p_prev.astype(BF16), vp, dn_pv, preferred_element_type=F32)
            out = acc / l
            lse = jnp.broadcast_to(m + jnp.log(l), out.shape)
            for gi, st in enumerate(starts):
                if dil == 1:
                    o_scr[p, pl.ds(st, blk), :] = out[gi]
                    lse_scr[p, pl.ds(st, blk), :] = lse[gi]
                else:
                    o_scr[p, pl.ds(st, blk, stride=dil), :] = out[gi]
                    lse_scr[p, pl.ds(st, blk, stride=dil), :] = lse[gi]

    lses = [lse_scr[p] for p in range(len(DIL_PATTERNS))]
    top = functools.reduce(jnp.maximum, lses)
    es = [jnp.exp(x - top) for x in lses]
    den = functools.reduce(lambda a, b: a + b, es)
    mixed = functools.reduce(lambda a, b: a + b, [(e / den) * o_scr[p] for p, e in enumerate(es)])
    o_ref[...] = mixed.astype(o_ref.dtype)


def _dilated_attention(qkv, bias_tiles, bsz, seq):
    t = qkv.shape[0]
    h16 = DIL_HEADS
    npat = len(DIL_PATTERNS)
    blk = (seq, HEAD_DIM)
    return pl.pallas_call(
        _dilated_kernel,
        grid=(bsz, DIL_HEADS),
        in_specs=[
            pl.BlockSpec(blk, lambda b, h: (b, h)),
            pl.BlockSpec(blk, lambda b, h: (b, h16 + h)),
            pl.BlockSpec(blk, lambda b, h: (b, 2 * h16 + h)),
            pl.BlockSpec((npat, None, DIL_BLOCK, 2 * DIL_BLOCK), lambda b, h: (0, h, 0, 0)),
        ],
        out_specs=pl.BlockSpec(blk, lambda b, h: (b, h)),
        out_shape=jax.ShapeDtypeStruct((t, D_MODEL), BF16),
        scratch_shapes=[
            pltpu.VMEM((seq, HEAD_DIM), F32),
            pltpu.VMEM((DIL_BLOCK + seq, HEAD_DIM), F32),
            pltpu.VMEM((DIL_BLOCK + seq, HEAD_DIM), F32),
            pltpu.VMEM((npat, seq, HEAD_DIM), F32),
            pltpu.VMEM((npat, seq, HEAD_DIM), F32),
        ],
        compiler_params=_cparams("parallel", "parallel"),
        name="dilated_attention",
    )(qkv, qkv, qkv, bias_tiles)


def _topk_rows(s, val_ref, pick_ref, payload=None):
    neg_rows = -lax.broadcasted_iota(jnp.int32, s.shape, 0).astype(F32)
    for k in range(PEER_TOPK):
        m = jnp.max(s, axis=0, keepdims=True)
        am = jnp.max(jnp.where(s == m, neg_rows, -jnp.inf), axis=0, keepdims=True)
        hit = neg_rows == am
        val_ref[k:k + 1, :] = m
        if payload is None:
            pick_ref[k:k + 1, :] = (-am).astype(jnp.int32)
        else:
            pick_ref[k:k + 1, :] = jnp.sum(jnp.where(hit, payload, 0), axis=0, keepdims=True)
        s = jnp.where(hit, -jnp.inf, s)


def _candidate_rows(a):
    need = PEER_TOPK // (a + 1)
    if need > SUBLANES:
        return PEER_TOPK
    return SUBLANES if need > 1 else 1


def _route_kernel(q_ref, keys_ref, idx_ref, gate_ref, v0, i0, v1, i1, best_ref):
    dn = (((1,), (1,)), ((), ()))
    half = PEER_DKEY // 2
    s0 = lax.dot_general(keys_ref[0], q_ref[:, :half], dn, preferred_element_type=F32)
    s1 = lax.dot_general(keys_ref[1], q_ref[:, half:], dn, preferred_element_type=F32)
    _topk_rows(s0, v0, i0)
    _topk_rows(s1, v1, i1)
    first_single = min(a for a in range(PEER_TOPK) if _candidate_rows(a) == 1)
    cand_s, cand_i = [], []
    for a in range(first_single):
        nb = _candidate_rows(a)
        cand_s.append(v0[a:a + 1, :] + v1[0:nb, :])
        cand_i.append(i0[a:a + 1, :] * PEER_NKEYS + i1[0:nb, :])
    cand_s.append(v0[first_single:, :] + v1[0:1, :])
    cand_i.append(i0[first_single:, :] * PEER_NKEYS + i1[0:1, :])
    _topk_rows(jnp.concatenate(cand_s, axis=0), best_ref, idx_ref, jnp.concatenate(cand_i, axis=0))
    best_s = best_ref[...]
    e = jnp.exp(best_s - best_s[0:1, :])
    gate_ref[...] = e / jnp.sum(e, axis=0, keepdims=True)


def _peer_route(qp, sub_keys):
    t = qp.shape[0]
    tm = ROUTE_TM
    out_spec = pl.BlockSpec((None, PEER_TOPK, tm), lambda i, h: (h, 0, i))
    return pl.pallas_call(
        _route_kernel,
        grid=(t // tm, PEER_HEADS),
        in_specs=[
            pl.BlockSpec((tm, PEER_DKEY), lambda i, h: (i, h)),
            pl.BlockSpec((2, PEER_NKEYS, PEER_DKEY // 2), lambda i, h: (0, 0, 0)),
        ],
        out_specs=[out_spec, out_spec],
        out_shape=[
            jax.ShapeDtypeStruct((PEER_HEADS, PEER_TOPK, t), jnp.int32),
            jax.ShapeDtypeStruct((PEER_HEADS, PEER_TOPK, t), F32),
        ],
        scratch_shapes=[
            pltpu.VMEM((PEER_TOPK, tm), F32), pltpu.VMEM((PEER_TOPK, tm), jnp.int32),
            pltpu.VMEM((PEER_TOPK, tm), F32), pltpu.VMEM((PEER_TOPK, tm), jnp.int32),
            pltpu.VMEM((PEER_TOPK, tm), F32),
        ],
        compiler_params=_cparams("parallel", "parallel"),
        name="peer_route",
    )(qp, sub_keys)


def _gelu_exact(a):
    return 0.5 * a * (1.0 + lax.erf(a * (2.0 ** -0.5)))


def _pack_expert_tables(table_u, table_v):
    e, d = table_u.shape
    half = d // 2

    def pack(tbl):
        bits = lax.bitcast_convert_type(tbl.astype(BF16), jnp.uint16).astype(jnp.uint32)
        return (bits[:, half:] << 16) | bits[:, :half]

    return jnp.concatenate([pack(table_u), pack(table_v)], axis=1).reshape(e, 1, d)


def _unpack_words(w):
    lo = lax.bitcast_convert_type(w << 16, F32)
    hi = lax.bitcast_convert_type(w & jnp.uint32(0xFFFF0000), F32)
    return lo, hi


PEER_NBUF = 4
PEER_LOOKAHEAD = 2


def _expert_kernel(idx_ref, idx_next_ref, x_ref, gain_ref, sc_ref, sh_ref, g2_ref, gate_ref,
                   tab_hbm, o_ref, *scratch):
    bufs, sems = scratch[:PEER_NBUF], scratch[PEER_NBUF]
    s = pl.program_id(0)
    n_steps = pl.num_programs(0)
    gt = x_ref.shape[0] // PEER_NBUF
    d = x_ref.shape[1]
    ngroup = PEER_PICKS // SUBLANES
    nword = d // 2 // LANES
    gain, sc, sh, g2 = gain_ref[...], sc_ref[...], sh_ref[...], g2_ref[...]
    eye = (lax.broadcasted_iota(jnp.int32, (PEER_PICKS, PEER_PICKS), 0)
           == lax.broadcasted_iota(jnp.int32, (PEER_PICKS, PEER_PICKS), 1))

    def start_token(iref, tok, j, ti):
        for g in range(ngroup):
            for k in range(SUBLANES):
                e = iref[0, tok, g * SUBLANES + k]
                pltpu.make_async_copy(
                    tab_hbm.at[e],
                    bufs[j].at[pl.ds((ti * ngroup + g) * SUBLANES + k, 1)],
                    sems.at[j]).start(priority=k % 2)

    def wait_group(j):
        pltpu.make_async_copy(bufs[j], bufs[j], sems.at[j]).wait()

    def gated_activation(buf, ti, tok):
        h = _modulated_norm(x_ref[pl.ds(tok, 1), :], gain, sc, sh)
        hb = jnp.broadcast_to(h, (SUBLANES, d))
        parts = []
        for g in range(ngroup):
            acc = None
            for c in range(nword):
                lo, hi = _unpack_words(buf[pl.ds((ti * ngroup + g) * SUBLANES, SUBLANES),
                                           c * LANES:(c + 1) * LANES])
                term = (lo * hb[:, c * LANES:(c + 1) * LANES]
                        + hi * hb[:, (nword + c) * LANES:(nword + c + 1) * LANES])
                acc = term if acc is None else acc + term
            parts.append(acc)
        act = jnp.sum(jnp.concatenate(parts, axis=0), axis=-1, keepdims=True)
        gate_col = jnp.sum(jnp.where(eye, gate_ref[pl.ds(tok, 1), :], 0.0), axis=-1, keepdims=True)
        return gate_col * _gelu_exact(act)

    def weighted_values(buf, ti, tok, a):
        ylo = [None] * nword
        yhi = [None] * nword
        for g in range(ngroup):
            ag = a[g * SUBLANES:(g + 1) * SUBLANES, :]
            for c in range(nword):
                lo, hi = _unpack_words(buf[pl.ds((ti * ngroup + g) * SUBLANES, SUBLANES),
                                           (nword + c) * LANES:(nword + c + 1) * LANES])
                ylo[c] = lo * ag if ylo[c] is None else ylo[c] + lo * ag
                yhi[c] = hi * ag if yhi[c] is None else yhi[c] + hi * ag
        y = jnp.sum(jnp.concatenate(ylo + yhi, axis=1), axis=0, keepdims=True)
        o_ref[pl.ds(tok, 1), :] = x_ref[pl.ds(tok, 1), :] + g2 * y

    @pl.when(s == 0)
    def _():
        for j in range(PEER_LOOKAHEAD):
            def body(ti, carry, j=j):
                start_token(idx_ref, j * gt + ti, j, ti)
                return carry
            lax.fori_loop(0, gt, body, 0)

    pending = None
    for j in range(PEER_NBUF):
        wait_group(j)
        ahead = j + PEER_LOOKAHEAD
        iref = idx_ref if ahead < PEER_NBUF else idx_next_ref
        for ti in range(gt):
            tok = j * gt + ti
            start_token(iref, (ahead % PEER_NBUF) * gt + ti, ahead % PEER_NBUF, ti)
            a = gated_activation(bufs[j], ti, tok)
            if pending is not None:
                weighted_values(*pending)
            pending = (bufs[j], ti, tok, a)
    weighted_values(*pending)

    @pl.when(s == n_steps - 1)
    def _():
        for j in range(PEER_LOOKAHEAD):
            wait_group(j)


def _peer_experts(x2, gain, sc, sh, g2, idx, gates, packed, seq):
    t, d = x2.shape
    tp = PEER_TP
    n_steps = t // tp
    idx3 = idx.reshape(n_steps, tp, PEER_PICKS)
    per_b = seq // tp
    smem_blk = (1, tp, PEER_PICKS)
    return pl.pallas_call(
        _expert_kernel,
        grid=(n_steps,),
        in_specs=[
            pl.BlockSpec(smem_blk, lambda s: (s, 0, 0), memory_space=pltpu.SMEM),
            pl.BlockSpec(smem_blk, lambda s: (jnp.minimum(s + 1, n_steps - 1), 0, 0),
                         memory_space=pltpu.SMEM),
            pl.BlockSpec((tp, d), lambda s: (s, 0)),
            pl.BlockSpec((1, d), lambda s: (0, 0)),
            pl.BlockSpec((None, 1, d), lambda s: (s // per_b, 0, 0)),
            pl.BlockSpec((None, 1, d), lambda s: (s // per_b, 0, 0)),
            pl.BlockSpec((None, 1, d), lambda s: (s // per_b, 0, 0)),
            pl.BlockSpec((tp, PEER_PICKS), lambda s: (s, 0)),
            pl.BlockSpec(memory_space=pl.ANY),
        ],
        out_specs=pl.BlockSpec((tp, d), lambda s: (s, 0)),
        out_shape=jax.ShapeDtypeStruct((t, d), F32),
        scratch_shapes=(
            [pltpu.VMEM((tp // PEER_NBUF * PEER_PICKS, d), jnp.uint32)] * PEER_NBUF
            + [pltpu.SemaphoreType.DMA((PEER_NBUF,))]),
        compiler_params=_cparams("arbitrary"),
        name="peer_experts",
    )(idx3, idx3, x2, gain.reshape(1, d), sc, sh, g2, gates, packed)


def _peer_ffn(x2, gain, sc, sh, g2, w_query, sub_keys, table_u, table_v, seq):
    t, d = x2.shape
    no_gain = jnp.ones((1, w_query.shape[1]), F32)
    qp = _norm_proj(x2, gain, sc, sh, w_query.astype(BF16), no_gain, (), seq)
    idx, gates = _peer_route(qp, sub_keys.astype(BF16))
    idx = idx.transpose(2, 0, 1).reshape(t, PEER_PICKS)
    gates = gates.transpose(2, 0, 1).reshape(t, PEER_PICKS)
    packed = _pack_expert_tables(table_u, table_v)
    return _peer_experts(x2, gain, sc, sh, g2, idx, gates, packed, seq)


def _tile_gain(gain, heads, scale=1.0):
    return jnp.tile(gain * scale, heads)


def _even_layer(x2, mods, norm_gain, w_in, b_forget, fox_qk_gain, diff_qk_gain, diff_lambda,
                diff_subln_gain, w_out, diff_bias, lam_init, bsz, seq):
    sh1, sc1, g1 = mods
    w_main = jnp.concatenate([w_in[:, :3 * FOX_W], w_in[:, 3 * FOX_W + FOX_HEADS:]], axis=1).astype(BF16)
    w_forget = jnp.zeros((D_MODEL, LANES), F32).at[:, :FOX_HEADS].set(
        w_in[:, 3 * FOX_W:3 * FOX_W + FOX_HEADS]).astype(BF16)
    ones = jnp.ones((FOX_W,), F32)
    head_gain = jnp.concatenate([
        _tile_gain(fox_qk_gain[0], FOX_HEADS, ATTN_SCALE), _tile_gain(fox_qk_gain[1], FOX_HEADS), ones,
        _tile_gain(diff_qk_gain[0], 2 * DIFF_HEADS, ATTN_SCALE), _tile_gain(diff_qk_gain[1], 2 * DIFF_HEADS),
        ones]).reshape(1, -1)
    proj, f_side = _norm_proj(x2, norm_gain, sc1, sh1, w_main, head_gain, (0, 1, 3, 4), seq,
                              w_side=w_forget)
    fcol, frow = _forget_cumsum(f_side, b_forget, bsz, seq)
    fox_o = _fox_attention(proj, frow, fcol, bsz, seq)
    bias_tiles = _causal_bias_tiles(diff_bias, seq, ATTN_T)
    diff_o = _diff_attention(proj, bias_tiles, diff_lambda, diff_subln_gain, lam_init, bsz, seq)
    mixed = jnp.concatenate([fox_o, diff_o], axis=-1)
    return _out_proj(mixed, w_out.astype(BF16), x2, g1, seq)


def _odd_layer(x2, mods, norm_gain, w_qkv, qk_gain, w_out, bias_table, bsz, seq):
    sh1, sc1, g1 = mods
    ones = jnp.ones((D_MODEL,), F32)
    head_gain = jnp.concatenate([
        _tile_gain(qk_gain[0], DIL_HEADS, ATTN_SCALE), _tile_gain(qk_gain[1], DIL_HEADS), ones]).reshape(1, -1)
    qkv = _norm_proj(x2, norm_gain, sc1, sh1, w_qkv.astype(BF16), head_gain, (0, 1, 2, 3), seq)
    mixed = _dilated_attention(qkv, _dilated_bias_tiles(bias_table), bsz, seq)
    return _out_proj(mixed, w_out.astype(BF16), x2, g1, seq)


def kernel(x, c, rel_bias, norm_gain, w_ada, b_ada, even_w_in, even_b_forget, even_fox_qk_gain,
           even_diff_qk_gain, even_diff_lambda, even_diff_subln_gain, even_w_out, odd_w_qkv,
           odd_qk_gain, odd_w_out, peer_w_query, peer_sub_keys, peer_u, peer_v):
    bsz, seq, d = x.shape
    x2 = x.reshape(bsz * seq, d)
    mod = _adaln(c, w_ada, b_ada)
    for i in range(DEPTH):
        sh1, sc1, g1, sh2, sc2, g2 = [m.reshape(bsz, 1, d) for m in jnp.split(mod[i], 6, axis=-1)]
        j = i // 2
        if i % 2 == 0:
            lam_init = 0.8 - 0.6 * math.exp(-0.3 * i)
            x2 = _even_layer(x2, (sh1, sc1, g1), norm_gain[i, 0], even_w_in[j], even_b_forget[j],
                             even_fox_qk_gain[j], even_diff_qk_gain[j], even_diff_lambda[j],
                             even_diff_subln_gain[j], even_w_out[j], rel_bias[:, :DIFF_HEADS],
                             lam_init, bsz, seq)
        else:
            x2 = _odd_layer(x2, (sh1, sc1, g1), norm_gain[i, 0], odd_w_qkv[j], odd_qk_gain[j],
                            odd_w_out[j], rel_bias, bsz, seq)
        x2 = _peer_ffn(x2, norm_gain[i, 1], sc2, sh2, g2, peer_w_query[i], peer_sub_keys[i],
                       peer_u[i], peer_v[i], seq)
    return x2.reshape(bsz, seq, d)
```

```python
import functools
import math

import numpy as np
import jax
import jax.numpy as jnp
from jax import lax
from jax.experimental import pallas as pl
from jax.experimental.pallas import tpu as pltpu

F32 = jnp.float32
BF16 = jnp.bfloat16

D_MODEL = 2048
DEPTH = 2
HEAD_DIM = 128
FOX_HEADS = 8
DIFF_HEADS = 4
DIFF_V_DIM = 2 * HEAD_DIM
DIL_HEADS = D_MODEL // HEAD_DIM
DIL_PATTERNS = ((128, 1), (512, 4), (2048, 16))
DIL_BLOCK = 128
N_BUCKETS = 32
BUCKET_MAX_EXACT = 16
BUCKET_MAX_DIST = 2048
PEER_HEADS = 8
PEER_NKEYS = 128
PEER_EXPERTS = PEER_NKEYS * PEER_NKEYS
PEER_DKEY = 256
PEER_TOPK = 16
PEER_PICKS = PEER_HEADS * PEER_TOPK
NORM_EPS = 1e-6
FOX_W = FOX_HEADS * HEAD_DIM
DIFF_QK_W = DIFF_HEADS * 2 * HEAD_DIM
DIFF_V_W = DIFF_HEADS * DIFF_V_DIM
ATTN_SCALE = HEAD_DIM ** -0.5
MASKED = -1e30

LANES = 128
SUBLANES = 8
VMEM_LIMIT = 48 * 1024 * 1024

PROJ_TM = 512
PROJ_TN = 1024
ATTN_T = 512
ROUTE_TM = 256
PEER_TP = 16


def _cparams(*sem):
    return pltpu.CompilerParams(dimension_semantics=sem, vmem_limit_bytes=VMEM_LIMIT)


def _adaln_kernel(c_ref, w_ref, b_ref, o_ref):
    c = c_ref[...]
    cond = c * jax.nn.sigmoid(c)
    o_ref[...] = jnp.dot(cond.astype(BF16), w_ref[...].astype(BF16),
                         preferred_element_type=F32) + b_ref[...]


def _adaln(c, w_ada, b_ada):
    depth, d, n = w_ada.shape
    bsz = c.shape[0]
    tn = 1024
    return pl.pallas_call(
        _adaln_kernel,
        grid=(depth, n // tn),
        in_specs=[
            pl.BlockSpec((bsz, d), lambda i, j: (0, 0)),
            pl.BlockSpec((None, d, tn), lambda i, j: (i, 0, j)),
            pl.BlockSpec((None, 1, tn), lambda i, j: (i, 0, j)),
        ],
        out_specs=pl.BlockSpec((None, bsz, tn), lambda i, j: (i, 0, j)),
        out_shape=jax.ShapeDtypeStruct((depth, bsz, n), F32),
        compiler_params=_cparams("parallel", "parallel"),
        name="adaln",
    )(c, w_ada, b_ada.reshape(depth, 1, n))


def _modulated_norm(x, gain, sc, sh):
    ms = jnp.mean(x * x, axis=-1, keepdims=True)
    y = x * lax.rsqrt(ms + NORM_EPS) * gain
    return y * (1.0 + sc) + sh


def _norm_proj_kernel(*refs, norm_tiles, has_side):
    if has_side:
        x_ref, gain_ref, sc_ref, sh_ref, w_ref, hg_ref, ws_ref, o_ref, side_ref, h_ref = refs
    else:
        x_ref, gain_ref, sc_ref, sh_ref, w_ref, hg_ref, o_ref, h_ref = refs
    j = pl.program_id(1)

    @pl.when(j == 0)
    def _():
        h = _modulated_norm(x_ref[...], gain_ref[...], sc_ref[...], sh_ref[...])
        h_ref[...] = h.astype(BF16)
        if has_side:
            side_ref[...] = jnp.dot(h_ref[...], ws_ref[...], preferred_element_type=F32)

    acc = jnp.dot(h_ref[...], w_ref[...], preferred_element_type=F32)
    tn = acc.shape[1]

    def write_normed():
        for g in range(tn // HEAD_DIM):
            sl = slice(g * HEAD_DIM, (g + 1) * HEAD_DIM)
            blk = acc[:, sl]
            ms = jnp.mean(blk * blk, axis=-1, keepdims=True)
            o_ref[g] = (blk * lax.rsqrt(ms + NORM_EPS) * hg_ref[:, sl]).astype(o_ref.dtype)

    def write_raw():
        for g in range(tn // HEAD_DIM):
            o_ref[g] = acc[:, g * HEAD_DIM:(g + 1) * HEAD_DIM].astype(o_ref.dtype)

    if not norm_tiles:
        write_raw()
    else:
        is_norm = functools.reduce(jnp.logical_or, [j == t for t in norm_tiles])
        pl.when(is_norm)(write_normed)
        pl.when(jnp.logical_not(is_norm))(write_raw)


def _norm_proj(x2, gain, sc, sh, w, head_gain, norm_tiles, seq, w_side=None):
    t, d = x2.shape
    n = w.shape[1]
    tm, tn = PROJ_TM, PROJ_TN
    per_b = seq // tm
    has_side = w_side is not None
    gpt = tn // HEAD_DIM
    in_specs = [
        pl.BlockSpec((tm, d), lambda i, j: (i, 0)),
        pl.BlockSpec((1, d), lambda i, j: (0, 0)),
        pl.BlockSpec((None, 1, d), lambda i, j: (i // per_b, 0, 0)),
        pl.BlockSpec((None, 1, d), lambda i, j: (i // per_b, 0, 0)),
        pl.BlockSpec((d, tn), lambda i, j: (0, j)),
        pl.BlockSpec((1, tn), lambda i, j: (0, j)),
    ]
    args = [x2, gain.reshape(1, d), sc, sh, w, head_gain]
    out_specs = pl.BlockSpec((gpt, tm, HEAD_DIM), lambda i, j: (j, i, 0))
    out_shape = jax.ShapeDtypeStruct((n // HEAD_DIM, t, HEAD_DIM), BF16)
    if has_side:
        ns = w_side.shape[1]
        in_specs.append(pl.BlockSpec((d, ns), lambda i, j: (0, 0)))
        args.append(w_side)
        out_specs = [out_specs, pl.BlockSpec((tm, ns), lambda i, j: (i, 0))]
        out_shape = [out_shape, jax.ShapeDtypeStruct((t, ns), F32)]
    return pl.pallas_call(
        functools.partial(_norm_proj_kernel, norm_tiles=tuple(norm_tiles), has_side=has_side),
        grid=(t // tm, n // tn),
        in_specs=in_specs,
        out_specs=out_specs,
        out_shape=out_shape,
        scratch_shapes=[pltpu.VMEM((tm, d), BF16)],
        compiler_params=_cparams("parallel", "arbitrary"),
        name="norm_proj",
    )(*args)


def _out_proj_kernel(*refs):
    *a_refs, w_ref, x_ref, g_ref, o_ref = refs
    a = jnp.concatenate([a_ref[g] for a_ref in a_refs for g in range(a_ref.shape[0])], axis=1)
    y = jnp.dot(a, w_ref[...], preferred_element_type=F32)
    o_ref[...] = x_ref[...] + g_ref[...] * y


def _out_proj(heads_major, w, x2, g, seq):
    t = x2.shape[0]
    k, n = w.shape
    assert sum(a.shape[0] for a in heads_major) * HEAD_DIM == k
    tm, tn = PROJ_TM, PROJ_TN
    per_b = seq // tm
    return pl.pallas_call(
        _out_proj_kernel,
        grid=(t // tm, n // tn),
        in_specs=[pl.BlockSpec((a.shape[0], tm, HEAD_DIM), lambda i, j: (0, i, 0)) for a in heads_major] + [
            pl.BlockSpec((k, tn), lambda i, j: (0, j)),
            pl.BlockSpec((tm, tn), lambda i, j: (i, j)),
            pl.BlockSpec((None, 1, tn), lambda i, j: (i // per_b, 0, j)),
        ],
        out_specs=pl.BlockSpec((tm, tn), lambda i, j: (i, j)),
        out_shape=jax.ShapeDtypeStruct((t, n), F32),
        compiler_params=_cparams("parallel", "parallel"),
        name="out_proj",
    )(*heads_major, w, x2, g)


def _forget_kernel(f_ref, b_ref, col_ref, row_ref):
    z = f_ref[...] + b_ref[...]
    lf = jnp.minimum(z, 0.0) - jnp.log1p(jnp.exp(-jnp.abs(z)))
    s = lf.shape[0]
    pos = lax.broadcasted_iota(jnp.int32, lf.shape, 0)
    d = 1
    while d < s:
        lf = lf + jnp.where(pos >= d, pltpu.roll(lf, d, axis=0), 0.0)
        d *= 2
    col_ref[...] = lf
    row_ref[...] = lf.T[:FOX_HEADS, :]


def _forget_cumsum(f_side, b_forget, bsz, seq):
    bpad = jnp.zeros((1, LANES), F32).at[0, :FOX_HEADS].set(b_forget)
    col, row = pl.pallas_call(
        _forget_kernel,
        grid=(bsz,),
        in_specs=[
            pl.BlockSpec((None, seq, LANES), lambda b: (b, 0, 0)),
            pl.BlockSpec((1, LANES), lambda b: (0, 0)),
        ],
        out_specs=[
            pl.BlockSpec((None, seq, LANES), lambda b: (b, 0, 0)),
            pl.BlockSpec((None, FOX_HEADS, seq), lambda b: (b, 0, 0)),
        ],
        out_shape=[
            jax.ShapeDtypeStruct((bsz, seq, LANES), F32),
            jax.ShapeDtypeStruct((bsz, FOX_HEADS, seq), F32),
        ],
        compiler_params=_cparams("parallel"),
        name="forget_cumsum",
    )(f_side.reshape(bsz, seq, LANES), bpad)
    return col, row.reshape(bsz, FOX_HEADS, 1, seq)


def _online_softmax_step(s, v, m_ref, l_ref, acc_ref):
    m_prev = m_ref[...]
    m_new = jnp.maximum(m_prev, jnp.max(s, axis=-1, keepdims=True))
    alpha = jnp.exp(m_prev - m_new)
    p = jnp.exp(s - m_new)
    l_ref[...] = alpha * l_ref[...] + jnp.sum(p, axis=-1, keepdims=True)
    acc_ref[...] = alpha * acc_ref[...] + jnp.dot(p.astype(v.dtype), v, preferred_element_type=F32)
    m_ref[...] = m_new


def _fox_kernel(q_ref, k_ref, v_ref, frow_ref, fcol_ref, o_ref, m_ref, l_ref, acc_ref, fq_ref):
    h = pl.program_id(1)
    qi = pl.program_id(2)
    ki = pl.program_id(3)
    tq, tk = q_ref.shape[0], k_ref.shape[0]

    @pl.when(ki == 0)
    def _():
        m_ref[...] = jnp.full_like(m_ref, MASKED)
        l_ref[...] = jnp.zeros_like(l_ref)
        acc_ref[...] = jnp.zeros_like(acc_ref)
        lane = lax.broadcasted_iota(jnp.int32, fcol_ref.shape, 1)
        fq_ref[...] = jnp.sum(jnp.where(lane == h, fcol_ref[...], 0.0), axis=-1, keepdims=True)

    @pl.when(ki <= qi)
    def _():
        s = lax.dot_general(q_ref[...], k_ref[...], (((1,), (1,)), ((), ())),
                            preferred_element_type=F32)
        s = s + fq_ref[...] - frow_ref[...]
        qpos = qi * tq + lax.broadcasted_iota(jnp.int32, s.shape, 0)
        kpos = ki * tk + lax.broadcasted_iota(jnp.int32, s.shape, 1)
        s = jnp.where(qpos >= kpos, s, MASKED)
        _online_softmax_step(s, v_ref[...], m_ref, l_ref, acc_ref)

    @pl.when(ki == qi)
    def _():
        o_ref[...] = (acc_ref[...] / l_ref[...]).astype(o_ref.dtype)


def _fox_attention(proj, frow, fcol, bsz, seq):
    t = proj.shape[1]
    tt = ATTN_T
    nb = seq // tt
    kcol = FOX_W // HEAD_DIM
    blk = (None, tt, HEAD_DIM)
    return pl.pallas_call(
        _fox_kernel,
        grid=(bsz, FOX_HEADS, nb, nb),
        in_specs=[
            pl.BlockSpec(blk, lambda b, h, qi, ki: (h, b * nb + qi, 0)),
            pl.BlockSpec(blk, lambda b, h, qi, ki: (kcol + h, b * nb + jnp.minimum(ki, qi), 0)),
            pl.BlockSpec(blk, lambda b, h, qi, ki: (2 * kcol + h, b * nb + jnp.minimum(ki, qi), 0)),
            pl.BlockSpec((None, None, 1, tt), lambda b, h, qi, ki: (b, h, 0, jnp.minimum(ki, qi))),
            pl.BlockSpec((None, tt, LANES), lambda b, h, qi, ki: (b, qi, 0)),
        ],
        out_specs=pl.BlockSpec(blk, lambda b, h, qi, ki: (h, b * nb + qi, 0)),
        out_shape=jax.ShapeDtypeStruct((FOX_HEADS, t, HEAD_DIM), BF16),
        scratch_shapes=[
            pltpu.VMEM((tt, 1), F32), pltpu.VMEM((tt, 1), F32),
            pltpu.VMEM((tt, HEAD_DIM), F32), pltpu.VMEM((tt, 1), F32),
        ],
        compiler_params=_cparams("parallel", "parallel", "parallel", "arbitrary"),
        name="fox_attention",
    )(proj, proj, proj, frow, fcol)


def _diff_kernel(q_ref, k_ref, v_ref, bias_ref, lam_ref, sg_ref, o_ref,
                 m0, l0, a0, m1, l1, a1, *, lam_init):
    qi = pl.program_id(2)
    ki = pl.program_id(3)

    @pl.when(ki == 0)
    def _():
        for m_ref, l_ref, a_ref in ((m0, l0, a0), (m1, l1, a1)):
            m_ref[...] = jnp.full_like(m_ref, MASKED)
            l_ref[...] = jnp.zeros_like(l_ref)
            a_ref[...] = jnp.zeros_like(a_ref)

    @pl.when(ki <= qi)
    def _():
        bias = bias_ref[...]
        v = jnp.concatenate([v_ref[0], v_ref[1]], axis=1)
        for mi, (m_ref, l_ref, a_ref) in enumerate(((m0, l0, a0), (m1, l1, a1))):
            s = lax.dot_general(q_ref[mi], k_ref[mi], (((1,), (1,)), ((), ())),
                                preferred_element_type=F32) + bias
            _online_softmax_step(s, v, m_ref, l_ref, a_ref)

    @pl.when(ki == qi)
    def _():
        lf = lam_ref[...]
        lam = (jnp.exp(jnp.sum(lf[0:1] * lf[1:2], axis=-1, keepdims=True))
               - jnp.exp(jnp.sum(lf[2:3] * lf[3:4], axis=-1, keepdims=True)) + lam_init)
        o = a0[...] / l0[...] - lam * (a1[...] / l1[...])
        ms = jnp.mean(o * o, axis=-1, keepdims=True)
        o = (o * lax.rsqrt(ms + NORM_EPS) * sg_ref[...] * (1.0 - lam_init)).astype(o_ref.dtype)
        o_ref[0] = o[:, :HEAD_DIM]
        o_ref[1] = o[:, HEAD_DIM:]


def _diff_attention(proj, bias_tiles, lam_param, subln_gain, lam_init, bsz, seq):
    t = proj.shape[1]
    tt = ATTN_T
    nb = seq // tt
    qcol = 3 * FOX_W // DIFF_V_DIM
    kcol = qcol + DIFF_QK_W // DIFF_V_DIM
    vcol = kcol + DIFF_QK_W // DIFF_V_DIM
    blk = (2, tt, HEAD_DIM)
    return pl.pallas_call(
        functools.partial(_diff_kernel, lam_init=lam_init),
        grid=(bsz, DIFF_HEADS, nb, nb),
        in_specs=[
            pl.BlockSpec(blk, lambda b, h, qi, ki: (qcol + h, b * nb + qi, 0)),
            pl.BlockSpec(blk, lambda b, h, qi, ki: (kcol + h, b * nb + jnp.minimum(ki, qi), 0)),
            pl.BlockSpec(blk, lambda b, h, qi, ki: (vcol + h, b * nb + jnp.minimum(ki, qi), 0)),
            pl.BlockSpec((None, None, tt, tt), lambda b, h, qi, ki: (h, jnp.maximum(qi - ki, 0), 0, 0)),
            pl.BlockSpec((4, HEAD_DIM), lambda b, h, qi, ki: (0, 0)),
            pl.BlockSpec((1, DIFF_V_DIM), lambda b, h, qi, ki: (0, 0)),
        ],
        out_specs=pl.BlockSpec(blk, lambda b, h, qi, ki: (h, b * nb + qi, 0)),
        out_shape=jax.ShapeDtypeStruct((2 * DIFF_HEADS, t, HEAD_DIM), BF16),
        scratch_shapes=[
            pltpu.VMEM((tt, 1), F32), pltpu.VMEM((tt, 1), F32), pltpu.VMEM((tt, DIFF_V_DIM), F32),
            pltpu.VMEM((tt, 1), F32), pltpu.VMEM((tt, 1), F32), pltpu.VMEM((tt, DIFF_V_DIM), F32),
        ],
        compiler_params=_cparams("parallel", "parallel", "parallel", "arbitrary"),
        name="diff_attention",
    )(proj, proj, proj, bias_tiles, lam_param, subln_gain.reshape(1, DIFF_V_DIM))


def _t5_bucket_np(dist):
    n = np.maximum(dist, 0)
    nf = np.maximum(n, 1).astype(np.float32)
    large = BUCKET_MAX_EXACT + (np.log(nf / np.float32(BUCKET_MAX_EXACT))
                                / np.float32(math.log(BUCKET_MAX_DIST / BUCKET_MAX_EXACT))
                                * np.float32(N_BUCKETS - BUCKET_MAX_EXACT)).astype(np.int32)
    large = np.minimum(large, N_BUCKETS - 1)
    return np.where(n < BUCKET_MAX_EXACT, n, large).astype(np.int32)


def _bucket_thresholds():
    buckets = _t5_bucket_np(np.arange(BUCKET_MAX_DIST + 1))
    assert np.all(np.diff(buckets) >= 0)
    return [int(np.argmax(buckets >= b)) for b in range(N_BUCKETS)]


_BUCKET_THRESHOLDS = _bucket_thresholds()


def _bias_from_distance(dist, valid, table_ref, h):
    val = jnp.full(dist.shape, table_ref[0, h], F32)
    for b in range(1, N_BUCKETS):
        val = jnp.where(dist >= _BUCKET_THRESHOLDS[b], table_ref[b, h], val)
    return jnp.where(valid, val, MASKED)


def _causal_bias_kernel(table_ref, o_ref):
    h, delta = pl.program_id(0), pl.program_id(1)
    tt = o_ref.shape[0]
    dist = (delta * tt + lax.broadcasted_iota(jnp.int32, o_ref.shape, 0)
            - lax.broadcasted_iota(jnp.int32, o_ref.shape, 1))
    o_ref[...] = _bias_from_distance(dist, dist >= 0, table_ref, h)


def _causal_bias_tiles(table, seq, tt):
    nb = seq // tt
    heads = table.shape[1]
    return pl.pallas_call(
        _causal_bias_kernel,
        grid=(heads, nb),
        in_specs=[pl.BlockSpec(memory_space=pltpu.SMEM)],
        out_specs=pl.BlockSpec((None, None, tt, tt), lambda h, dlt: (h, dlt, 0, 0)),
        out_shape=jax.ShapeDtypeStruct((heads, nb, tt, tt), F32),
        compiler_params=_cparams("parallel", "parallel"),
        name="causal_bias_tiles",
    )(table)


def _dilated_bias_kernel(table_ref, o_ref):
    h = pl.program_id(0)
    span = DIL_BLOCK
    shape = o_ref.shape[1:]
    step = lax.broadcasted_iota(jnp.int32, shape, 0) + span - lax.broadcasted_iota(jnp.int32, shape, 1)
    valid = jnp.logical_and(step >= 0, step <= span)
    for p, (_, dil) in enumerate(DIL_PATTERNS):
        o_ref[p] = _bias_from_distance(step * dil, valid, table_ref, h)


def _dilated_bias_tiles(table):
    heads = table.shape[1]
    npat = len(DIL_PATTERNS)
    return pl.pallas_call(
        _dilated_bias_kernel,
        grid=(heads,),
        in_specs=[pl.BlockSpec(memory_space=pltpu.SMEM)],
        out_specs=pl.BlockSpec((npat, None, DIL_BLOCK, 2 * DIL_BLOCK), lambda h: (0, h, 0, 0)),
        out_shape=jax.ShapeDtypeStruct((npat, heads, DIL_BLOCK, 2 * DIL_BLOCK), F32),
        compiler_params=_cparams("parallel"),
        name="dilated_bias_tiles",
    )(table)


DIL_GROUP = 4


def _dilated_kernel(q_ref, k_ref, v_ref, bias_ref, o_ref, qf, kf, vf, o_scr, lse_scr):
    seq = q_ref.shape[0]
    blk = DIL_BLOCK
    qf[...] = q_ref[...].astype(F32)
    for src, dst in ((k_ref, kf), (v_ref, vf)):
        dst[0:blk, :] = jnp.zeros((blk, HEAD_DIM), F32)
        dst[blk:, :] = src[...].astype(F32)
    dn_qk = (((2,), (2,)), ((0,), (0,)))
    dn_pv = (((2,), (1,)), ((0,), (0,)))

    def rows(ref, start, dil):
        if dil == 1:
            return ref[pl.ds(start, blk), :]
        return ref[pl.ds(start, blk, stride=dil), :]

    for p, (_, dil) in enumerate(DIL_PATTERNS):
        nblk = seq // dil // blk
        blocks = [(jb, r) for jb in range(nblk) for r in range(dil)]
        bias = bias_ref[p]
        for g0 in range(0, len(blocks), DIL_GROUP):
            group = blocks[g0:g0 + DIL_GROUP]
            starts = [jb * blk * dil + r for jb, r in group]
            has_prev = [jb > 0 for jb, _ in group]
            q = jnp.stack([rows(qf, st, dil) for st in starts]).astype(BF16)
            kc = jnp.stack([rows(kf, blk + st, dil) for st in starts]).astype(BF16)
            vc = jnp.stack([rows(vf, blk + st, dil) for st in starts]).astype(BF16)
            s_cur = lax.dot_general(q, kc, dn_qk, preferred_element_type=F32) + bias[:, blk:]
            m = jnp.max(s_cur, axis=-1, keepdims=True)
            if any(has_prev):
                prev_starts = [max(blk + st - blk * dil, 0) for st in starts]
                kp = jnp.stack([rows(kf, st, dil) for st in prev_starts]).astype(BF16)
                vp = jnp.stack([rows(vf, st, dil) for st in prev_starts]).astype(BF16)
                s_prev = lax.dot_general(q, kp, dn_qk, preferred_element_type=F32) + bias[:, :blk]
                which = lax.broadcasted_iota(jnp.int32, s_prev.shape, 0)
                for gi, ok in enumerate(has_prev):
                    if not ok:
                        s_prev = jnp.where(which == gi, MASKED, s_prev)
                m = jnp.maximum(m, jnp.max(s_prev, axis=-1, keepdims=True))
                p_prev = jnp.exp(s_prev - m)
            p_cur = jnp.exp(s_cur - m)
            l = jnp.sum(p_cur, axis=-1, keepdims=True)
            acc = lax.dot_general(p_cur.astype(BF16), vc, dn_pv, preferred_element_type=F32)
            if any(has_prev):
                l = l + jnp.sum(p_prev, axis=-1, keepdims=True)
                acc = acc + lax.dot_general(p_prev.astype(BF16), vp, dn_pv, preferred_element_type=F32)
            out = acc / l
            lse = jnp.broadcast_to(m + jnp.log(l), out.shape)
            for gi, st in enumerate(starts):
                if dil == 1:
                    o_scr[p, pl.ds(st, blk), :] = out[gi]
                    lse_scr[p, pl.ds(st, blk), :] = lse[gi]
                else:
                    o_scr[p, pl.ds(st, blk, stride=dil), :] = out[gi]
                    lse_scr[p, pl.ds(st, blk, stride=dil), :] = lse[gi]

    lses = [lse_scr[p] for p in range(len(DIL_PATTERNS))]
    top = functools.reduce(jnp.maximum, lses)
    es = [jnp.exp(x - top) for x in lses]
    den = functools.reduce(lambda a, b: a + b, es)
    mixed = functools.reduce(lambda a, b: a + b, [(e / den) * o_scr[p] for p, e in enumerate(es)])
    o_ref[...] = mixed.astype(o_ref.dtype)


def _dilated_attention(qkv, bias_tiles, bsz, seq):
    t = qkv.shape[1]
    h16 = DIL_HEADS
    npat = len(DIL_PATTERNS)
    blk = (None, seq, HEAD_DIM)
    return pl.pallas_call(
        _dilated_kernel,
        grid=(bsz, DIL_HEADS),
        in_specs=[
            pl.BlockSpec(blk, lambda b, h: (h, b, 0)),
            pl.BlockSpec(blk, lambda b, h: (h16 + h, b, 0)),
            pl.BlockSpec(blk, lambda b, h: (2 * h16 + h, b, 0)),
            pl.BlockSpec((npat, None, DIL_BLOCK, 2 * DIL_BLOCK), lambda b, h: (0, h, 0, 0)),
        ],
        out_specs=pl.BlockSpec(blk, lambda b, h: (h, b, 0)),
        out_shape=jax.ShapeDtypeStruct((DIL_HEADS, t, HEAD_DIM), BF16),
        scratch_shapes=[
            pltpu.VMEM((seq, HEAD_DIM), F32),
            pltpu.VMEM((DIL_BLOCK + seq, HEAD_DIM), F32),
            pltpu.VMEM((DIL_BLOCK + seq, HEAD_DIM), F32),
            pltpu.VMEM((npat, seq, HEAD_DIM), F32),
            pltpu.VMEM((npat, seq, HEAD_DIM), F32),
        ],
        compiler_params=_cparams("parallel", "parallel"),
        name="dilated_attention",
    )(qkv, qkv, qkv, bias_tiles)


def _topk_rows(s, val_ref, pick_ref, payload=None):
    neg_rows = -lax.broadcasted_iota(jnp.int32, s.shape, 0).astype(F32)
    for k in range(PEER_TOPK):
        m = jnp.max(s, axis=0, keepdims=True)
        am = jnp.max(jnp.where(s == m, neg_rows, -jnp.inf), axis=0, keepdims=True)
        hit = neg_rows == am
        val_ref[k:k + 1, :] = m
        if payload is None:
            pick_ref[k:k + 1, :] = (-am).astype(jnp.int32)
        else:
            pick_ref[k:k + 1, :] = jnp.sum(jnp.where(hit, payload, 0), axis=0, keepdims=True)
        s = jnp.where(hit, -jnp.inf, s)


def _candidate_rows(a):
    need = PEER_TOPK // (a + 1)
    if need > SUBLANES:
        return PEER_TOPK
    return SUBLANES if need > 1 else 1


def _route_kernel(q_ref, keys_ref, idx_ref, gate_ref, v0, i0, v1, i1, best_ref):
    dn = (((1,), (1,)), ((), ()))
    s0 = lax.dot_general(keys_ref[0], q_ref[0], dn, preferred_element_type=F32)
    s1 = lax.dot_general(keys_ref[1], q_ref[1], dn, preferred_element_type=F32)
    _topk_rows(s0, v0, i0)
    _topk_rows(s1, v1, i1)
    first_single = min(a for a in range(PEER_TOPK) if _candidate_rows(a) == 1)
    cand_s, cand_i = [], []
    for a in range(first_single):
        nb = _candidate_rows(a)
        cand_s.append(v0[a:a + 1, :] + v1[0:nb, :])
        cand_i.append(i0[a:a + 1, :] * PEER_NKEYS + i1[0:nb, :])
    cand_s.append(v0[first_single:, :] + v1[0:1, :])
    cand_i.append(i0[first_single:, :] * PEER_NKEYS + i1[0:1, :])
    _topk_rows(jnp.concatenate(cand_s, axis=0), best_ref, idx_ref, jnp.concatenate(cand_i, axis=0))
    best_s = best_ref[...]
    e = jnp.exp(best_s - best_s[0:1, :])
    gate_ref[...] = e / jnp.sum(e, axis=0, keepdims=True)


def _peer_route(qp, sub_keys):
    t = qp.shape[1]
    tm = ROUTE_TM
    out_spec = pl.BlockSpec((None, PEER_TOPK, tm), lambda i, h: (h, 0, i))
    return pl.pallas_call(
        _route_kernel,
        grid=(t // tm, PEER_HEADS),
        in_specs=[
            pl.BlockSpec((2, tm, PEER_DKEY // 2), lambda i, h: (h, i, 0)),
            pl.BlockSpec((2, PEER_NKEYS, PEER_DKEY // 2), lambda i, h: (0, 0, 0)),
        ],
        out_specs=[out_spec, out_spec],
        out_shape=[
            jax.ShapeDtypeStruct((PEER_HEADS, PEER_TOPK, t), jnp.int32),
            jax.ShapeDtypeStruct((PEER_HEADS, PEER_TOPK, t), F32),
        ],
        scratch_shapes=[
            pltpu.VMEM((PEER_TOPK, tm), F32), pltpu.VMEM((PEER_TOPK, tm), jnp.int32),
            pltpu.VMEM((PEER_TOPK, tm), F32), pltpu.VMEM((PEER_TOPK, tm), jnp.int32),
            pltpu.VMEM((PEER_TOPK, tm), F32),
        ],
        compiler_params=_cparams("parallel", "parallel"),
        name="peer_route",
    )(qp, sub_keys)


def _gelu_exact(a):
    return 0.5 * a * (1.0 + lax.erf(a * (2.0 ** -0.5)))


def _pack_expert_tables(table_u, table_v):
    e, d = table_u.shape
    half = d // 2

    def pack(tbl):
        bits = lax.bitcast_convert_type(tbl.astype(BF16), jnp.uint16).astype(jnp.uint32)
        return (bits[:, half:] << 16) | bits[:, :half]

    return jnp.concatenate([pack(table_u), pack(table_v)], axis=1).reshape(e, 1, d)


def _unpack_words(w):
    lo = lax.bitcast_convert_type(w << 16, F32)
    hi = lax.bitcast_convert_type(w & jnp.uint32(0xFFFF0000), F32)
    return lo, hi


PEER_NBUF = 4
PEER_LOOKAHEAD = 2


def _expert_kernel(idx_ref, idx_next_ref, x_ref, gain_ref, sc_ref, sh_ref, g2_ref, gate_ref,
                   tab_hbm, o_ref, *scratch):
    bufs, sems = scratch[:PEER_NBUF], scratch[PEER_NBUF]
    s = pl.program_id(0)
    n_steps = pl.num_programs(0)
    gt = x_ref.shape[0] // PEER_NBUF
    d = x_ref.shape[1]
    ngroup = PEER_PICKS // SUBLANES
    nword = d // 2 // LANES
    gain, sc, sh, g2 = gain_ref[...], sc_ref[...], sh_ref[...], g2_ref[...]
    eye = (lax.broadcasted_iota(jnp.int32, (PEER_PICKS, PEER_PICKS), 0)
           == lax.broadcasted_iota(jnp.int32, (PEER_PICKS, PEER_PICKS), 1))

    def start_token(iref, tok, j, ti):
        for g in range(ngroup):
            for k in range(SUBLANES):
                e = iref[0, tok, g * SUBLANES + k]
                pltpu.make_async_copy(
                    tab_hbm.at[e],
                    bufs[j].at[pl.ds((ti * ngroup + g) * SUBLANES + k, 1)],
                    sems.at[j]).start(priority=k % 2)

    def wait_group(j):
        pltpu.make_async_copy(bufs[j], bufs[j], sems.at[j]).wait()

    def gated_activation(buf, ti, tok):
        h = _modulated_norm(x_ref[pl.ds(tok, 1), :], gain, sc, sh)
        hb = jnp.broadcast_to(h, (SUBLANES, d))
        parts = []
        for g in range(ngroup):
            acc = None
            for c in range(nword):
                lo, hi = _unpack_words(buf[pl.ds((ti * ngroup + g) * SUBLANES, SUBLANES),
                                           c * LANES:(c + 1) * LANES])
                term = (lo * hb[:, c * LANES:(c + 1) * LANES]
                        + hi * hb[:, (nword + c) * LANES:(nword + c + 1) * LANES])
                acc = term if acc is None else acc + term
            parts.append(acc)
        act = jnp.sum(jnp.concatenate(parts, axis=0), axis=-1, keepdims=True)
        gate_col = jnp.sum(jnp.where(eye, gate_ref[pl.ds(tok, 1), :], 0.0), axis=-1, keepdims=True)
        return gate_col * _gelu_exact(act)

    def weighted_values(buf, ti, tok, a):
        ylo = [None] * nword
        yhi = [None] * nword
        for g in range(ngroup):
            ag = a[g * SUBLANES:(g + 1) * SUBLANES, :]
            for c in range(nword):
                lo, hi = _unpack_words(buf[pl.ds((ti * ngroup + g) * SUBLANES, SUBLANES),
                                           (nword + c) * LANES:(nword + c + 1) * LANES])
                ylo[c] = lo * ag if ylo[c] is None else ylo[c] + lo * ag
                yhi[c] = hi * ag if yhi[c] is None else yhi[c] + hi * ag
        y = jnp.sum(jnp.concatenate(ylo + yhi, axis=1), axis=0, keepdims=True)
        o_ref[pl.ds(tok, 1), :] = x_ref[pl.ds(tok, 1), :] + g2 * y

    @pl.when(s == 0)
    def _():
        for j in range(PEER_LOOKAHEAD):
            def body(ti, carry, j=j):
                start_token(idx_ref, j * gt + ti, j, ti)
                return carry
            lax.fori_loop(0, gt, body, 0)

    pending = None
    for j in range(PEER_NBUF):
        wait_group(j)
        ahead = j + PEER_LOOKAHEAD
        iref = idx_ref if ahead < PEER_NBUF else idx_next_ref
        for ti in range(gt):
            tok = j * gt + ti
            start_token(iref, (ahead % PEER_NBUF) * gt + ti, ahead % PEER_NBUF, ti)
            a = gated_activation(bufs[j], ti, tok)
            if pending is not None:
                weighted_values(*pending)
            pending = (bufs[j], ti, tok, a)
    weighted_values(*pending)

    @pl.when(s == n_steps - 1)
    def _():
        for j in range(PEER_LOOKAHEAD):
            wait_group(j)


def _peer_experts(x2, gain, sc, sh, g2, idx, gates, packed, seq):
    t, d = x2.shape
    tp = PEER_TP
    n_steps = t // tp
    idx3 = idx.reshape(n_steps, tp, PEER_PICKS)
    per_b = seq // tp
    smem_blk = (1, tp, PEER_PICKS)
    return pl.pallas_call(
        _expert_kernel,
        grid=(n_steps,),
        in_specs=[
            pl.BlockSpec(smem_blk, lambda s: (s, 0, 0), memory_space=pltpu.SMEM),
            pl.BlockSpec(smem_blk, lambda s: (jnp.minimum(s + 1, n_steps - 1), 0, 0),
                         memory_space=pltpu.SMEM),
            pl.BlockSpec((tp, d), lambda s: (s, 0)),
            pl.BlockSpec((1, d), lambda s: (0, 0)),
            pl.BlockSpec((None, 1, d), lambda s: (s // per_b, 0, 0)),
            pl.BlockSpec((None, 1, d), lambda s: (s // per_b, 0, 0)),
            pl.BlockSpec((None, 1, d), lambda s: (s // per_b, 0, 0)),
            pl.BlockSpec((tp, PEER_PICKS), lambda s: (s, 0)),
            pl.BlockSpec(memory_space=pl.ANY),
        ],
        out_specs=pl.BlockSpec((tp, d), lambda s: (s, 0)),
        out_shape=jax.ShapeDtypeStruct((t, d), F32),
        scratch_shapes=(
            [pltpu.VMEM((tp // PEER_NBUF * PEER_PICKS, d), jnp.uint32)] * PEER_NBUF
            + [pltpu.SemaphoreType.DMA((PEER_NBUF,))]),
        compiler_params=_cparams("arbitrary"),
        name="peer_experts",
    )(idx3, idx3, x2, gain.reshape(1, d), sc, sh, g2, gates, packed)


def _peer_ffn(x2, gain, sc, sh, g2, w_query, sub_keys, table_u, table_v, seq):
    t, d = x2.shape
    no_gain = jnp.ones((1, w_query.shape[1]), F32)
    qp = _norm_proj(x2, gain, sc, sh, w_query.astype(BF16), no_gain, (), seq)
    idx, gates = _peer_route(qp, sub_keys.astype(BF16))
    idx = idx.transpose(2, 0, 1).reshape(t, PEER_PICKS)
    gates = gates.transpose(2, 0, 1).reshape(t, PEER_PICKS)
    packed = _pack_expert_tables(table_u, table_v)
    return _peer_experts(x2, gain, sc, sh, g2, idx, gates, packed, seq)


def _tile_gain(gain, heads, scale=1.0):
    return jnp.tile(gain * scale, heads)


def _even_layer(x2, mods, norm_gain, w_in, b_forget, fox_qk_gain, diff_qk_gain, diff_lambda,
                diff_subln_gain, w_out, diff_bias, lam_init, bsz, seq):
    sh1, sc1, g1 = mods
    w_main = jnp.concatenate([w_in[:, :3 * FOX_W], w_in[:, 3 * FOX_W + FOX_HEADS:]], axis=1).astype(BF16)
    w_forget = jnp.zeros((D_MODEL, LANES), F32).at[:, :FOX_HEADS].set(
        w_in[:, 3 * FOX_W:3 * FOX_W + FOX_HEADS]).astype(BF16)
    ones = jnp.ones((FOX_W,), F32)
    head_gain = jnp.concatenate([
        _tile_gain(fox_qk_gain[0], FOX_HEADS, ATTN_SCALE), _tile_gain(fox_qk_gain[1], FOX_HEADS), ones,
        _tile_gain(diff_qk_gain[0], 2 * DIFF_HEADS, ATTN_SCALE), _tile_gain(diff_qk_gain[1], 2 * DIFF_HEADS),
        ones]).reshape(1, -1)
    proj, f_side = _norm_proj(x2, norm_gain, sc1, sh1, w_main, head_gain, (0, 1, 3, 4), seq,
                              w_side=w_forget)
    fcol, frow = _forget_cumsum(f_side, b_forget, bsz, seq)
    fox_o = _fox_attention(proj, frow, fcol, bsz, seq)
    bias_tiles = _causal_bias_tiles(diff_bias, seq, ATTN_T)
    diff_o = _diff_attention(proj, bias_tiles, diff_lambda, diff_subln_gain, lam_init, bsz, seq)
    return _out_proj([fox_o, diff_o], w_out.astype(BF16), x2, g1, seq)


def _odd_layer(x2, mods, norm_gain, w_qkv, qk_gain, w_out, bias_table, bsz, seq):
    sh1, sc1, g1 = mods
    ones = jnp.ones((D_MODEL,), F32)
    head_gain = jnp.concatenate([
        _tile_gain(qk_gain[0], DIL_HEADS, ATTN_SCALE), _tile_gain(qk_gain[1], DIL_HEADS), ones]).reshape(1, -1)
    qkv = _norm_proj(x2, norm_gain, sc1, sh1, w_qkv.astype(BF16), head_gain, (0, 1, 2, 3), seq)
    mixed = _dilated_attention(qkv, _dilated_bias_tiles(bias_table), bsz, seq)
    return _out_proj([mixed], w_out.astype(BF16), x2, g1, seq)


def kernel(x, c, rel_bias, norm_gain, w_ada, b_ada, even_w_in, even_b_forget, even_fox_qk_gain,
           even_diff_qk_gain, even_diff_lambda, even_diff_subln_gain, even_w_out, odd_w_qkv,
           odd_qk_gain, odd_w_out, peer_w_query, peer_sub_keys, peer_u, peer_v):
    bsz, seq, d = x.shape
    x2 = x.reshape(bsz * seq, d)
    mod = _adaln(c, w_ada, b_ada)
    for i in range(DEPTH):
        sh1, sc1, g1, sh2, sc2, g2 = [m.reshape(bsz, 1, d) for m in jnp.split(mod[i], 6, axis=-1)]
        j = i // 2
        if i % 2 == 0:
            lam_init = 0.8 - 0.6 * math.exp(-0.3 * i)
            x2 = _even_layer(x2, (sh1, sc1, g1), norm_gain[i, 0], even_w_in[j], even_b_forget[j],
                             even_fox_qk_gain[j], even_diff_qk_gain[j], even_diff_lambda[j],
                             even_diff_subln_gain[j], even_w_out[j], rel_bias[:, :DIFF_HEADS],
                             lam_init, bsz, seq)
        else:
            x2 = _odd_layer(x2, (sh1, sc1, g1), norm_gain[i, 0], odd_w_qkv[j], odd_qk_gain[j],
                            odd_w_out[j], rel_bias, bsz, seq)
        x2 = _peer_ffn(x2, norm_gain[i, 1], sc2, sh2, g2, peer_w_query[i], peer_sub_keys[i],
                       peer_u[i], peer_v[i], seq)
    return x2.reshape(bsz, seq, d)
```

```python
import functools
import math

import numpy as np
import jax
import jax.numpy as jnp
from jax import lax
from jax.experimental import pallas as pl
from jax.experimental.pallas import tpu as pltpu

F32 = jnp.float32
BF16 = jnp.bfloat16

D_MODEL = 2048
DEPTH = 2
HEAD_DIM = 128
FOX_HEADS = 8
DIFF_HEADS = 4
DIFF_V_DIM = 2 * HEAD_DIM
DIL_HEADS = D_MODEL // HEAD_DIM
DIL_PATTERNS = ((128, 1), (512, 4), (2048, 16))
DIL_BLOCK = 128
N_BUCKETS = 32
BUCKET_MAX_EXACT = 16
BUCKET_MAX_DIST = 2048
PEER_HEADS = 8
PEER_NKEYS = 128
PEER_EXPERTS = PEER_NKEYS * PEER_NKEYS
PEER_DKEY = 256
PEER_TOPK = 16
PEER_PICKS = PEER_HEADS * PEER_TOPK
NORM_EPS = 1e-6
FOX_W = FOX_HEADS * HEAD_DIM
DIFF_QK_W = DIFF_HEADS * 2 * HEAD_DIM
DIFF_V_W = DIFF_HEADS * DIFF_V_DIM
ATTN_SCALE = HEAD_DIM ** -0.5
MASKED = -1e30

LANES = 128
SUBLANES = 8
VMEM_LIMIT = 48 * 1024 * 1024

PROJ_TM = 512
PROJ_TN = 1024
ATTN_T = 512
ROUTE_TM = 128
PEER_TP = 16


def _cparams(*sem):
    return pltpu.CompilerParams(dimension_semantics=sem, vmem_limit_bytes=VMEM_LIMIT)


def _adaln_kernel(c_ref, w_ref, b_ref, o_ref):
    c = c_ref[...]
    cond = c * jax.nn.sigmoid(c)
    o_ref[...] = jnp.dot(cond.astype(BF16), w_ref[...].astype(BF16),
                         preferred_element_type=F32) + b_ref[...]


def _adaln(c, w_ada, b_ada):
    depth, d, n = w_ada.shape
    bsz = c.shape[0]
    tn = 1024
    return pl.pallas_call(
        _adaln_kernel,
        grid=(depth, n // tn),
        in_specs=[
            pl.BlockSpec((bsz, d), lambda i, j: (0, 0)),
            pl.BlockSpec((None, d, tn), lambda i, j: (i, 0, j)),
            pl.BlockSpec((None, 1, tn), lambda i, j: (i, 0, j)),
        ],
        out_specs=pl.BlockSpec((None, bsz, tn), lambda i, j: (i, 0, j)),
        out_shape=jax.ShapeDtypeStruct((depth, bsz, n), F32),
        compiler_params=_cparams("parallel", "parallel"),
        name="adaln",
    )(c, w_ada, b_ada.reshape(depth, 1, n))


def _modulated_norm(x, gain, sc, sh):
    ms = jnp.mean(x * x, axis=-1, keepdims=True)
    y = x * lax.rsqrt(ms + NORM_EPS) * gain
    return y * (1.0 + sc) + sh


def _norm_proj_kernel(*refs, norm_tiles, has_side):
    if has_side:
        x_ref, gain_ref, sc_ref, sh_ref, w_ref, hg_ref, ws_ref, o_ref, side_ref, h_ref = refs
    else:
        x_ref, gain_ref, sc_ref, sh_ref, w_ref, hg_ref, o_ref, h_ref = refs
    j = pl.program_id(1)

    @pl.when(j == 0)
    def _():
        h = _modulated_norm(x_ref[...], gain_ref[...], sc_ref[...], sh_ref[...])
        h_ref[...] = h.astype(BF16)
        if has_side:
            side_ref[...] = jnp.dot(h_ref[...], ws_ref[...], preferred_element_type=F32)

    acc = jnp.dot(h_ref[...], w_ref[...], preferred_element_type=F32)
    tn = acc.shape[1]

    def write_normed():
        for g in range(tn // HEAD_DIM):
            sl = slice(g * HEAD_DIM, (g + 1) * HEAD_DIM)
            blk = acc[:, sl]
            ms = jnp.mean(blk * blk, axis=-1, keepdims=True)
            o_ref[g] = (blk * lax.rsqrt(ms + NORM_EPS) * hg_ref[:, sl]).astype(o_ref.dtype)

    def write_raw():
        for g in range(tn // HEAD_DIM):
            o_ref[g] = acc[:, g * HEAD_DIM:(g + 1) * HEAD_DIM].astype(o_ref.dtype)

    if not norm_tiles:
        write_raw()
    else:
        is_norm = functools.reduce(jnp.logical_or, [j == t for t in norm_tiles])
        pl.when(is_norm)(write_normed)
        pl.when(jnp.logical_not(is_norm))(write_raw)


def _norm_proj(x2, gain, sc, sh, w, head_gain, norm_tiles, seq, w_side=None):
    t, d = x2.shape
    n = w.shape[1]
    tm, tn = PROJ_TM, PROJ_TN
    per_b = seq // tm
    has_side = w_side is not None
    gpt = tn // HEAD_DIM
    in_specs = [
        pl.BlockSpec((tm, d), lambda i, j: (i, 0)),
        pl.BlockSpec((1, d), lambda i, j: (0, 0)),
        pl.BlockSpec((None, 1, d), lambda i, j: (i // per_b, 0, 0)),
        pl.BlockSpec((None, 1, d), lambda i, j: (i // per_b, 0, 0)),
        pl.BlockSpec((d, tn), lambda i, j: (0, j)),
        pl.BlockSpec((1, tn), lambda i, j: (0, j)),
    ]
    args = [x2, gain.reshape(1, d), sc, sh, w, head_gain]
    out_specs = pl.BlockSpec((gpt, tm, HEAD_DIM), lambda i, j: (j, i, 0))
    out_shape = jax.ShapeDtypeStruct((n // HEAD_DIM, t, HEAD_DIM), BF16)
    if has_side:
        ns = w_side.shape[1]
        in_specs.append(pl.BlockSpec((d, ns), lambda i, j: (0, 0)))
        args.append(w_side)
        out_specs = [out_specs, pl.BlockSpec((tm, ns), lambda i, j: (i, 0))]
        out_shape = [out_shape, jax.ShapeDtypeStruct((t, ns), F32)]
    return pl.pallas_call(
        functools.partial(_norm_proj_kernel, norm_tiles=tuple(norm_tiles), has_side=has_side),
        grid=(t // tm, n // tn),
        in_specs=in_specs,
        out_specs=out_specs,
        out_shape=out_shape,
        scratch_shapes=[pltpu.VMEM((tm, d), BF16)],
        compiler_params=_cparams("parallel", "arbitrary"),
        name="norm_proj",
    )(*args)


def _out_proj_kernel(*refs):
    *a_refs, w_ref, x_ref, g_ref, o_ref = refs
    a = jnp.concatenate([a_ref[g] for a_ref in a_refs for g in range(a_ref.shape[0])], axis=1)
    y = jnp.dot(a, w_ref[...], preferred_element_type=F32)
    o_ref[...] = x_ref[...] + g_ref[...] * y


def _out_proj(heads_major, w, x2, g, seq):
    t = x2.shape[0]
    k, n = w.shape
    assert sum(a.shape[0] for a in heads_major) * HEAD_DIM == k
    tm, tn = PROJ_TM, PROJ_TN
    per_b = seq // tm
    return pl.pallas_call(
        _out_proj_kernel,
        grid=(t // tm, n // tn),
        in_specs=[pl.BlockSpec((a.shape[0], tm, HEAD_DIM), lambda i, j: (0, i, 0)) for a in heads_major] + [
            pl.BlockSpec((k, tn), lambda i, j: (0, j)),
            pl.BlockSpec((tm, tn), lambda i, j: (i, j)),
            pl.BlockSpec((None, 1, tn), lambda i, j: (i // per_b, 0, j)),
        ],
        out_specs=pl.BlockSpec((tm, tn), lambda i, j: (i, j)),
        out_shape=jax.ShapeDtypeStruct((t, n), F32),
        compiler_params=_cparams("parallel", "parallel"),
        name="out_proj",
    )(*heads_major, w, x2, g)


def _forget_kernel(f_ref, b_ref, col_ref, row_ref):
    z = f_ref[...] + b_ref[...]
    lf = jnp.minimum(z, 0.0) - jnp.log1p(jnp.exp(-jnp.abs(z)))
    s = lf.shape[0]
    pos = lax.broadcasted_iota(jnp.int32, lf.shape, 0)
    d = 1
    while d < s:
        lf = lf + jnp.where(pos >= d, pltpu.roll(lf, d, axis=0), 0.0)
        d *= 2
    col_ref[...] = lf
    row_ref[...] = lf.T[:FOX_HEADS, :]


def _forget_cumsum(f_side, b_forget, bsz, seq):
    bpad = jnp.zeros((1, LANES), F32).at[0, :FOX_HEADS].set(b_forget)
    col, row = pl.pallas_call(
        _forget_kernel,
        grid=(bsz,),
        in_specs=[
            pl.BlockSpec((None, seq, LANES), lambda b: (b, 0, 0)),
            pl.BlockSpec((1, LANES), lambda b: (0, 0)),
        ],
        out_specs=[
            pl.BlockSpec((None, seq, LANES), lambda b: (b, 0, 0)),
            pl.BlockSpec((None, FOX_HEADS, seq), lambda b: (b, 0, 0)),
        ],
        out_shape=[
            jax.ShapeDtypeStruct((bsz, seq, LANES), F32),
            jax.ShapeDtypeStruct((bsz, FOX_HEADS, seq), F32),
        ],
        compiler_params=_cparams("parallel"),
        name="forget_cumsum",
    )(f_side.reshape(bsz, seq, LANES), bpad)
    return col, row.reshape(bsz, FOX_HEADS, 1, seq)


def _online_softmax_step(s, v, m_ref, l_ref, acc_ref):
    m_prev = m_ref[...]
    m_new = jnp.maximum(m_prev, jnp.max(s, axis=-1, keepdims=True))
    alpha = jnp.exp(m_prev - m_new)
    p = jnp.exp(s - m_new)
    l_ref[...] = alpha * l_ref[...] + jnp.sum(p, axis=-1, keepdims=True)
    acc_ref[...] = alpha * acc_ref[...] + jnp.dot(p.astype(v.dtype), v, preferred_element_type=F32)
    m_ref[...] = m_new


def _fox_kernel(q_ref, k_ref, v_ref, frow_ref, fcol_ref, o_ref, m_ref, l_ref, acc_ref, fq_ref):
    h = pl.program_id(1)
    qi = pl.program_id(2)
    ki = pl.program_id(3)
    tq, tk = q_ref.shape[0], k_ref.shape[0]

    @pl.when(ki == 0)
    def _():
        m_ref[...] = jnp.full_like(m_ref, MASKED)
        l_ref[...] = jnp.zeros_like(l_ref)
        acc_ref[...] = jnp.zeros_like(acc_ref)
        lane = lax.broadcasted_iota(jnp.int32, fcol_ref.shape, 1)
        fq_ref[...] = jnp.sum(jnp.where(lane == h, fcol_ref[...], 0.0), axis=-1, keepdims=True)

    @pl.when(ki <= qi)
    def _():
        s = lax.dot_general(q_ref[...], k_ref[...], (((1,), (1,)), ((), ())),
                            preferred_element_type=F32)
        s = s + fq_ref[...] - frow_ref[...]
        qpos = qi * tq + lax.broadcasted_iota(jnp.int32, s.shape, 0)
        kpos = ki * tk + lax.broadcasted_iota(jnp.int32, s.shape, 1)
        s = jnp.where(qpos >= kpos, s, MASKED)
        _online_softmax_step(s, v_ref[...], m_ref, l_ref, acc_ref)

    @pl.when(ki == qi)
    def _():
        o_ref[...] = (acc_ref[...] / l_ref[...]).astype(o_ref.dtype)


def _fox_attention(proj, frow, fcol, bsz, seq):
    t = proj.shape[1]
    tt = ATTN_T
    nb = seq // tt
    kcol = FOX_W // HEAD_DIM
    blk = (None, tt, HEAD_DIM)
    return pl.pallas_call(
        _fox_kernel,
        grid=(bsz, FOX_HEADS, nb, nb),
        in_specs=[
            pl.BlockSpec(blk, lambda b, h, qi, ki: (h, b * nb + qi, 0)),
            pl.BlockSpec(blk, lambda b, h, qi, ki: (kcol + h, b * nb + jnp.minimum(ki, qi), 0)),
            pl.BlockSpec(blk, lambda b, h, qi, ki: (2 * kcol + h, b * nb + jnp.minimum(ki, qi), 0)),
            pl.BlockSpec((None, None, 1, tt), lambda b, h, qi, ki: (b, h, 0, jnp.minimum(ki, qi))),
            pl.BlockSpec((None, tt, LANES), lambda b, h, qi, ki: (b, qi, 0)),
        ],
        out_specs=pl.BlockSpec(blk, lambda b, h, qi, ki: (h, b * nb + qi, 0)),
        out_shape=jax.ShapeDtypeStruct((FOX_HEADS, t, HEAD_DIM), BF16),
        scratch_shapes=[
            pltpu.VMEM((tt, 1), F32), pltpu.VMEM((tt, 1), F32),
            pltpu.VMEM((tt, HEAD_DIM), F32), pltpu.VMEM((tt, 1), F32),
        ],
        compiler_params=_cparams("parallel", "parallel", "parallel", "arbitrary"),
        name="fox_attention",
    )(proj, proj, proj, frow, fcol)


def _diff_kernel(q_ref, k_ref, v_ref, bias_ref, lam_ref, sg_ref, o_ref,
                 m0, l0, a0, m1, l1, a1, *, lam_init):
    qi = pl.program_id(2)
    ki = pl.program_id(3)

    @pl.when(ki == 0)
    def _():
        for m_ref, l_ref, a_ref in ((m0, l0, a0), (m1, l1, a1)):
            m_ref[...] = jnp.full_like(m_ref, MASKED)
            l_ref[...] = jnp.zeros_like(l_ref)
            a_ref[...] = jnp.zeros_like(a_ref)

    @pl.when(ki <= qi)
    def _():
        bias = bias_ref[...]
        v = jnp.concatenate([v_ref[0], v_ref[1]], axis=1)
        for mi, (m_ref, l_ref, a_ref) in enumerate(((m0, l0, a0), (m1, l1, a1))):
            s = lax.dot_general(q_ref[mi], k_ref[mi], (((1,), (1,)), ((), ())),
                                preferred_element_type=F32) + bias
            _online_softmax_step(s, v, m_ref, l_ref, a_ref)

    @pl.when(ki == qi)
    def _():
        lf = lam_ref[...]
        lam = (jnp.exp(jnp.sum(lf[0:1] * lf[1:2], axis=-1, keepdims=True))
               - jnp.exp(jnp.sum(lf[2:3] * lf[3:4], axis=-1, keepdims=True)) + lam_init)
        o = a0[...] / l0[...] - lam * (a1[...] / l1[...])
        ms = jnp.mean(o * o, axis=-1, keepdims=True)
        o = (o * lax.rsqrt(ms + NORM_EPS) * sg_ref[...] * (1.0 - lam_init)).astype(o_ref.dtype)
        o_ref[0] = o[:, :HEAD_DIM]
        o_ref[1] = o[:, HEAD_DIM:]


def _diff_attention(proj, bias_tiles, lam_param, subln_gain, lam_init, bsz, seq):
    t = proj.shape[1]
    tt = ATTN_T
    nb = seq // tt
    qcol = 3 * FOX_W // DIFF_V_DIM
    kcol = qcol + DIFF_QK_W // DIFF_V_DIM
    vcol = kcol + DIFF_QK_W // DIFF_V_DIM
    blk = (2, tt, HEAD_DIM)
    return pl.pallas_call(
        functools.partial(_diff_kernel, lam_init=lam_init),
        grid=(bsz, DIFF_HEADS, nb, nb),
        in_specs=[
            pl.BlockSpec(blk, lambda b, h, qi, ki: (qcol + h, b * nb + qi, 0)),
            pl.BlockSpec(blk, lambda b, h, qi, ki: (kcol + h, b * nb + jnp.minimum(ki, qi), 0)),
            pl.BlockSpec(blk, lambda b, h, qi, ki: (vcol + h, b * nb + jnp.minimum(ki, qi), 0)),
            pl.BlockSpec((None, None, tt, tt), lambda b, h, qi, ki: (h, jnp.maximum(qi - ki, 0), 0, 0)),
            pl.BlockSpec((4, HEAD_DIM), lambda b, h, qi, ki: (0, 0)),
            pl.BlockSpec((1, DIFF_V_DIM), lambda b, h, qi, ki: (0, 0)),
        ],
        out_specs=pl.BlockSpec(blk, lambda b, h, qi, ki: (h, b * nb + qi, 0)),
        out_shape=jax.ShapeDtypeStruct((2 * DIFF_HEADS, t, HEAD_DIM), BF16),
        scratch_shapes=[
            pltpu.VMEM((tt, 1), F32), pltpu.VMEM((tt, 1), F32), pltpu.VMEM((tt, DIFF_V_DIM), F32),
            pltpu.VMEM((tt, 1), F32), pltpu.VMEM((tt, 1), F32), pltpu.VMEM((tt, DIFF_V_DIM), F32),
        ],
        compiler_params=_cparams("parallel", "parallel", "parallel", "arbitrary"),
        name="diff_attention",
    )(proj, proj, proj, bias_tiles, lam_param, subln_gain.reshape(1, DIFF_V_DIM))


def _t5_bucket_np(dist):
    n = np.maximum(dist, 0)
    nf = np.maximum(n, 1).astype(np.float32)
    large = BUCKET_MAX_EXACT + (np.log(nf / np.float32(BUCKET_MAX_EXACT))
                                / np.float32(math.log(BUCKET_MAX_DIST / BUCKET_MAX_EXACT))
                                * np.float32(N_BUCKETS - BUCKET_MAX_EXACT)).astype(np.int32)
    large = np.minimum(large, N_BUCKETS - 1)
    return np.where(n < BUCKET_MAX_EXACT, n, large).astype(np.int32)


def _bucket_thresholds():
    buckets = _t5_bucket_np(np.arange(BUCKET_MAX_DIST + 1))
    assert np.all(np.diff(buckets) >= 0)
    return [int(np.argmax(buckets >= b)) for b in range(N_BUCKETS)]


_BUCKET_THRESHOLDS = _bucket_thresholds()


def _bias_from_distance(dist, valid, table_ref, h):
    val = jnp.full(dist.shape, table_ref[0, h], F32)
    for b in range(1, N_BUCKETS):
        val = jnp.where(dist >= _BUCKET_THRESHOLDS[b], table_ref[b, h], val)
    return jnp.where(valid, val, MASKED)


def _causal_bias_kernel(table_ref, o_ref):
    h, delta = pl.program_id(0), pl.program_id(1)
    tt = o_ref.shape[0]
    dist = (delta * tt + lax.broadcasted_iota(jnp.int32, o_ref.shape, 0)
            - lax.broadcasted_iota(jnp.int32, o_ref.shape, 1))
    o_ref[...] = _bias_from_distance(dist, dist >= 0, table_ref, h)


def _causal_bias_tiles(table, seq, tt):
    nb = seq // tt
    heads = table.shape[1]
    return pl.pallas_call(
        _causal_bias_kernel,
        grid=(heads, nb),
        in_specs=[pl.BlockSpec(memory_space=pltpu.SMEM)],
        out_specs=pl.BlockSpec((None, None, tt, tt), lambda h, dlt: (h, dlt, 0, 0)),
        out_shape=jax.ShapeDtypeStruct((heads, nb, tt, tt), F32),
        compiler_params=_cparams("parallel", "parallel"),
        name="causal_bias_tiles",
    )(table)


def _dilated_bias_kernel(table_ref, o_ref):
    h = pl.program_id(0)
    span = DIL_BLOCK
    shape = o_ref.shape[1:]
    step = lax.broadcasted_iota(jnp.int32, shape, 0) + span - lax.broadcasted_iota(jnp.int32, shape, 1)
    valid = jnp.logical_and(step >= 0, step <= span)
    for p, (_, dil) in enumerate(DIL_PATTERNS):
        o_ref[p] = _bias_from_distance(step * dil, valid, table_ref, h)


def _dilated_bias_tiles(table):
    heads = table.shape[1]
    npat = len(DIL_PATTERNS)
    return pl.pallas_call(
        _dilated_bias_kernel,
        grid=(heads,),
        in_specs=[pl.BlockSpec(memory_space=pltpu.SMEM)],
        out_specs=pl.BlockSpec((npat, None, DIL_BLOCK, 2 * DIL_BLOCK), lambda h: (0, h, 0, 0)),
        out_shape=jax.ShapeDtypeStruct((npat, heads, DIL_BLOCK, 2 * DIL_BLOCK), F32),
        compiler_params=_cparams("parallel"),
        name="dilated_bias_tiles",
    )(table)


DIL_GROUP = 4


def _dilated_kernel(q_ref, k_ref, v_ref, bias_ref, o_ref, qf, kf, vf, o_scr, lse_scr):
    seq = q_ref.shape[0]
    blk = DIL_BLOCK
    qf[...] = q_ref[...].astype(F32)
    for src, dst in ((k_ref, kf), (v_ref, vf)):
        dst[0:blk, :] = jnp.zeros((blk, HEAD_DIM), F32)
        dst[blk:, :] = src[...].astype(F32)
    dn_qk = (((2,), (2,)), ((0,), (0,)))
    dn_pv = (((2,), (1,)), ((0,), (0,)))

    def rows(ref, start, dil):
        if dil == 1:
            return ref[pl.ds(start, blk), :]
        return ref[pl.ds(start, blk, stride=dil), :]

    for p, (_, dil) in enumerate(DIL_PATTERNS):
        nblk = seq // dil // blk
        blocks = [(jb, r) for jb in range(nblk) for r in range(dil)]
        bias = bias_ref[p]
        for g0 in range(0, len(blocks), DIL_GROUP):
            group = blocks[g0:g0 + DIL_GROUP]
            starts = [jb * blk * dil + r for jb, r in group]
            has_prev = [jb > 0 for jb, _ in group]
            q = jnp.stack([rows(qf, st, dil) for st in starts]).astype(BF16)
            kc = jnp.stack([rows(kf, blk + st, dil) for st in starts]).astype(BF16)
            vc = jnp.stack([rows(vf, blk + st, dil) for st in starts]).astype(BF16)
            s_cur = lax.dot_general(q, kc, dn_qk, preferred_element_type=F32) + bias[:, blk:]
            m = jnp.max(s_cur, axis=-1, keepdims=True)
            if any(has_prev):
                prev_starts = [max(blk + st - blk * dil, 0) for st in starts]
                kp = jnp.stack([rows(kf, st, dil) for st in prev_starts]).astype(BF16)
                vp = jnp.stack([rows(vf, st, dil) for st in prev_starts]).astype(BF16)
                s_prev = lax.dot_general(q, kp, dn_qk, preferred_element_type=F32) + bias[:, :blk]
                which = lax.broadcasted_iota(jnp.int32, s_prev.shape, 0)
                for gi, ok in enumerate(has_prev):
                    if not ok:
                        s_prev = jnp.where(which == gi, MASKED, s_prev)
                m = jnp.maximum(m, jnp.max(s_prev, axis=-1, keepdims=True))
                p_prev = jnp.exp(s_prev - m)
            p_cur = jnp.exp(s_cur - m)
            l = jnp.sum(p_cur, axis=-1, keepdims=True)
            acc = lax.dot_general(p_cur.astype(BF16), vc, dn_pv, preferred_element_type=F32)
            if any(has_prev):
                l = l + jnp.sum(p_prev, axis=-1, keepdims=True)
                acc = acc + lax.dot_general(p_prev.astype(BF16), vp, dn_pv, preferred_element_type=F32)
            out = acc / l
            lse = jnp.broadcast_to(m + jnp.log(l), out.shape)
            for gi, st in enumerate(starts):
                if dil == 1:
                    o_scr[p, pl.ds(st, blk), :] = out[gi]
                    lse_scr[p, pl.ds(st, blk), :] = lse[gi]
                else:
                    o_scr[p, pl.ds(st, blk, stride=dil), :] = out[gi]
                    lse_scr[p, pl.ds(st, blk, stride=dil), :] = lse[gi]

    lses = [lse_scr[p] for p in range(len(DIL_PATTERNS))]
    top = functools.reduce(jnp.maximum, lses)
    es = [jnp.exp(x - top) for x in lses]
    den = functools.reduce(lambda a, b: a + b, es)
    mixed = functools.reduce(lambda a, b: a + b, [(e / den) * o_scr[p] for p, e in enumerate(es)])
    o_ref[...] = mixed.astype(o_ref.dtype)


def _dilated_attention(qkv, bias_tiles, bsz, seq):
    t = qkv.shape[1]
    h16 = DIL_HEADS
    npat = len(DIL_PATTERNS)
    blk = (None, seq, HEAD_DIM)
    return pl.pallas_call(
        _dilated_kernel,
        grid=(bsz, DIL_HEADS),
        in_specs=[
            pl.BlockSpec(blk, lambda b, h: (h, b, 0)),
            pl.BlockSpec(blk, lambda b, h: (h16 + h, b, 0)),
            pl.BlockSpec(blk, lambda b, h: (2 * h16 + h, b, 0)),
            pl.BlockSpec((npat, None, DIL_BLOCK, 2 * DIL_BLOCK), lambda b, h: (0, h, 0, 0)),
        ],
        out_specs=pl.BlockSpec(blk, lambda b, h: (h, b, 0)),
        out_shape=jax.ShapeDtypeStruct((DIL_HEADS, t, HEAD_DIM), BF16),
        scratch_shapes=[
            pltpu.VMEM((seq, HEAD_DIM), F32),
            pltpu.VMEM((DIL_BLOCK + seq, HEAD_DIM), F32),
            pltpu.VMEM((DIL_BLOCK + seq, HEAD_DIM), F32),
            pltpu.VMEM((npat, seq, HEAD_DIM), F32),
            pltpu.VMEM((npat, seq, HEAD_DIM), F32),
        ],
        compiler_params=_cparams("parallel", "parallel"),
        name="dilated_attention",
    )(qkv, qkv, qkv, bias_tiles)


def _topk_rows(s, val_ref, pick_ref, payload=None):
    neg_rows = -lax.broadcasted_iota(jnp.int32, s.shape, 0).astype(F32)
    for k in range(PEER_TOPK):
        m = jnp.max(s, axis=0, keepdims=True)
        am = jnp.max(jnp.where(s == m, neg_rows, -jnp.inf), axis=0, keepdims=True)
        hit = neg_rows == am
        val_ref[k:k + 1, :] = m
        if payload is None:
            pick_ref[k:k + 1, :] = (-am).astype(jnp.int32)
        else:
            pick_ref[k:k + 1, :] = jnp.sum(jnp.where(hit, payload, 0), axis=0, keepdims=True)
        s = jnp.where(hit, -jnp.inf, s)


def _candidate_rows(a):
    need = PEER_TOPK // (a + 1)
    if need > SUBLANES:
        return PEER_TOPK
    return SUBLANES if need > 1 else 1


def _route_kernel(q_ref, keys_ref, idx_ref, gate_ref, v0, i0, v1, i1, best_ref):
    _route_head(q_ref[0], q_ref[1], keys_ref, idx_ref, gate_ref, v0, i0, v1, i1, best_ref)


def _route_head(q0, q1, keys_ref, idx_ref, gate_ref, v0, i0, v1, i1, best_ref):
    dn = (((1,), (1,)), ((), ()))
    s0 = lax.dot_general(keys_ref[0], q0, dn, preferred_element_type=F32)
    s1 = lax.dot_general(keys_ref[1], q1, dn, preferred_element_type=F32)
    _topk_rows(s0, v0, i0)
    _topk_rows(s1, v1, i1)
    first_single = min(a for a in range(PEER_TOPK) if _candidate_rows(a) == 1)
    cand_s, cand_i = [], []
    for a in range(first_single):
        nb = _candidate_rows(a)
        cand_s.append(v0[a:a + 1, :] + v1[0:nb, :])
        cand_i.append(i0[a:a + 1, :] * PEER_NKEYS + i1[0:nb, :])
    cand_s.append(v0[first_single:, :] + v1[0:1, :])
    cand_i.append(i0[first_single:, :] * PEER_NKEYS + i1[0:1, :])
    _topk_rows(jnp.concatenate(cand_s, axis=0), best_ref, idx_ref, jnp.concatenate(cand_i, axis=0))
    best_s = best_ref[...]
    e = jnp.exp(best_s - best_s[0:1, :])
    gate_ref[...] = e / jnp.sum(e, axis=0, keepdims=True)


def _route_scratch(tm):
    return [pltpu.VMEM((PEER_TOPK, tm), F32), pltpu.VMEM((PEER_TOPK, tm), jnp.int32),
            pltpu.VMEM((PEER_TOPK, tm), F32), pltpu.VMEM((PEER_TOPK, tm), jnp.int32),
            pltpu.VMEM((PEER_TOPK, tm), F32)]


def _peer_route(qp, sub_keys, t):
    tm = ROUTE_TM
    out_spec = pl.BlockSpec((None, PEER_TOPK, tm), lambda i, h: (h, 0, i))
    return pl.pallas_call(
        _route_kernel,
        grid=(t // tm, PEER_HEADS),
        in_specs=[
            pl.BlockSpec((2, tm, PEER_DKEY // 2), lambda i, h: (h, i, 0)),
            pl.BlockSpec((2, PEER_NKEYS, PEER_DKEY // 2), lambda i, h: (0, 0, 0)),
        ],
        out_specs=[out_spec, out_spec],
        out_shape=[
            jax.ShapeDtypeStruct((PEER_HEADS, PEER_TOPK, t), jnp.int32),
            jax.ShapeDtypeStruct((PEER_HEADS, PEER_TOPK, t), F32),
        ],
        scratch_shapes=_route_scratch(tm),
        compiler_params=_cparams("parallel", "parallel"),
        name="peer_route",
    )(qp, sub_keys)


def _gelu_exact(a):
    return 0.5 * a * (1.0 + lax.erf(a * (2.0 ** -0.5)))


def _pack_expert_tables(table_u, table_v):
    e, d = table_u.shape
    half = d // 2

    def pack(tbl):
        bits = lax.bitcast_convert_type(tbl.astype(BF16), jnp.uint16).astype(jnp.uint32)
        return (bits[:, half:] << 16) | bits[:, :half]

    return jnp.concatenate([pack(table_u), pack(table_v)], axis=1).reshape(e, 1, d)


def _unpack_words(w):
    lo = lax.bitcast_convert_type(w << 16, F32)
    hi = lax.bitcast_convert_type(w & jnp.uint32(0xFFFF0000), F32)
    return lo, hi


PEER_NBUF = 4
PEER_LOOKAHEAD = 2


def _expert_kernel(idx0_ref, gates0_ref, qnext_ref, keys_ref, x_ref, gain_ref, sc_ref, sh_ref, g2_ref,
                   tab_hbm, o_ref, *scratch):
    bufs, sems = scratch[:PEER_NBUF], scratch[PEER_NBUF]
    (idx_smem, gates_v, idx_t, gate_t, stage, route_sem), route_scratch = (
        scratch[PEER_NBUF + 1:PEER_NBUF + 7], scratch[PEER_NBUF + 7:])
    s = pl.program_id(0)
    n_steps = pl.num_programs(0)
    tp = x_ref.shape[0]
    gt = tp // PEER_NBUF
    d = x_ref.shape[1]
    ngroup = PEER_PICKS // SUBLANES
    nword = d // 2 // LANES
    spt = ROUTE_TM // tp
    r = s % spt
    par = (s // spt) % 2
    gain, sc, sh, g2 = gain_ref[...], sc_ref[...], sh_ref[...], g2_ref[...]
    eye = (lax.broadcasted_iota(jnp.int32, (PEER_PICKS, PEER_PICKS), 0)
           == lax.broadcasted_iota(jnp.int32, (PEER_PICKS, PEER_PICKS), 1))

    def park_routing(idx_rows, to_par):
        cp = pltpu.make_async_copy(idx_rows, idx_smem.at[to_par], route_sem.at[0])
        cp.start()
        cp.wait()

    @pl.when(s == 0)
    def _():
        park_routing(idx0_ref, 0)
        gates_v[0] = gates0_ref[...]

    @pl.when(r < PEER_HEADS)
    def _():
        rows = pl.ds(pl.multiple_of(r * PEER_TOPK, PEER_TOPK), PEER_TOPK)
        _route_head(qnext_ref[2 * r], qnext_ref[2 * r + 1], keys_ref, idx_t.at[rows], gate_t.at[rows],
                    *route_scratch)

    @pl.when(r == spt - 1)
    def _():
        stage[...] = idx_t[...].T
        gates_v[1 - par] = gate_t[...].T
        park_routing(stage, 1 - par)

    def start_token(tile_par, row, j, ti):
        for g in range(ngroup):
            for k in range(SUBLANES):
                e = idx_smem[tile_par, row, g * SUBLANES + k]
                pltpu.make_async_copy(
                    tab_hbm.at[e],
                    bufs[j].at[pl.ds((ti * ngroup + g) * SUBLANES + k, 1)],
                    sems.at[j]).start(priority=k % 2)

    def wait_group(j):
        pltpu.make_async_copy(bufs[j], bufs[j], sems.at[j]).wait()

    def gated_activation(buf, ti, tok):
        h = _modulated_norm(x_ref[pl.ds(tok, 1), :], gain, sc, sh)
        hb = jnp.broadcast_to(h, (SUBLANES, d))
        parts = []
        for g in range(ngroup):
            acc = None
            for c in range(nword):
                lo, hi = _unpack_words(buf[pl.ds((ti * ngroup + g) * SUBLANES, SUBLANES),
                                           c * LANES:(c + 1) * LANES])
                term = (lo * hb[:, c * LANES:(c + 1) * LANES]
                        + hi * hb[:, (nword + c) * LANES:(nword + c + 1) * LANES])
                acc = term if acc is None else acc + term
            parts.append(acc)
        act = jnp.sum(jnp.concatenate(parts, axis=0), axis=-1, keepdims=True)
        gate_row = gates_v[par, pl.ds(r * tp + tok, 1), :]
        gate_col = jnp.sum(jnp.where(eye, gate_row, 0.0), axis=-1, keepdims=True)
        return gate_col * _gelu_exact(act)

    def weighted_values(buf, ti, tok, a):
        ylo = [None] * nword
        yhi = [None] * nword
        for g in range(ngroup):
            ag = a[g * SUBLANES:(g + 1) * SUBLANES, :]
            for c in range(nword):
                lo, hi = _unpack_words(buf[pl.ds((ti * ngroup + g) * SUBLANES, SUBLANES),
                                           (nword + c) * LANES:(nword + c + 1) * LANES])
                ylo[c] = lo * ag if ylo[c] is None else ylo[c] + lo * ag
                yhi[c] = hi * ag if yhi[c] is None else yhi[c] + hi * ag
        y = jnp.sum(jnp.concatenate(ylo + yhi, axis=1), axis=0, keepdims=True)
        o_ref[pl.ds(tok, 1), :] = x_ref[pl.ds(tok, 1), :] + g2 * y

    @pl.when(s == 0)
    def _():
        for j in range(PEER_LOOKAHEAD):
            def body(ti, carry, j=j):
                start_token(0, j * gt + ti, j, ti)
                return carry
            lax.fori_loop(0, gt, body, 0)

    last_of_tile = r == spt - 1
    next_par = jnp.where(last_of_tile, 1 - par, par)
    next_row = jnp.where(last_of_tile, 0, (r + 1) * tp)

    pending = None
    for j in range(PEER_NBUF):
        wait_group(j)
        ahead = j + PEER_LOOKAHEAD
        for ti in range(gt):
            tok = j * gt + ti
            if ahead < PEER_NBUF:
                start_token(par, r * tp + ahead * gt + ti, ahead, ti)
            else:
                start_token(next_par, next_row + (ahead - PEER_NBUF) * gt + ti, ahead - PEER_NBUF, ti)
            a = gated_activation(bufs[j], ti, tok)
            if pending is not None:
                weighted_values(*pending)
            pending = (bufs[j], ti, tok, a)
    weighted_values(*pending)

    @pl.when(s == n_steps - 1)
    def _():
        for j in range(PEER_LOOKAHEAD):
            wait_group(j)


def _peer_experts(x2, gain, sc, sh, g2, idx0, gates0, qp, sub_keys, packed, seq):
    t, d = x2.shape
    tp = PEER_TP
    tm = ROUTE_TM
    assert tm % tp == 0 and tm // tp >= PEER_HEADS
    n_steps = t // tp
    n_tiles = t // tm
    spt = tm // tp
    per_b = seq // tp
    whole = lambda s: (0, 0)
    return pl.pallas_call(
        _expert_kernel,
        grid=(n_steps,),
        in_specs=[
            pl.BlockSpec((tm, PEER_PICKS), whole),
            pl.BlockSpec((tm, PEER_PICKS), whole),
            pl.BlockSpec((2 * PEER_HEADS, tm, PEER_DKEY // 2),
                         lambda s: (0, jnp.minimum(s // spt + 1, n_tiles - 1), 0)),
            pl.BlockSpec((2, PEER_NKEYS, PEER_DKEY // 2), lambda s: (0, 0, 0)),
            pl.BlockSpec((tp, d), lambda s: (s, 0)),
            pl.BlockSpec((1, d), lambda s: (0, 0)),
            pl.BlockSpec((None, 1, d), lambda s: (s // per_b, 0, 0)),
            pl.BlockSpec((None, 1, d), lambda s: (s // per_b, 0, 0)),
            pl.BlockSpec((None, 1, d), lambda s: (s // per_b, 0, 0)),
            pl.BlockSpec(memory_space=pl.ANY),
        ],
        out_specs=pl.BlockSpec((tp, d), lambda s: (s, 0)),
        out_shape=jax.ShapeDtypeStruct((t, d), F32),
        scratch_shapes=(
            [pltpu.VMEM((tp // PEER_NBUF * PEER_PICKS, d), jnp.uint32)] * PEER_NBUF
            + [pltpu.SemaphoreType.DMA((PEER_NBUF,)),
               pltpu.SMEM((2, tm, PEER_PICKS), jnp.int32),
               pltpu.VMEM((2, tm, PEER_PICKS), F32),
               pltpu.VMEM((PEER_PICKS, tm), jnp.int32),
               pltpu.VMEM((PEER_PICKS, tm), F32),
               pltpu.VMEM((tm, PEER_PICKS), jnp.int32),
               pltpu.SemaphoreType.DMA((1,))]
            + _route_scratch(tm)),
        compiler_params=_cparams("arbitrary"),
        name="peer_experts",
    )(idx0, gates0, qp, sub_keys, x2, gain.reshape(1, d), sc, sh, g2, packed)


def _peer_ffn(x2, gain, sc, sh, g2, w_query, sub_keys, table_u, table_v, seq):
    no_gain = jnp.ones((1, w_query.shape[1]), F32)
    qp = _norm_proj(x2, gain, sc, sh, w_query.astype(BF16), no_gain, (), seq)
    keys = sub_keys.astype(BF16)
    idx0, gates0 = _peer_route(qp, keys, ROUTE_TM)
    idx0 = idx0.transpose(2, 0, 1).reshape(ROUTE_TM, PEER_PICKS)
    gates0 = gates0.transpose(2, 0, 1).reshape(ROUTE_TM, PEER_PICKS)
    packed = _pack_expert_tables(table_u, table_v)
    return _peer_experts(x2, gain, sc, sh, g2, idx0, gates0, qp, keys, packed, seq)


def _tile_gain(gain, heads, scale=1.0):
    return jnp.tile(gain * scale, heads)


def _even_layer(x2, mods, norm_gain, w_in, b_forget, fox_qk_gain, diff_qk_gain, diff_lambda,
                diff_subln_gain, w_out, diff_bias, lam_init, bsz, seq):
    sh1, sc1, g1 = mods
    w_main = jnp.concatenate([w_in[:, :3 * FOX_W], w_in[:, 3 * FOX_W + FOX_HEADS:]], axis=1).astype(BF16)
    w_forget = jnp.zeros((D_MODEL, LANES), F32).at[:, :FOX_HEADS].set(
        w_in[:, 3 * FOX_W:3 * FOX_W + FOX_HEADS]).astype(BF16)
    ones = jnp.ones((FOX_W,), F32)
    head_gain = jnp.concatenate([
        _tile_gain(fox_qk_gain[0], FOX_HEADS, ATTN_SCALE), _tile_gain(fox_qk_gain[1], FOX_HEADS), ones,
        _tile_gain(diff_qk_gain[0], 2 * DIFF_HEADS, ATTN_SCALE), _tile_gain(diff_qk_gain[1], 2 * DIFF_HEADS),
        ones]).reshape(1, -1)
    proj, f_side = _norm_proj(x2, norm_gain, sc1, sh1, w_main, head_gain, (0, 1, 3, 4), seq,
                              w_side=w_forget)
    fcol, frow = _forget_cumsum(f_side, b_forget, bsz, seq)
    fox_o = _fox_attention(proj, frow, fcol, bsz, seq)
    bias_tiles = _causal_bias_tiles(diff_bias, seq, ATTN_T)
    diff_o = _diff_attention(proj, bias_tiles, diff_lambda, diff_subln_gain, lam_init, bsz, seq)
    return _out_proj([fox_o, diff_o], w_out.astype(BF16), x2, g1, seq)


def _odd_layer(x2, mods, norm_gain, w_qkv, qk_gain, w_out, bias_table, bsz, seq):
    sh1, sc1, g1 = mods
    ones = jnp.ones((D_MODEL,), F32)
    head_gain = jnp.concatenate([
        _tile_gain(qk_gain[0], DIL_HEADS, ATTN_SCALE), _tile_gain(qk_gain[1], DIL_HEADS), ones]).reshape(1, -1)
    qkv = _norm_proj(x2, norm_gain, sc1, sh1, w_qkv.astype(BF16), head_gain, (0, 1, 2, 3), seq)
    mixed = _dilated_attention(qkv, _dilated_bias_tiles(bias_table), bsz, seq)
    return _out_proj([mixed], w_out.astype(BF16), x2, g1, seq)


def kernel(x, c, rel_bias, norm_gain, w_ada, b_ada, even_w_in, even_b_forget, even_fox_qk_gain,
           even_diff_qk_gain, even_diff_lambda, even_diff_subln_gain, even_w_out, odd_w_qkv,
           odd_qk_gain, odd_w_out, peer_w_query, peer_sub_keys, peer_u, peer_v):
    bsz, seq, d = x.shape
    x2 = x.reshape(bsz * seq, d)
    mod = _adaln(c, w_ada, b_ada)
    for i in range(DEPTH):
        sh1, sc1, g1, sh2, sc2, g2 = [m.reshape(bsz, 1, d) for m in jnp.split(mod[i], 6, axis=-1)]
        j = i // 2
        if i % 2 == 0:
            lam_init = 0.8 - 0.6 * math.exp(-0.3 * i)
            x2 = _even_layer(x2, (sh1, sc1, g1), norm_gain[i, 0], even_w_in[j], even_b_forget[j],
                             even_fox_qk_gain[j], even_diff_qk_gain[j], even_diff_lambda[j],
                             even_diff_subln_gain[j], even_w_out[j], rel_bias[:, :DIFF_HEADS],
                             lam_init, bsz, seq)
        else:
            x2 = _odd_layer(x2, (sh1, sc1, g1), norm_gain[i, 0], odd_w_qkv[j], odd_qk_gain[j],
                            odd_w_out[j], rel_bias, bsz, seq)
        x2 = _peer_ffn(x2, norm_gain[i, 1], sc2, sh2, g2, peer_w_query[i], peer_sub_keys[i],
                       peer_u[i], peer_v[i], seq)
    return x2.reshape(bsz, seq, d)
```

```python
import functools
import math

import numpy as np
import jax
import jax.numpy as jnp
from jax import lax
from jax.experimental import pallas as pl
from jax.experimental.pallas import tpu as pltpu

F32 = jnp.float32
BF16 = jnp.bfloat16

D_MODEL = 2048
DEPTH = 2
HEAD_DIM = 128
FOX_HEADS = 8
DIFF_HEADS = 4
DIFF_V_DIM = 2 * HEAD_DIM
DIL_HEADS = D_MODEL // HEAD_DIM
DIL_PATTERNS = ((128, 1), (512, 4), (2048, 16))
DIL_BLOCK = 128
N_BUCKETS = 32
BUCKET_MAX_EXACT = 16
BUCKET_MAX_DIST = 2048
PEER_HEADS = 8
PEER_NKEYS = 128
PEER_EXPERTS = PEER_NKEYS * PEER_NKEYS
PEER_DKEY = 256
PEER_TOPK = 16
PEER_PICKS = PEER_HEADS * PEER_TOPK
NORM_EPS = 1e-6
FOX_W = FOX_HEADS * HEAD_DIM
DIFF_QK_W = DIFF_HEADS * 2 * HEAD_DIM
DIFF_V_W = DIFF_HEADS * DIFF_V_DIM
ATTN_SCALE = HEAD_DIM ** -0.5
MASKED = -1e30

LANES = 128
SUBLANES = 8
VMEM_LIMIT = 48 * 1024 * 1024

PROJ_TM = 512
PROJ_TN = 1024
ATTN_T = 512
ROUTE_TM = 128
PEER_TP = 16


def _cparams(*sem):
    return pltpu.CompilerParams(dimension_semantics=sem, vmem_limit_bytes=VMEM_LIMIT)


def _adaln_kernel(c_ref, w_ref, b_ref, o_ref):
    c = c_ref[...]
    cond = c * jax.nn.sigmoid(c)
    o_ref[...] = jnp.dot(cond.astype(BF16), w_ref[...].astype(BF16),
                         preferred_element_type=F32) + b_ref[...]


def _adaln(c, w_ada, b_ada):
    depth, d, n = w_ada.shape
    bsz = c.shape[0]
    tn = 1024
    return pl.pallas_call(
        _adaln_kernel,
        grid=(depth, n // tn),
        in_specs=[
            pl.BlockSpec((bsz, d), lambda i, j: (0, 0)),
            pl.BlockSpec((None, d, tn), lambda i, j: (i, 0, j)),
            pl.BlockSpec((None, 1, tn), lambda i, j: (i, 0, j)),
        ],
        out_specs=pl.BlockSpec((None, bsz, tn), lambda i, j: (i, 0, j)),
        out_shape=jax.ShapeDtypeStruct((depth, bsz, n), F32),
        compiler_params=_cparams("parallel", "parallel"),
        name="adaln",
    )(c, w_ada, b_ada.reshape(depth, 1, n))


def _modulated_norm(x, gain, sc, sh):
    ms = jnp.mean(x * x, axis=-1, keepdims=True)
    y = x * lax.rsqrt(ms + NORM_EPS) * gain
    return y * (1.0 + sc) + sh


def _norm_proj_kernel(*refs, norm_tiles, has_side):
    if has_side:
        x_ref, gain_ref, sc_ref, sh_ref, w_ref, hg_ref, ws_ref, o_ref, side_ref, h_ref = refs
    else:
        x_ref, gain_ref, sc_ref, sh_ref, w_ref, hg_ref, o_ref, h_ref = refs
    j = pl.program_id(1)

    @pl.when(j == 0)
    def _():
        h = _modulated_norm(x_ref[...], gain_ref[...], sc_ref[...], sh_ref[...])
        h_ref[...] = h.astype(BF16)
        if has_side:
            side_ref[...] = jnp.dot(h_ref[...], ws_ref[...], preferred_element_type=F32)

    acc = jnp.dot(h_ref[...], w_ref[...], preferred_element_type=F32)
    tn = acc.shape[1]

    def write_normed():
        for g in range(tn // HEAD_DIM):
            sl = slice(g * HEAD_DIM, (g + 1) * HEAD_DIM)
            blk = acc[:, sl]
            ms = jnp.mean(blk * blk, axis=-1, keepdims=True)
            o_ref[g] = (blk * lax.rsqrt(ms + NORM_EPS) * hg_ref[:, sl]).astype(o_ref.dtype)

    def write_raw():
        for g in range(tn // HEAD_DIM):
            o_ref[g] = acc[:, g * HEAD_DIM:(g + 1) * HEAD_DIM].astype(o_ref.dtype)

    if not norm_tiles:
        write_raw()
    else:
        is_norm = functools.reduce(jnp.logical_or, [j == t for t in norm_tiles])
        pl.when(is_norm)(write_normed)
        pl.when(jnp.logical_not(is_norm))(write_raw)


def _norm_proj(x2, gain, sc, sh, w, head_gain, norm_tiles, seq, w_side=None):
    t, d = x2.shape
    n = w.shape[1]
    tm, tn = PROJ_TM, PROJ_TN
    per_b = seq // tm
    has_side = w_side is not None
    gpt = tn // HEAD_DIM
    in_specs = [
        pl.BlockSpec((tm, d), lambda i, j: (i, 0)),
        pl.BlockSpec((1, d), lambda i, j: (0, 0)),
        pl.BlockSpec((None, 1, d), lambda i, j: (i // per_b, 0, 0)),
        pl.BlockSpec((None, 1, d), lambda i, j: (i // per_b, 0, 0)),
        pl.BlockSpec((d, tn), lambda i, j: (0, j)),
        pl.BlockSpec((1, tn), lambda i, j: (0, j)),
    ]
    args = [x2, gain.reshape(1, d), sc, sh, w, head_gain]
    out_specs = pl.BlockSpec((gpt, tm, HEAD_DIM), lambda i, j: (j, i, 0))
    out_shape = jax.ShapeDtypeStruct((n // HEAD_DIM, t, HEAD_DIM), BF16)
    if has_side:
        ns = w_side.shape[1]
        in_specs.append(pl.BlockSpec((d, ns), lambda i, j: (0, 0)))
        args.append(w_side)
        out_specs = [out_specs, pl.BlockSpec((tm, ns), lambda i, j: (i, 0))]
        out_shape = [out_shape, jax.ShapeDtypeStruct((t, ns), F32)]
    return pl.pallas_call(
        functools.partial(_norm_proj_kernel, norm_tiles=tuple(norm_tiles), has_side=has_side),
        grid=(t // tm, n // tn),
        in_specs=in_specs,
        out_specs=out_specs,
        out_shape=out_shape,
        scratch_shapes=[pltpu.VMEM((tm, d), BF16)],
        compiler_params=_cparams("parallel", "arbitrary"),
        name="norm_proj",
    )(*args)


def _out_proj_kernel(*refs):
    *a_refs, w_ref, x_ref, g_ref, o_ref = refs
    a = jnp.concatenate([a_ref[g] for a_ref in a_refs for g in range(a_ref.shape[0])], axis=1)
    y = jnp.dot(a, w_ref[...], preferred_element_type=F32)
    o_ref[...] = x_ref[...] + g_ref[...] * y


def _out_proj(heads_major, w, x2, g, seq):
    t = x2.shape[0]
    k, n = w.shape
    assert sum(a.shape[0] for a in heads_major) * HEAD_DIM == k
    tm, tn = PROJ_TM, PROJ_TN
    per_b = seq // tm
    return pl.pallas_call(
        _out_proj_kernel,
        grid=(t // tm, n // tn),
        in_specs=[pl.BlockSpec((a.shape[0], tm, HEAD_DIM), lambda i, j: (0, i, 0)) for a in heads_major] + [
            pl.BlockSpec((k, tn), lambda i, j: (0, j)),
            pl.BlockSpec((tm, tn), lambda i, j: (i, j)),
            pl.BlockSpec((None, 1, tn), lambda i, j: (i // per_b, 0, j)),
        ],
        out_specs=pl.BlockSpec((tm, tn), lambda i, j: (i, j)),
        out_shape=jax.ShapeDtypeStruct((t, n), F32),
        compiler_params=_cparams("parallel", "parallel"),
        name="out_proj",
    )(*heads_major, w, x2, g)


def _forget_kernel(f_ref, b_ref, col_ref, row_ref):
    z = f_ref[...] + b_ref[...]
    lf = jnp.minimum(z, 0.0) - jnp.log1p(jnp.exp(-jnp.abs(z)))
    s = lf.shape[0]
    pos = lax.broadcasted_iota(jnp.int32, lf.shape, 0)
    d = 1
    while d < s:
        lf = lf + jnp.where(pos >= d, pltpu.roll(lf, d, axis=0), 0.0)
        d *= 2
    col_ref[...] = lf
    row_ref[...] = lf.T[:FOX_HEADS, :]


def _forget_cumsum(f_side, b_forget, bsz, seq):
    bpad = jnp.zeros((1, LANES), F32).at[0, :FOX_HEADS].set(b_forget)
    col, row = pl.pallas_call(
        _forget_kernel,
        grid=(bsz,),
        in_specs=[
            pl.BlockSpec((None, seq, LANES), lambda b: (b, 0, 0)),
            pl.BlockSpec((1, LANES), lambda b: (0, 0)),
        ],
        out_specs=[
            pl.BlockSpec((None, seq, LANES), lambda b: (b, 0, 0)),
            pl.BlockSpec((None, FOX_HEADS, seq), lambda b: (b, 0, 0)),
        ],
        out_shape=[
            jax.ShapeDtypeStruct((bsz, seq, LANES), F32),
            jax.ShapeDtypeStruct((bsz, FOX_HEADS, seq), F32),
        ],
        compiler_params=_cparams("parallel"),
        name="forget_cumsum",
    )(f_side.reshape(bsz, seq, LANES), bpad)
    return col, row.reshape(bsz, FOX_HEADS, 1, seq)


def _online_softmax_step(s, v, m_ref, l_ref, acc_ref):
    m_prev = m_ref[...]
    m_new = jnp.maximum(m_prev, jnp.max(s, axis=-1, keepdims=True))
    alpha = jnp.exp(m_prev - m_new)
    p = jnp.exp(s - m_new)
    l_ref[...] = alpha * l_ref[...] + jnp.sum(p, axis=-1, keepdims=True)
    acc_ref[...] = alpha * acc_ref[...] + jnp.dot(p.astype(v.dtype), v, preferred_element_type=F32)
    m_ref[...] = m_new


def _fox_kernel(q_ref, k_ref, v_ref, frow_ref, fcol_ref, o_ref, m_ref, l_ref, acc_ref, fq_ref):
    h = pl.program_id(1)
    qi = pl.program_id(2)
    ki = pl.program_id(3)
    tq, tk = q_ref.shape[0], k_ref.shape[0]

    @pl.when(ki == 0)
    def _():
        m_ref[...] = jnp.full_like(m_ref, MASKED)
        l_ref[...] = jnp.zeros_like(l_ref)
        acc_ref[...] = jnp.zeros_like(acc_ref)
        lane = lax.broadcasted_iota(jnp.int32, fcol_ref.shape, 1)
        fq_ref[...] = jnp.sum(jnp.where(lane == h, fcol_ref[...], 0.0), axis=-1, keepdims=True)

    @pl.when(ki <= qi)
    def _():
        s = lax.dot_general(q_ref[...], k_ref[...], (((1,), (1,)), ((), ())),
                            preferred_element_type=F32)
        s = s + fq_ref[...] - frow_ref[...]
        qpos = qi * tq + lax.broadcasted_iota(jnp.int32, s.shape, 0)
        kpos = ki * tk + lax.broadcasted_iota(jnp.int32, s.shape, 1)
        s = jnp.where(qpos >= kpos, s, MASKED)
        _online_softmax_step(s, v_ref[...], m_ref, l_ref, acc_ref)

    @pl.when(ki == qi)
    def _():
        o_ref[...] = (acc_ref[...] / l_ref[...]).astype(o_ref.dtype)


def _fox_attention(proj, frow, fcol, bsz, seq):
    t = proj.shape[1]
    tt = ATTN_T
    nb = seq // tt
    kcol = FOX_W // HEAD_DIM
    blk = (None, tt, HEAD_DIM)
    return pl.pallas_call(
        _fox_kernel,
        grid=(bsz, FOX_HEADS, nb, nb),
        in_specs=[
            pl.BlockSpec(blk, lambda b, h, qi, ki: (h, b * nb + qi, 0)),
            pl.BlockSpec(blk, lambda b, h, qi, ki: (kcol + h, b * nb + jnp.minimum(ki, qi), 0)),
            pl.BlockSpec(blk, lambda b, h, qi, ki: (2 * kcol + h, b * nb + jnp.minimum(ki, qi), 0)),
            pl.BlockSpec((None, None, 1, tt), lambda b, h, qi, ki: (b, h, 0, jnp.minimum(ki, qi))),
            pl.BlockSpec((None, tt, LANES), lambda b, h, qi, ki: (b, qi, 0)),
        ],
        out_specs=pl.BlockSpec(blk, lambda b, h, qi, ki: (h, b * nb + qi, 0)),
        out_shape=jax.ShapeDtypeStruct((FOX_HEADS, t, HEAD_DIM), BF16),
        scratch_shapes=[
            pltpu.VMEM((tt, 1), F32), pltpu.VMEM((tt, 1), F32),
            pltpu.VMEM((tt, HEAD_DIM), F32), pltpu.VMEM((tt, 1), F32),
        ],
        compiler_params=_cparams("parallel", "parallel", "parallel", "arbitrary"),
        name="fox_attention",
    )(proj, proj, proj, frow, fcol)


def _diff_kernel(q_ref, k_ref, v_ref, bias_ref, lam_ref, sg_ref, o_ref,
                 m0, l0, a0, m1, l1, a1, *, lam_init):
    qi = pl.program_id(2)
    ki = pl.program_id(3)

    @pl.when(ki == 0)
    def _():
        for m_ref, l_ref, a_ref in ((m0, l0, a0), (m1, l1, a1)):
            m_ref[...] = jnp.full_like(m_ref, MASKED)
            l_ref[...] = jnp.zeros_like(l_ref)
            a_ref[...] = jnp.zeros_like(a_ref)

    @pl.when(ki <= qi)
    def _():
        bias = bias_ref[...]
        v = jnp.concatenate([v_ref[0], v_ref[1]], axis=1)
        for mi, (m_ref, l_ref, a_ref) in enumerate(((m0, l0, a0), (m1, l1, a1))):
            s = lax.dot_general(q_ref[mi], k_ref[mi], (((1,), (1,)), ((), ())),
                                preferred_element_type=F32) + bias
            _online_softmax_step(s, v, m_ref, l_ref, a_ref)

    @pl.when(ki == qi)
    def _():
        lf = lam_ref[...]
        lam = (jnp.exp(jnp.sum(lf[0:1] * lf[1:2], axis=-1, keepdims=True))
               - jnp.exp(jnp.sum(lf[2:3] * lf[3:4], axis=-1, keepdims=True)) + lam_init)
        o = a0[...] / l0[...] - lam * (a1[...] / l1[...])
        ms = jnp.mean(o * o, axis=-1, keepdims=True)
        o = (o * lax.rsqrt(ms + NORM_EPS) * sg_ref[...] * (1.0 - lam_init)).astype(o_ref.dtype)
        o_ref[0] = o[:, :HEAD_DIM]
        o_ref[1] = o[:, HEAD_DIM:]


def _diff_attention(proj, bias_tiles, lam_param, subln_gain, lam_init, bsz, seq):
    t = proj.shape[1]
    tt = ATTN_T
    nb = seq // tt
    qcol = 3 * FOX_W // DIFF_V_DIM
    kcol = qcol + DIFF_QK_W // DIFF_V_DIM
    vcol = kcol + DIFF_QK_W // DIFF_V_DIM
    blk = (2, tt, HEAD_DIM)
    return pl.pallas_call(
        functools.partial(_diff_kernel, lam_init=lam_init),
        grid=(bsz, DIFF_HEADS, nb, nb),
        in_specs=[
            pl.BlockSpec(blk, lambda b, h, qi, ki: (qcol + h, b * nb + qi, 0)),
            pl.BlockSpec(blk, lambda b, h, qi, ki: (kcol + h, b * nb + jnp.minimum(ki, qi), 0)),
            pl.BlockSpec(blk, lambda b, h, qi, ki: (vcol + h, b * nb + jnp.minimum(ki, qi), 0)),
            pl.BlockSpec((None, None, tt, tt), lambda b, h, qi, ki: (h, jnp.maximum(qi - ki, 0), 0, 0)),
            pl.BlockSpec((4, HEAD_DIM), lambda b, h, qi, ki: (0, 0)),
            pl.BlockSpec((1, DIFF_V_DIM), lambda b, h, qi, ki: (0, 0)),
        ],
        out_specs=pl.BlockSpec(blk, lambda b, h, qi, ki: (h, b * nb + qi, 0)),
        out_shape=jax.ShapeDtypeStruct((2 * DIFF_HEADS, t, HEAD_DIM), BF16),
        scratch_shapes=[
            pltpu.VMEM((tt, 1), F32), pltpu.VMEM((tt, 1), F32), pltpu.VMEM((tt, DIFF_V_DIM), F32),
            pltpu.VMEM((tt, 1), F32), pltpu.VMEM((tt, 1), F32), pltpu.VMEM((tt, DIFF_V_DIM), F32),
        ],
        compiler_params=_cparams("parallel", "parallel", "parallel", "arbitrary"),
        name="diff_attention",
    )(proj, proj, proj, bias_tiles, lam_param, subln_gain.reshape(1, DIFF_V_DIM))


def _t5_bucket_np(dist):
    n = np.maximum(dist, 0)
    nf = np.maximum(n, 1).astype(np.float32)
    large = BUCKET_MAX_EXACT + (np.log(nf / np.float32(BUCKET_MAX_EXACT))
                                / np.float32(math.log(BUCKET_MAX_DIST / BUCKET_MAX_EXACT))
                                * np.float32(N_BUCKETS - BUCKET_MAX_EXACT)).astype(np.int32)
    large = np.minimum(large, N_BUCKETS - 1)
    return np.where(n < BUCKET_MAX_EXACT, n, large).astype(np.int32)


def _bucket_thresholds():
    buckets = _t5_bucket_np(np.arange(BUCKET_MAX_DIST + 1))
    assert np.all(np.diff(buckets) >= 0)
    return [int(np.argmax(buckets >= b)) for b in range(N_BUCKETS)]


_BUCKET_THRESHOLDS = _bucket_thresholds()


def _bias_from_distance(dist, valid, table_ref, h):
    val = jnp.full(dist.shape, table_ref[0, h], F32)
    for b in range(1, N_BUCKETS):
        val = jnp.where(dist >= _BUCKET_THRESHOLDS[b], table_ref[b, h], val)
    return jnp.where(valid, val, MASKED)


def _causal_bias_kernel(table_ref, o_ref):
    h, delta = pl.program_id(0), pl.program_id(1)
    tt = o_ref.shape[0]
    dist = (delta * tt + lax.broadcasted_iota(jnp.int32, o_ref.shape, 0)
            - lax.broadcasted_iota(jnp.int32, o_ref.shape, 1))
    o_ref[...] = _bias_from_distance(dist, dist >= 0, table_ref, h)


def _causal_bias_tiles(table, seq, tt):
    nb = seq // tt
    heads = table.shape[1]
    return pl.pallas_call(
        _causal_bias_kernel,
        grid=(heads, nb),
        in_specs=[pl.BlockSpec(memory_space=pltpu.SMEM)],
        out_specs=pl.BlockSpec((None, None, tt, tt), lambda h, dlt: (h, dlt, 0, 0)),
        out_shape=jax.ShapeDtypeStruct((heads, nb, tt, tt), F32),
        compiler_params=_cparams("parallel", "parallel"),
        name="causal_bias_tiles",
    )(table)


def _dilated_bias_kernel(table_ref, o_ref):
    h = pl.program_id(0)
    span = DIL_BLOCK
    shape = o_ref.shape[1:]
    step = lax.broadcasted_iota(jnp.int32, shape, 0) + span - lax.broadcasted_iota(jnp.int32, shape, 1)
    valid = jnp.logical_and(step >= 0, step <= span)
    for p, (_, dil) in enumerate(DIL_PATTERNS):
        o_ref[p] = _bias_from_distance(step * dil, valid, table_ref, h)


def _dilated_bias_tiles(table):
    heads = table.shape[1]
    npat = len(DIL_PATTERNS)
    return pl.pallas_call(
        _dilated_bias_kernel,
        grid=(heads,),
        in_specs=[pl.BlockSpec(memory_space=pltpu.SMEM)],
        out_specs=pl.BlockSpec((npat, None, DIL_BLOCK, 2 * DIL_BLOCK), lambda h: (0, h, 0, 0)),
        out_shape=jax.ShapeDtypeStruct((npat, heads, DIL_BLOCK, 2 * DIL_BLOCK), F32),
        compiler_params=_cparams("parallel"),
        name="dilated_bias_tiles",
    )(table)


DIL_GROUP = 4


def _dilated_kernel(q_ref, k_ref, v_ref, bias_ref, o_ref, qf, kf, vf, o_scr, lse_scr):
    seq = q_ref.shape[0]
    blk = DIL_BLOCK
    qf[...] = q_ref[...].astype(F32)
    for src, dst in ((k_ref, kf), (v_ref, vf)):
        dst[0:blk, :] = jnp.zeros((blk, HEAD_DIM), F32)
        dst[blk:, :] = src[...].astype(F32)
    dn_qk = (((2,), (2,)), ((0,), (0,)))
    dn_pv = (((2,), (1,)), ((0,), (0,)))

    def rows(ref, start, dil):
        if dil == 1:
            return ref[pl.ds(start, blk), :]
        return ref[pl.ds(start, blk, stride=dil), :]

    for p, (_, dil) in enumerate(DIL_PATTERNS):
        nblk = seq // dil // blk
        blocks = [(jb, r) for jb in range(nblk) for r in range(dil)]
        bias = bias_ref[p]
        for g0 in range(0, len(blocks), DIL_GROUP):
            group = blocks[g0:g0 + DIL_GROUP]
            starts = [jb * blk * dil + r for jb, r in group]
            has_prev = [jb > 0 for jb, _ in group]
            q = jnp.stack([rows(qf, st, dil) for st in starts]).astype(BF16)
            kc = jnp.stack([rows(kf, blk + st, dil) for st in starts]).astype(BF16)
            vc = jnp.stack([rows(vf, blk + st, dil) for st in starts]).astype(BF16)
            s_cur = lax.dot_general(q, kc, dn_qk, preferred_element_type=F32) + bias[:, blk:]
            m = jnp.max(s_cur, axis=-1, keepdims=True)
            if any(has_prev):
                prev_starts = [max(blk + st - blk * dil, 0) for st in starts]
                kp = jnp.stack([rows(kf, st, dil) for st in prev_starts]).astype(BF16)
                vp = jnp.stack([rows(vf, st, dil) for st in prev_starts]).astype(BF16)
                s_prev = lax.dot_general(q, kp, dn_qk, preferred_element_type=F32) + bias[:, :blk]
                which = lax.broadcasted_iota(jnp.int32, s_prev.shape, 0)
                for gi, ok in enumerate(has_prev):
                    if not ok:
                        s_prev = jnp.where(which == gi, MASKED, s_prev)
                m = jnp.maximum(m, jnp.max(s_prev, axis=-1, keepdims=True))
                p_prev = jnp.exp(s_prev - m)
            p_cur = jnp.exp(s_cur - m)
            l = jnp.sum(p_cur, axis=-1, keepdims=True)
            acc = lax.dot_general(p_cur.astype(BF16), vc, dn_pv, preferred_element_type=F32)
            if any(has_prev):
                l = l + jnp.sum(p_prev, axis=-1, keepdims=True)
                acc = acc + lax.dot_general(p_prev.astype(BF16), vp, dn_pv, preferred_element_type=F32)
            out = acc / l
            lse = jnp.broadcast_to(m + jnp.log(l), out.shape)
            for gi, st in enumerate(starts):
                if dil == 1:
                    o_scr[p, pl.ds(st, blk), :] = out[gi]
                    lse_scr[p, pl.ds(st, blk), :] = lse[gi]
                else:
                    o_scr[p, pl.ds(st, blk, stride=dil), :] = out[gi]
                    lse_scr[p, pl.ds(st, blk, stride=dil), :] = lse[gi]

    lses = [lse_scr[p] for p in range(len(DIL_PATTERNS))]
    top = functools.reduce(jnp.maximum, lses)
    es = [jnp.exp(x - top) for x in lses]
    den = functools.reduce(lambda a, b: a + b, es)
    mixed = functools.reduce(lambda a, b: a + b, [(e / den) * o_scr[p] for p, e in enumerate(es)])
    o_ref[...] = mixed.astype(o_ref.dtype)


def _dilated_attention(qkv, bias_tiles, bsz, seq):
    t = qkv.shape[1]
    h16 = DIL_HEADS
    npat = len(DIL_PATTERNS)
    blk = (None, seq, HEAD_DIM)
    return pl.pallas_call(
        _dilated_kernel,
        grid=(bsz, DIL_HEADS),
        in_specs=[
            pl.BlockSpec(blk, lambda b, h: (h, b, 0)),
            pl.BlockSpec(blk, lambda b, h: (h16 + h, b, 0)),
            pl.BlockSpec(blk, lambda b, h: (2 * h16 + h, b, 0)),
            pl.BlockSpec((npat, None, DIL_BLOCK, 2 * DIL_BLOCK), lambda b, h: (0, h, 0, 0)),
        ],
        out_specs=pl.BlockSpec(blk, lambda b, h: (h, b, 0)),
        out_shape=jax.ShapeDtypeStruct((DIL_HEADS, t, HEAD_DIM), BF16),
        scratch_shapes=[
            pltpu.VMEM((seq, HEAD_DIM), F32),
            pltpu.VMEM((DIL_BLOCK + seq, HEAD_DIM), F32),
            pltpu.VMEM((DIL_BLOCK + seq, HEAD_DIM), F32),
            pltpu.VMEM((npat, seq, HEAD_DIM), F32),
            pltpu.VMEM((npat, seq, HEAD_DIM), F32),
        ],
        compiler_params=_cparams("parallel", "parallel"),
        name="dilated_attention",
    )(qkv, qkv, qkv, bias_tiles)


def _topk_rows(s, val_ref, pick_ref, payload=None):
    neg_rows = -lax.broadcasted_iota(jnp.int32, s.shape, 0).astype(F32)
    for k in range(PEER_TOPK):
        m = jnp.max(s, axis=0, keepdims=True)
        am = jnp.max(jnp.where(s == m, neg_rows, -jnp.inf), axis=0, keepdims=True)
        hit = neg_rows == am
        val_ref[k:k + 1, :] = m
        if payload is None:
            pick_ref[k:k + 1, :] = (-am).astype(jnp.int32)
        else:
            pick_ref[k:k + 1, :] = jnp.sum(jnp.where(hit, payload, 0), axis=0, keepdims=True)
        s = jnp.where(hit, -jnp.inf, s)


def _candidate_rows(a):
    need = PEER_TOPK // (a + 1)
    if need > SUBLANES:
        return PEER_TOPK
    return SUBLANES if need > 1 else 1


def _route_kernel(q_ref, keys_ref, idx_ref, gate_ref, v0, i0, v1, i1, best_ref):
    _route_half(q_ref[0], keys_ref[0], v0, i0)
    _route_half(q_ref[1], keys_ref[1], v1, i1)
    _route_combine(idx_ref, gate_ref, v0, i0, v1, i1, best_ref)


def _route_half(q, keys, val_ref, idx_ref):
    scores = lax.dot_general(keys, q, (((1,), (1,)), ((), ())), preferred_element_type=F32)
    _topk_rows(scores, val_ref, idx_ref)


def _route_combine(idx_ref, gate_ref, v0, i0, v1, i1, best_ref):
    first_single = min(a for a in range(PEER_TOPK) if _candidate_rows(a) == 1)
    cand_s, cand_i = [], []
    for a in range(first_single):
        nb = _candidate_rows(a)
        cand_s.append(v0[a:a + 1, :] + v1[0:nb, :])
        cand_i.append(i0[a:a + 1, :] * PEER_NKEYS + i1[0:nb, :])
    cand_s.append(v0[first_single:, :] + v1[0:1, :])
    cand_i.append(i0[first_single:, :] * PEER_NKEYS + i1[0:1, :])
    _topk_rows(jnp.concatenate(cand_s, axis=0), best_ref, idx_ref, jnp.concatenate(cand_i, axis=0))
    best_s = best_ref[...]
    e = jnp.exp(best_s - best_s[0:1, :])
    gate_ref[...] = e / jnp.sum(e, axis=0, keepdims=True)


def _route_scratch(tm):
    return [pltpu.VMEM((PEER_TOPK, tm), F32), pltpu.VMEM((PEER_TOPK, tm), jnp.int32),
            pltpu.VMEM((PEER_TOPK, tm), F32), pltpu.VMEM((PEER_TOPK, tm), jnp.int32),
            pltpu.VMEM((PEER_TOPK, tm), F32)]


def _peer_route(qp, sub_keys, t):
    tm = ROUTE_TM
    out_spec = pl.BlockSpec((None, PEER_TOPK, tm), lambda i, h: (h, 0, i))
    return pl.pallas_call(
        _route_kernel,
        grid=(t // tm, PEER_HEADS),
        in_specs=[
            pl.BlockSpec((2, tm, PEER_DKEY // 2), lambda i, h: (h, i, 0)),
            pl.BlockSpec((2, PEER_NKEYS, PEER_DKEY // 2), lambda i, h: (0, 0, 0)),
        ],
        out_specs=[out_spec, out_spec],
        out_shape=[
            jax.ShapeDtypeStruct((PEER_HEADS, PEER_TOPK, t), jnp.int32),
            jax.ShapeDtypeStruct((PEER_HEADS, PEER_TOPK, t), F32),
        ],
        scratch_shapes=_route_scratch(tm),
        compiler_params=_cparams("parallel", "parallel"),
        name="peer_route",
    )(qp, sub_keys)


def _gelu_exact(a):
    return 0.5 * a * (1.0 + lax.erf(a * (2.0 ** -0.5)))


def _pack_expert_tables(table_u, table_v):
    e, d = table_u.shape
    half = d // 2

    def pack(tbl):
        bits = lax.bitcast_convert_type(tbl.astype(BF16), jnp.uint16).astype(jnp.uint32)
        return (bits[:, half:] << 16) | bits[:, :half]

    return jnp.concatenate([pack(table_u), pack(table_v)], axis=1).reshape(e, 1, d)


def _unpack_words(w):
    lo = lax.bitcast_convert_type(w << 16, F32)
    hi = lax.bitcast_convert_type(w & jnp.uint32(0xFFFF0000), F32)
    return lo, hi


PEER_NBUF = 4
PEER_LOOKAHEAD = 2
ROUTE_AHEAD_BUFS = 3


def _expert_kernel(idx0_ref, gates0_ref, qnext_ref, keys_ref, x_ref, gain_ref, sc_ref, sh_ref, g2_ref,
                   tab_hbm, o_ref, *scratch):
    bufs, sems = scratch[:PEER_NBUF], scratch[PEER_NBUF]
    (idx_smem, gates_v, idx_t, gate_t, stage, route_sem), route_scratch = (
        scratch[PEER_NBUF + 1:PEER_NBUF + 7], scratch[PEER_NBUF + 7:])
    s = pl.program_id(0)
    n_steps = pl.num_programs(0)
    tp = x_ref.shape[0]
    gt = tp // PEER_NBUF
    d = x_ref.shape[1]
    ngroup = PEER_PICKS // SUBLANES
    nword = d // 2 // LANES
    spt = ROUTE_TM // tp
    r = s % spt
    par = (s // spt) % ROUTE_AHEAD_BUFS
    gain, sc, sh, g2 = gain_ref[...], sc_ref[...], sh_ref[...], g2_ref[...]
    eye = (lax.broadcasted_iota(jnp.int32, (PEER_PICKS, PEER_PICKS), 0)
           == lax.broadcasted_iota(jnp.int32, (PEER_PICKS, PEER_PICKS), 1))

    def park_routing(idx_rows, to_par):
        cp = pltpu.make_async_copy(idx_rows, idx_smem.at[to_par], route_sem.at[0])
        cp.start()
        cp.wait()

    @pl.when(s == 0)
    def _():
        for tile in range(ROUTE_AHEAD_BUFS - 1):
            park_routing(idx0_ref.at[pl.ds(tile * ROUTE_TM, ROUTE_TM)], tile)
            gates_v[tile] = gates0_ref[pl.ds(tile * ROUTE_TM, ROUTE_TM), :]

    head_rows = pl.ds(pl.multiple_of(r * PEER_TOPK, PEER_TOPK), PEER_TOPK)
    rv0, ri0, rv1, ri1, rbest = route_scratch
    route_stages = (
        lambda: _route_half(qnext_ref[2 * r], keys_ref[0], rv0, ri0),
        lambda: _route_half(qnext_ref[2 * r + 1], keys_ref[1], rv1, ri1),
        lambda: _route_combine(idx_t.at[head_rows], gate_t.at[head_rows], rv0, ri0, rv1, ri1, rbest),
    )

    def start_token(tile_par, row, j, ti):
        for g in range(ngroup):
            for k in range(SUBLANES):
                e = idx_smem[tile_par, row, g * SUBLANES + k]
                pltpu.make_async_copy(
                    tab_hbm.at[e],
                    bufs[j].at[pl.ds((ti * ngroup + g) * SUBLANES + k, 1)],
                    sems.at[j]).start(priority=k % 2)

    def wait_group(j):
        pltpu.make_async_copy(bufs[j], bufs[j], sems.at[j]).wait()

    def gated_activation(buf, ti, tok):
        h = _modulated_norm(x_ref[pl.ds(tok, 1), :], gain, sc, sh)
        hb = jnp.broadcast_to(h, (SUBLANES, d))
        parts = []
        for g in range(ngroup):
            acc = None
            for c in range(nword):
                lo, hi = _unpack_words(buf[pl.ds((ti * ngroup + g) * SUBLANES, SUBLANES),
                                           c * LANES:(c + 1) * LANES])
                term = (lo * hb[:, c * LANES:(c + 1) * LANES]
                        + hi * hb[:, (nword + c) * LANES:(nword + c + 1) * LANES])
                acc = term if acc is None else acc + term
            parts.append(acc)
        act = jnp.sum(jnp.concatenate(parts, axis=0), axis=-1, keepdims=True)
        gate_row = gates_v[par, pl.ds(r * tp + tok, 1), :]
        gate_col = jnp.sum(jnp.where(eye, gate_row, 0.0), axis=-1, keepdims=True)
        return gate_col * _gelu_exact(act)

    def weighted_values(buf, ti, tok, a):
        ylo = [None] * nword
        yhi = [None] * nword
        for g in range(ngroup):
            ag = a[g * SUBLANES:(g + 1) * SUBLANES, :]
            for c in range(nword):
                lo, hi = _unpack_words(buf[pl.ds((ti * ngroup + g) * SUBLANES, SUBLANES),
                                           (nword + c) * LANES:(nword + c + 1) * LANES])
                ylo[c] = lo * ag if ylo[c] is None else ylo[c] + lo * ag
                yhi[c] = hi * ag if yhi[c] is None else yhi[c] + hi * ag
        y = jnp.sum(jnp.concatenate(ylo + yhi, axis=1), axis=0, keepdims=True)
        o_ref[pl.ds(tok, 1), :] = x_ref[pl.ds(tok, 1), :] + g2 * y

    @pl.when(s == 0)
    def _():
        for j in range(PEER_LOOKAHEAD):
            def body(ti, carry, j=j):
                start_token(0, j * gt + ti, j, ti)
                return carry
            lax.fori_loop(0, gt, body, 0)

    last_of_tile = r == spt - 1
    next_par = jnp.where(last_of_tile, (par + 1) % ROUTE_AHEAD_BUFS, par)
    next_row = jnp.where(last_of_tile, 0, (r + 1) * tp)

    pending = None
    for j in range(PEER_NBUF):
        wait_group(j)
        if j < len(route_stages):
            route_stages[j]()
        ahead = j + PEER_LOOKAHEAD
        for ti in range(gt):
            tok = j * gt + ti
            if ahead < PEER_NBUF:
                start_token(par, r * tp + ahead * gt + ti, ahead, ti)
            else:
                start_token(next_par, next_row + (ahead - PEER_NBUF) * gt + ti, ahead - PEER_NBUF, ti)
            a = gated_activation(bufs[j], ti, tok)
            if pending is not None:
                weighted_values(*pending)
            pending = (bufs[j], ti, tok, a)
    weighted_values(*pending)

    @pl.when(last_of_tile)
    def _():
        free_par = (par + ROUTE_AHEAD_BUFS - 1) % ROUTE_AHEAD_BUFS
        stage[...] = idx_t[...].T
        gates_v[free_par] = gate_t[...].T
        park_routing(stage, free_par)

    @pl.when(s == n_steps - 1)
    def _():
        for j in range(PEER_LOOKAHEAD):
            wait_group(j)


def _peer_experts(x2, gain, sc, sh, g2, idx0, gates0, qp, sub_keys, packed, seq):
    t, d = x2.shape
    tp = PEER_TP
    tm = ROUTE_TM
    ahead = ROUTE_AHEAD_BUFS - 1
    assert tm == PEER_HEADS * tp
    n_steps = t // tp
    n_tiles = t // tm
    spt = tm // tp
    per_b = seq // tp
    whole = lambda s: (0, 0)
    return pl.pallas_call(
        _expert_kernel,
        grid=(n_steps,),
        in_specs=[
            pl.BlockSpec((ahead * tm, PEER_PICKS), whole),
            pl.BlockSpec((ahead * tm, PEER_PICKS), whole),
            pl.BlockSpec((2 * PEER_HEADS, tm, PEER_DKEY // 2),
                         lambda s: (0, jnp.minimum(s // spt + ahead, n_tiles - 1), 0)),
            pl.BlockSpec((2, PEER_NKEYS, PEER_DKEY // 2), lambda s: (0, 0, 0)),
            pl.BlockSpec((tp, d), lambda s: (s, 0)),
            pl.BlockSpec((1, d), lambda s: (0, 0)),
            pl.BlockSpec((None, 1, d), lambda s: (s // per_b, 0, 0)),
            pl.BlockSpec((None, 1, d), lambda s: (s // per_b, 0, 0)),
            pl.BlockSpec((None, 1, d), lambda s: (s // per_b, 0, 0)),
            pl.BlockSpec(memory_space=pl.ANY),
        ],
        out_specs=pl.BlockSpec((tp, d), lambda s: (s, 0)),
        out_shape=jax.ShapeDtypeStruct((t, d), F32),
        scratch_shapes=(
            [pltpu.VMEM((tp // PEER_NBUF * PEER_PICKS, d), jnp.uint32)] * PEER_NBUF
            + [pltpu.SemaphoreType.DMA((PEER_NBUF,)),
               pltpu.SMEM((ROUTE_AHEAD_BUFS, tm, PEER_PICKS), jnp.int32),
               pltpu.VMEM((ROUTE_AHEAD_BUFS, tm, PEER_PICKS), F32),
               pltpu.VMEM((PEER_PICKS, tm), jnp.int32),
               pltpu.VMEM((PEER_PICKS, tm), F32),
               pltpu.VMEM((tm, PEER_PICKS), jnp.int32),
               pltpu.SemaphoreType.DMA((1,))]
            + _route_scratch(tm)),
        compiler_params=_cparams("arbitrary"),
        name="peer_experts",
    )(idx0, gates0, qp, sub_keys, x2, gain.reshape(1, d), sc, sh, g2, packed)


def _peer_ffn(x2, gain, sc, sh, g2, w_query, sub_keys, table_u, table_v, seq):
    no_gain = jnp.ones((1, w_query.shape[1]), F32)
    qp = _norm_proj(x2, gain, sc, sh, w_query.astype(BF16), no_gain, (), seq)
    keys = sub_keys.astype(BF16)
    lead = (ROUTE_AHEAD_BUFS - 1) * ROUTE_TM
    idx0, gates0 = _peer_route(qp, keys, lead)
    idx0 = idx0.transpose(2, 0, 1).reshape(lead, PEER_PICKS)
    gates0 = gates0.transpose(2, 0, 1).reshape(lead, PEER_PICKS)
    packed = _pack_expert_tables(table_u, table_v)
    return _peer_experts(x2, gain, sc, sh, g2, idx0, gates0, qp, keys, packed, seq)


def _tile_gain(gain, heads, scale=1.0):
    return jnp.tile(gain * scale, heads)


def _even_layer(x2, mods, norm_gain, w_in, b_forget, fox_qk_gain, diff_qk_gain, diff_lambda,
                diff_subln_gain, w_out, diff_bias, lam_init, bsz, seq):
    sh1, sc1, g1 = mods
    w_main = jnp.concatenate([w_in[:, :3 * FOX_W], w_in[:, 3 * FOX_W + FOX_HEADS:]], axis=1).astype(BF16)
    w_forget = jnp.zeros((D_MODEL, LANES), F32).at[:, :FOX_HEADS].set(
        w_in[:, 3 * FOX_W:3 * FOX_W + FOX_HEADS]).astype(BF16)
    ones = jnp.ones((FOX_W,), F32)
    head_gain = jnp.concatenate([
        _tile_gain(fox_qk_gain[0], FOX_HEADS, ATTN_SCALE), _tile_gain(fox_qk_gain[1], FOX_HEADS), ones,
        _tile_gain(diff_qk_gain[0], 2 * DIFF_HEADS, ATTN_SCALE), _tile_gain(diff_qk_gain[1], 2 * DIFF_HEADS),
        ones]).reshape(1, -1)
    proj, f_side = _norm_proj(x2, norm_gain, sc1, sh1, w_main, head_gain, (0, 1, 3, 4), seq,
                              w_side=w_forget)
    fcol, frow = _forget_cumsum(f_side, b_forget, bsz, seq)
    fox_o = _fox_attention(proj, frow, fcol, bsz, seq)
    bias_tiles = _causal_bias_tiles(diff_bias, seq, ATTN_T)
    diff_o = _diff_attention(proj, bias_tiles, diff_lambda, diff_subln_gain, lam_init, bsz, seq)
    return _out_proj([fox_o, diff_o], w_out.astype(BF16), x2, g1, seq)


def _odd_layer(x2, mods, norm_gain, w_qkv, qk_gain, w_out, bias_table, bsz, seq):
    sh1, sc1, g1 = mods
    ones = jnp.ones((D_MODEL,), F32)
    head_gain = jnp.concatenate([
        _tile_gain(qk_gain[0], DIL_HEADS, ATTN_SCALE), _tile_gain(qk_gain[1], DIL_HEADS), ones]).reshape(1, -1)
    qkv = _norm_proj(x2, norm_gain, sc1, sh1, w_qkv.astype(BF16), head_gain, (0, 1, 2, 3), seq)
    mixed = _dilated_attention(qkv, _dilated_bias_tiles(bias_table), bsz, seq)
    return _out_proj([mixed], w_out.astype(BF16), x2, g1, seq)


def kernel(x, c, rel_bias, norm_gain, w_ada, b_ada, even_w_in, even_b_forget, even_fox_qk_gain,
           even_diff_qk_gain, even_diff_lambda, even_diff_subln_gain, even_w_out, odd_w_qkv,
           odd_qk_gain, odd_w_out, peer_w_query, peer_sub_keys, peer_u, peer_v):
    bsz, seq, d = x.shape
    x2 = x.reshape(bsz * seq, d)
    mod = _adaln(c, w_ada, b_ada)
    for i in range(DEPTH):
        sh1, sc1, g1, sh2, sc2, g2 = [m.reshape(bsz, 1, d) for m in jnp.split(mod[i], 6, axis=-1)]
        j = i // 2
        if i % 2 == 0:
            lam_init = 0.8 - 0.6 * math.exp(-0.3 * i)
            x2 = _even_layer(x2, (sh1, sc1, g1), norm_gain[i, 0], even_w_in[j], even_b_forget[j],
                             even_fox_qk_gain[j], even_diff_qk_gain[j], even_diff_lambda[j],
                             even_diff_subln_gain[j], even_w_out[j], rel_bias[:, :DIFF_HEADS],
                             lam_init, bsz, seq)
        else:
            x2 = _odd_layer(x2, (sh1, sc1, g1), norm_gain[i, 0], odd_w_qkv[j], odd_qk_gain[j],
                            odd_w_out[j], rel_bias, bsz, seq)
        x2 = _peer_ffn(x2, norm_gain[i, 1], sc2, sh2, g2, peer_w_query[i], peer_sub_keys[i],
                       peer_u[i], peer_v[i], seq)
    return x2.reshape(bsz, seq, d)
```

```python
import functools
import math

import numpy as np
import jax
import jax.numpy as jnp
from jax import lax
from jax.experimental import pallas as pl
from jax.experimental.pallas import tpu as pltpu

F32 = jnp.float32
BF16 = jnp.bfloat16

D_MODEL = 2048
DEPTH = 2
HEAD_DIM = 128
FOX_HEADS = 8
DIFF_HEADS = 4
DIFF_V_DIM = 2 * HEAD_DIM
DIL_HEADS = D_MODEL // HEAD_DIM
DIL_PATTERNS = ((128, 1), (512, 4), (2048, 16))
DIL_BLOCK = 128
N_BUCKETS = 32
BUCKET_MAX_EXACT = 16
BUCKET_MAX_DIST = 2048
PEER_HEADS = 8
PEER_NKEYS = 128
PEER_EXPERTS = PEER_NKEYS * PEER_NKEYS
PEER_DKEY = 256
PEER_TOPK = 16
PEER_PICKS = PEER_HEADS * PEER_TOPK
NORM_EPS = 1e-6
FOX_W = FOX_HEADS * HEAD_DIM
DIFF_QK_W = DIFF_HEADS * 2 * HEAD_DIM
DIFF_V_W = DIFF_HEADS * DIFF_V_DIM
ATTN_SCALE = HEAD_DIM ** -0.5
MASKED = -1e30

LANES = 128
SUBLANES = 8
VMEM_LIMIT = 48 * 1024 * 1024

PROJ_TM = 512
PROJ_TN = 1024
ATTN_T = 512
ROUTE_TM = 128
PEER_TP = 16


def _cparams(*sem):
    return pltpu.CompilerParams(dimension_semantics=sem, vmem_limit_bytes=VMEM_LIMIT)


def _adaln_kernel(c_ref, w_ref, b_ref, o_ref):
    c = c_ref[...]
    cond = c * jax.nn.sigmoid(c)
    o_ref[...] = jnp.dot(cond.astype(BF16), w_ref[...].astype(BF16),
                         preferred_element_type=F32) + b_ref[...]


def _adaln(c, w_ada, b_ada):
    depth, d, n = w_ada.shape
    bsz = c.shape[0]
    tn = 1024
    return pl.pallas_call(
        _adaln_kernel,
        grid=(depth, n // tn),
        in_specs=[
            pl.BlockSpec((bsz, d), lambda i, j: (0, 0)),
            pl.BlockSpec((None, d, tn), lambda i, j: (i, 0, j)),
            pl.BlockSpec((None, 1, tn), lambda i, j: (i, 0, j)),
        ],
        out_specs=pl.BlockSpec((None, bsz, tn), lambda i, j: (i, 0, j)),
        out_shape=jax.ShapeDtypeStruct((depth, bsz, n), F32),
        compiler_params=_cparams("parallel", "parallel"),
        name="adaln",
    )(c, w_ada, b_ada.reshape(depth, 1, n))


def _modulated_norm(x, gain, sc, sh):
    ms = jnp.mean(x * x, axis=-1, keepdims=True)
    y = x * lax.rsqrt(ms + NORM_EPS) * gain
    return y * (1.0 + sc) + sh


def _norm_proj_kernel(*refs, norm_tiles, has_side):
    if has_side:
        x_ref, gain_ref, sc_ref, sh_ref, w_ref, hg_ref, ws_ref, o_ref, side_ref, h_ref = refs
    else:
        x_ref, gain_ref, sc_ref, sh_ref, w_ref, hg_ref, o_ref, h_ref = refs
    j = pl.program_id(1)

    @pl.when(j == 0)
    def _():
        h = _modulated_norm(x_ref[...], gain_ref[...], sc_ref[...], sh_ref[...])
        h_ref[...] = h.astype(BF16)
        if has_side:
            side_ref[...] = jnp.dot(h_ref[...], ws_ref[...], preferred_element_type=F32)

    acc = jnp.dot(h_ref[...], w_ref[...], preferred_element_type=F32)
    tn = acc.shape[1]

    def write_normed():
        for g in range(tn // HEAD_DIM):
            sl = slice(g * HEAD_DIM, (g + 1) * HEAD_DIM)
            blk = acc[:, sl]
            ms = jnp.mean(blk * blk, axis=-1, keepdims=True)
            o_ref[g] = (blk * lax.rsqrt(ms + NORM_EPS) * hg_ref[:, sl]).astype(o_ref.dtype)

    def write_raw():
        for g in range(tn // HEAD_DIM):
            o_ref[g] = acc[:, g * HEAD_DIM:(g + 1) * HEAD_DIM].astype(o_ref.dtype)

    if not norm_tiles:
        write_raw()
    else:
        is_norm = functools.reduce(jnp.logical_or, [j == t for t in norm_tiles])
        pl.when(is_norm)(write_normed)
        pl.when(jnp.logical_not(is_norm))(write_raw)


def _norm_proj(x2, gain, sc, sh, w, head_gain, norm_tiles, seq, w_side=None):
    t, d = x2.shape
    n = w.shape[1]
    tm, tn = PROJ_TM, PROJ_TN
    per_b = seq // tm
    has_side = w_side is not None
    gpt = tn // HEAD_DIM
    in_specs = [
        pl.BlockSpec((tm, d), lambda i, j: (i, 0)),
        pl.BlockSpec((1, d), lambda i, j: (0, 0)),
        pl.BlockSpec((None, 1, d), lambda i, j: (i // per_b, 0, 0)),
        pl.BlockSpec((None, 1, d), lambda i, j: (i // per_b, 0, 0)),
        pl.BlockSpec((d, tn), lambda i, j: (0, j)),
        pl.BlockSpec((1, tn), lambda i, j: (0, j)),
    ]
    args = [x2, gain.reshape(1, d), sc, sh, w, head_gain]
    out_specs = pl.BlockSpec((gpt, tm, HEAD_DIM), lambda i, j: (j, i, 0))
    out_shape = jax.ShapeDtypeStruct((n // HEAD_DIM, t, HEAD_DIM), BF16)
    if has_side:
        ns = w_side.shape[1]
        in_specs.append(pl.BlockSpec((d, ns), lambda i, j: (0, 0)))
        args.append(w_side)
        out_specs = [out_specs, pl.BlockSpec((tm, ns), lambda i, j: (i, 0))]
        out_shape = [out_shape, jax.ShapeDtypeStruct((t, ns), F32)]
    return pl.pallas_call(
        functools.partial(_norm_proj_kernel, norm_tiles=tuple(norm_tiles), has_side=has_side),
        grid=(t // tm, n // tn),
        in_specs=in_specs,
        out_specs=out_specs,
        out_shape=out_shape,
        scratch_shapes=[pltpu.VMEM((tm, d), BF16)],
        compiler_params=_cparams("parallel", "arbitrary"),
        name="norm_proj",
    )(*args)


def _out_proj_kernel(*refs):
    *a_refs, w_ref, x_ref, g_ref, o_ref = refs
    a = jnp.concatenate([a_ref[g] for a_ref in a_refs for g in range(a_ref.shape[0])], axis=1)
    y = jnp.dot(a, w_ref[...], preferred_element_type=F32)
    o_ref[...] = x_ref[...] + g_ref[...] * y


def _out_proj(heads_major, w, x2, g, seq):
    t = x2.shape[0]
    k, n = w.shape
    assert sum(a.shape[0] for a in heads_major) * HEAD_DIM == k
    tm, tn = PROJ_TM, PROJ_TN
    per_b = seq // tm
    return pl.pallas_call(
        _out_proj_kernel,
        grid=(t // tm, n // tn),
        in_specs=[pl.BlockSpec((a.shape[0], tm, HEAD_DIM), lambda i, j: (0, i, 0)) for a in heads_major] + [
            pl.BlockSpec((k, tn), lambda i, j: (0, j)),
            pl.BlockSpec((tm, tn), lambda i, j: (i, j)),
            pl.BlockSpec((None, 1, tn), lambda i, j: (i // per_b, 0, j)),
        ],
        out_specs=pl.BlockSpec((tm, tn), lambda i, j: (i, j)),
        out_shape=jax.ShapeDtypeStruct((t, n), F32),
        compiler_params=_cparams("parallel", "parallel"),
        name="out_proj",
    )(*heads_major, w, x2, g)


def _forget_kernel(f_ref, b_ref, col_ref, row_ref):
    z = f_ref[...] + b_ref[...]
    lf = jnp.minimum(z, 0.0) - jnp.log1p(jnp.exp(-jnp.abs(z)))
    s = lf.shape[0]
    pos = lax.broadcasted_iota(jnp.int32, lf.shape, 0)
    d = 1
    while d < s:
        lf = lf + jnp.where(pos >= d, pltpu.roll(lf, d, axis=0), 0.0)
        d *= 2
    col_ref[...] = lf
    row_ref[...] = lf.T[:FOX_HEADS, :]


def _forget_cumsum(f_side, b_forget, bsz, seq):
    bpad = jnp.zeros((1, LANES), F32).at[0, :FOX_HEADS].set(b_forget)
    col, row = pl.pallas_call(
        _forget_kernel,
        grid=(bsz,),
        in_specs=[
            pl.BlockSpec((None, seq, LANES), lambda b: (b, 0, 0)),
            pl.BlockSpec((1, LANES), lambda b: (0, 0)),
        ],
        out_specs=[
            pl.BlockSpec((None, seq, LANES), lambda b: (b, 0, 0)),
            pl.BlockSpec((None, FOX_HEADS, seq), lambda b: (b, 0, 0)),
        ],
        out_shape=[
            jax.ShapeDtypeStruct((bsz, seq, LANES), F32),
            jax.ShapeDtypeStruct((bsz, FOX_HEADS, seq), F32),
        ],
        compiler_params=_cparams("parallel"),
        name="forget_cumsum",
    )(f_side.reshape(bsz, seq, LANES), bpad)
    return col, row.reshape(bsz, FOX_HEADS, 1, seq)


def _online_softmax_step(s, v, m_ref, l_ref, acc_ref):
    m_prev = m_ref[...]
    m_new = jnp.maximum(m_prev, jnp.max(s, axis=-1, keepdims=True))
    alpha = jnp.exp(m_prev - m_new)
    p = jnp.exp(s - m_new)
    l_ref[...] = alpha * l_ref[...] + jnp.sum(p, axis=-1, keepdims=True)
    acc_ref[...] = alpha * acc_ref[...] + jnp.dot(p.astype(v.dtype), v, preferred_element_type=F32)
    m_ref[...] = m_new


def _fox_kernel(q_ref, k_ref, v_ref, frow_ref, fcol_ref, o_ref, m_ref, l_ref, acc_ref, fq_ref):
    h = pl.program_id(1)
    qi = pl.program_id(2)
    ki = pl.program_id(3)
    tq, tk = q_ref.shape[0], k_ref.shape[0]

    @pl.when(ki == 0)
    def _():
        m_ref[...] = jnp.full_like(m_ref, MASKED)
        l_ref[...] = jnp.zeros_like(l_ref)
        acc_ref[...] = jnp.zeros_like(acc_ref)
        lane = lax.broadcasted_iota(jnp.int32, fcol_ref.shape, 1)
        fq_ref[...] = jnp.sum(jnp.where(lane == h, fcol_ref[...], 0.0), axis=-1, keepdims=True)

    @pl.when(ki <= qi)
    def _():
        s = lax.dot_general(q_ref[...], k_ref[...], (((1,), (1,)), ((), ())),
                            preferred_element_type=F32)
        s = s + fq_ref[...] - frow_ref[...]
        qpos = qi * tq + lax.broadcasted_iota(jnp.int32, s.shape, 0)
        kpos = ki * tk + lax.broadcasted_iota(jnp.int32, s.shape, 1)
        s = jnp.where(qpos >= kpos, s, MASKED)
        _online_softmax_step(s, v_ref[...], m_ref, l_ref, acc_ref)

    @pl.when(ki == qi)
    def _():
        o_ref[...] = (acc_ref[...] / l_ref[...]).astype(o_ref.dtype)


def _fox_attention(proj, frow, fcol, bsz, seq):
    t = proj.shape[1]
    tt = ATTN_T
    nb = seq // tt
    kcol = FOX_W // HEAD_DIM
    blk = (None, tt, HEAD_DIM)
    return pl.pallas_call(
        _fox_kernel,
        grid=(bsz, FOX_HEADS, nb, nb),
        in_specs=[
            pl.BlockSpec(blk, lambda b, h, qi, ki: (h, b * nb + qi, 0)),
            pl.BlockSpec(blk, lambda b, h, qi, ki: (kcol + h, b * nb + jnp.minimum(ki, qi), 0)),
            pl.BlockSpec(blk, lambda b, h, qi, ki: (2 * kcol + h, b * nb + jnp.minimum(ki, qi), 0)),
            pl.BlockSpec((None, None, 1, tt), lambda b, h, qi, ki: (b, h, 0, jnp.minimum(ki, qi))),
            pl.BlockSpec((None, tt, LANES), lambda b, h, qi, ki: (b, qi, 0)),
        ],
        out_specs=pl.BlockSpec(blk, lambda b, h, qi, ki: (h, b * nb + qi, 0)),
        out_shape=jax.ShapeDtypeStruct((FOX_HEADS, t, HEAD_DIM), BF16),
        scratch_shapes=[
            pltpu.VMEM((tt, 1), F32), pltpu.VMEM((tt, 1), F32),
            pltpu.VMEM((tt, HEAD_DIM), F32), pltpu.VMEM((tt, 1), F32),
        ],
        compiler_params=_cparams("parallel", "parallel", "parallel", "arbitrary"),
        name="fox_attention",
    )(proj, proj, proj, frow, fcol)


def _diff_kernel(q_ref, k_ref, v_ref, bias_ref, lam_ref, sg_ref, o_ref,
                 m0, l0, a0, m1, l1, a1, *, lam_init):
    qi = pl.program_id(2)
    ki = pl.program_id(3)

    @pl.when(ki == 0)
    def _():
        for m_ref, l_ref, a_ref in ((m0, l0, a0), (m1, l1, a1)):
            m_ref[...] = jnp.full_like(m_ref, MASKED)
            l_ref[...] = jnp.zeros_like(l_ref)
            a_ref[...] = jnp.zeros_like(a_ref)

    @pl.when(ki <= qi)
    def _():
        bias = bias_ref[...]
        v = jnp.concatenate([v_ref[0], v_ref[1]], axis=1)
        for mi, (m_ref, l_ref, a_ref) in enumerate(((m0, l0, a0), (m1, l1, a1))):
            s = lax.dot_general(q_ref[mi], k_ref[mi], (((1,), (1,)), ((), ())),
                                preferred_element_type=F32) + bias
            _online_softmax_step(s, v, m_ref, l_ref, a_ref)

    @pl.when(ki == qi)
    def _():
        lf = lam_ref[...]
        lam = (jnp.exp(jnp.sum(lf[0:1] * lf[1:2], axis=-1, keepdims=True))
               - jnp.exp(jnp.sum(lf[2:3] * lf[3:4], axis=-1, keepdims=True)) + lam_init)
        o = a0[...] / l0[...] - lam * (a1[...] / l1[...])
        ms = jnp.mean(o * o, axis=-1, keepdims=True)
        o = (o * lax.rsqrt(ms + NORM_EPS) * sg_ref[...] * (1.0 - lam_init)).astype(o_ref.dtype)
        o_ref[0] = o[:, :HEAD_DIM]
        o_ref[1] = o[:, HEAD_DIM:]


def _diff_attention(proj, bias_tiles, lam_param, subln_gain, lam_init, bsz, seq):
    t = proj.shape[1]
    tt = ATTN_T
    nb = seq // tt
    qcol = 3 * FOX_W // DIFF_V_DIM
    kcol = qcol + DIFF_QK_W // DIFF_V_DIM
    vcol = kcol + DIFF_QK_W // DIFF_V_DIM
    blk = (2, tt, HEAD_DIM)
    return pl.pallas_call(
        functools.partial(_diff_kernel, lam_init=lam_init),
        grid=(bsz, DIFF_HEADS, nb, nb),
        in_specs=[
            pl.BlockSpec(blk, lambda b, h, qi, ki: (qcol + h, b * nb + qi, 0)),
            pl.BlockSpec(blk, lambda b, h, qi, ki: (kcol + h, b * nb + jnp.minimum(ki, qi), 0)),
            pl.BlockSpec(blk, lambda b, h, qi, ki: (vcol + h, b * nb + jnp.minimum(ki, qi), 0)),
            pl.BlockSpec((None, None, tt, tt), lambda b, h, qi, ki: (h, jnp.maximum(qi - ki, 0), 0, 0)),
            pl.BlockSpec((4, HEAD_DIM), lambda b, h, qi, ki: (0, 0)),
            pl.BlockSpec((1, DIFF_V_DIM), lambda b, h, qi, ki: (0, 0)),
        ],
        out_specs=pl.BlockSpec(blk, lambda b, h, qi, ki: (h, b * nb + qi, 0)),
        out_shape=jax.ShapeDtypeStruct((2 * DIFF_HEADS, t, HEAD_DIM), BF16),
        scratch_shapes=[
            pltpu.VMEM((tt, 1), F32), pltpu.VMEM((tt, 1), F32), pltpu.VMEM((tt, DIFF_V_DIM), F32),
            pltpu.VMEM((tt, 1), F32), pltpu.VMEM((tt, 1), F32), pltpu.VMEM((tt, DIFF_V_DIM), F32),
        ],
        compiler_params=_cparams("parallel", "parallel", "parallel", "arbitrary"),
        name="diff_attention",
    )(proj, proj, proj, bias_tiles, lam_param, subln_gain.reshape(1, DIFF_V_DIM))


def _t5_bucket_np(dist):
    n = np.maximum(dist, 0)
    nf = np.maximum(n, 1).astype(np.float32)
    large = BUCKET_MAX_EXACT + (np.log(nf / np.float32(BUCKET_MAX_EXACT))
                                / np.float32(math.log(BUCKET_MAX_DIST / BUCKET_MAX_EXACT))
                                * np.float32(N_BUCKETS - BUCKET_MAX_EXACT)).astype(np.int32)
    large = np.minimum(large, N_BUCKETS - 1)
    return np.where(n < BUCKET_MAX_EXACT, n, large).astype(np.int32)


def _bucket_thresholds():
    buckets = _t5_bucket_np(np.arange(BUCKET_MAX_DIST + 1))
    assert np.all(np.diff(buckets) >= 0)
    return [int(np.argmax(buckets >= b)) for b in range(N_BUCKETS)]


_BUCKET_THRESHOLDS = _bucket_thresholds()


def _bias_from_distance(dist, valid, table_ref, h):
    val = jnp.full(dist.shape, table_ref[0, h], F32)
    for b in range(1, N_BUCKETS):
        val = jnp.where(dist >= _BUCKET_THRESHOLDS[b], table_ref[b, h], val)
    return jnp.where(valid, val, MASKED)


def _causal_bias_kernel(table_ref, o_ref):
    h, delta = pl.program_id(0), pl.program_id(1)
    tt = o_ref.shape[0]
    dist = (delta * tt + lax.broadcasted_iota(jnp.int32, o_ref.shape, 0)
            - lax.broadcasted_iota(jnp.int32, o_ref.shape, 1))
    o_ref[...] = _bias_from_distance(dist, dist >= 0, table_ref, h)


def _causal_bias_tiles(table, seq, tt):
    nb = seq // tt
    heads = table.shape[1]
    return pl.pallas_call(
        _causal_bias_kernel,
        grid=(heads, nb),
        in_specs=[pl.BlockSpec(memory_space=pltpu.SMEM)],
        out_specs=pl.BlockSpec((None, None, tt, tt), lambda h, dlt: (h, dlt, 0, 0)),
        out_shape=jax.ShapeDtypeStruct((heads, nb, tt, tt), F32),
        compiler_params=_cparams("parallel", "parallel"),
        name="causal_bias_tiles",
    )(table)


def _dilated_bias_kernel(table_ref, o_ref):
    h = pl.program_id(0)
    span = DIL_BLOCK
    shape = o_ref.shape[1:]
    step = lax.broadcasted_iota(jnp.int32, shape, 0) + span - lax.broadcasted_iota(jnp.int32, shape, 1)
    valid = jnp.logical_and(step >= 0, step <= span)
    for p, (_, dil) in enumerate(DIL_PATTERNS):
        o_ref[p] = _bias_from_distance(step * dil, valid, table_ref, h)


def _dilated_bias_tiles(table):
    heads = table.shape[1]
    npat = len(DIL_PATTERNS)
    return pl.pallas_call(
        _dilated_bias_kernel,
        grid=(heads,),
        in_specs=[pl.BlockSpec(memory_space=pltpu.SMEM)],
        out_specs=pl.BlockSpec((npat, None, DIL_BLOCK, 2 * DIL_BLOCK), lambda h: (0, h, 0, 0)),
        out_shape=jax.ShapeDtypeStruct((npat, heads, DIL_BLOCK, 2 * DIL_BLOCK), F32),
        compiler_params=_cparams("parallel"),
        name="dilated_bias_tiles",
    )(table)


DIL_GROUP = 4


def _dilated_kernel(q_ref, k_ref, v_ref, bias_ref, o_ref, qf, kf, vf, o_scr, lse_scr):
    seq = q_ref.shape[0]
    blk = DIL_BLOCK
    qf[...] = q_ref[...].astype(F32)
    for src, dst in ((k_ref, kf), (v_ref, vf)):
        dst[0:blk, :] = jnp.zeros((blk, HEAD_DIM), F32)
        dst[blk:, :] = src[...].astype(F32)
    dn_qk = (((2,), (2,)), ((0,), (0,)))
    dn_pv = (((2,), (1,)), ((0,), (0,)))

    def rows(ref, start, dil):
        if dil == 1:
            return ref[pl.ds(start, blk), :]
        return ref[pl.ds(start, blk, stride=dil), :]

    for p, (_, dil) in enumerate(DIL_PATTERNS):
        nblk = seq // dil // blk
        blocks = [(jb, r) for jb in range(nblk) for r in range(dil)]
        bias = bias_ref[p]
        for g0 in range(0, len(blocks), DIL_GROUP):
            group = blocks[g0:g0 + DIL_GROUP]
            starts = [jb * blk * dil + r for jb, r in group]
            has_prev = [jb > 0 for jb, _ in group]
            q = jnp.stack([rows(qf, st, dil) for st in starts]).astype(BF16)
            kc = jnp.stack([rows(kf, blk + st, dil) for st in starts]).astype(BF16)
            vc = jnp.stack([rows(vf, blk + st, dil) for st in starts]).astype(BF16)
            s_cur = lax.dot_general(q, kc, dn_qk, preferred_element_type=F32) + bias[:, blk:]
            m = jnp.max(s_cur, axis=-1, keepdims=True)
            if any(has_prev):
                prev_starts = [max(blk + st - blk * dil, 0) for st in starts]
                kp = jnp.stack([rows(kf, st, dil) for st in prev_starts]).astype(BF16)
                vp = jnp.stack([rows(vf, st, dil) for st in prev_starts]).astype(BF16)
                s_prev = lax.dot_general(q, kp, dn_qk, preferred_element_type=F32) + bias[:, :blk]
                which = lax.broadcasted_iota(jnp.int32, s_prev.shape, 0)
                for gi, ok in enumerate(has_prev):
                    if not ok:
                        s_prev = jnp.where(which == gi, MASKED, s_prev)
                m = jnp.maximum(m, jnp.max(s_prev, axis=-1, keepdims=True))
                p_prev = jnp.exp(s_prev - m)
            p_cur = jnp.exp(s_cur - m)
            l = jnp.sum(p_cur, axis=-1, keepdims=True)
            acc = lax.dot_general(p_cur.astype(BF16), vc, dn_pv, preferred_element_type=F32)
            if any(has_prev):
                l = l + jnp.sum(p_prev, axis=-1, keepdims=True)
                acc = acc + lax.dot_general(p_prev.astype(BF16), vp, dn_pv, preferred_element_type=F32)
            out = acc / l
            lse = jnp.broadcast_to(m + jnp.log(l), out.shape)
            for gi, st in enumerate(starts):
                if dil == 1:
                    o_scr[p, pl.ds(st, blk), :] = out[gi]
                    lse_scr[p, pl.ds(st, blk), :] = lse[gi]
                else:
                    o_scr[p, pl.ds(st, blk, stride=dil), :] = out[gi]
                    lse_scr[p, pl.ds(st, blk, stride=dil), :] = lse[gi]

    lses = [lse_scr[p] for p in range(len(DIL_PATTERNS))]
    top = functools.reduce(jnp.maximum, lses)
    es = [jnp.exp(x - top) for x in lses]
    den = functools.reduce(lambda a, b: a + b, es)
    mixed = functools.reduce(lambda a, b: a + b, [(e / den) * o_scr[p] for p, e in enumerate(es)])
    o_ref[...] = mixed.astype(o_ref.dtype)


def _dilated_attention(qkv, bias_tiles, bsz, seq):
    t = qkv.shape[1]
    h16 = DIL_HEADS
    npat = len(DIL_PATTERNS)
    blk = (None, seq, HEAD_DIM)
    return pl.pallas_call(
        _dilated_kernel,
        grid=(bsz, DIL_HEADS),
        in_specs=[
            pl.BlockSpec(blk, lambda b, h: (h, b, 0)),
            pl.BlockSpec(blk, lambda b, h: (h16 + h, b, 0)),
            pl.BlockSpec(blk, lambda b, h: (2 * h16 + h, b, 0)),
            pl.BlockSpec((npat, None, DIL_BLOCK, 2 * DIL_BLOCK), lambda b, h: (0, h, 0, 0)),
        ],
        out_specs=pl.BlockSpec(blk, lambda b, h: (h, b, 0)),
        out_shape=jax.ShapeDtypeStruct((DIL_HEADS, t, HEAD_DIM), BF16),
        scratch_shapes=[
            pltpu.VMEM((seq, HEAD_DIM), F32),
            pltpu.VMEM((DIL_BLOCK + seq, HEAD_DIM), F32),
            pltpu.VMEM((DIL_BLOCK + seq, HEAD_DIM), F32),
            pltpu.VMEM((npat, seq, HEAD_DIM), F32),
            pltpu.VMEM((npat, seq, HEAD_DIM), F32),
        ],
        compiler_params=_cparams("parallel", "parallel"),
        name="dilated_attention",
    )(qkv, qkv, qkv, bias_tiles)


def _topk_rows(s, val_ref, pick_ref, payload=None):
    neg_rows = -lax.broadcasted_iota(jnp.int32, s.shape, 0).astype(F32)
    for k in range(PEER_TOPK):
        m = jnp.max(s, axis=0, keepdims=True)
        am = jnp.max(jnp.where(s == m, neg_rows, -jnp.inf), axis=0, keepdims=True)
        hit = neg_rows == am
        val_ref[k:k + 1, :] = m
        if payload is None:
            pick_ref[k:k + 1, :] = (-am).astype(jnp.int32)
        else:
            pick_ref[k:k + 1, :] = jnp.sum(jnp.where(hit, payload, 0), axis=0, keepdims=True)
        s = jnp.where(hit, -jnp.inf, s)


def _candidate_rows(a):
    need = PEER_TOPK // (a + 1)
    if need > SUBLANES:
        return PEER_TOPK
    return SUBLANES if need > 1 else 1


def _route_kernel(q_ref, keys_ref, idx_ref, gate_ref, v0, i0, v1, i1, best_ref):
    _route_half(q_ref[0], keys_ref[0], v0, i0)
    _route_half(q_ref[1], keys_ref[1], v1, i1)
    _route_combine(idx_ref, gate_ref, v0, i0, v1, i1, best_ref)


def _route_half(q, keys, val_ref, idx_ref):
    scores = lax.dot_general(keys, q, (((1,), (1,)), ((), ())), preferred_element_type=F32)
    _topk_rows(scores, val_ref, idx_ref)


def _route_combine(idx_ref, gate_ref, v0, i0, v1, i1, best_ref):
    first_single = min(a for a in range(PEER_TOPK) if _candidate_rows(a) == 1)
    cand_s, cand_i = [], []
    for a in range(first_single):
        nb = _candidate_rows(a)
        cand_s.append(v0[a:a + 1, :] + v1[0:nb, :])
        cand_i.append(i0[a:a + 1, :] * PEER_NKEYS + i1[0:nb, :])
    cand_s.append(v0[first_single:, :] + v1[0:1, :])
    cand_i.append(i0[first_single:, :] * PEER_NKEYS + i1[0:1, :])
    _topk_rows(jnp.concatenate(cand_s, axis=0), best_ref, idx_ref, jnp.concatenate(cand_i, axis=0))
    best_s = best_ref[...]
    e = jnp.exp(best_s - best_s[0:1, :])
    gate_ref[...] = e / jnp.sum(e, axis=0, keepdims=True)


def _route_scratch(tm):
    return [pltpu.VMEM((PEER_TOPK, tm), F32), pltpu.VMEM((PEER_TOPK, tm), jnp.int32),
            pltpu.VMEM((PEER_TOPK, tm), F32), pltpu.VMEM((PEER_TOPK, tm), jnp.int32),
            pltpu.VMEM((PEER_TOPK, tm), F32)]


def _peer_route(qp, sub_keys, t):
    tm = ROUTE_TM
    out_spec = pl.BlockSpec((None, PEER_TOPK, tm), lambda i, h: (h, 0, i))
    return pl.pallas_call(
        _route_kernel,
        grid=(t // tm, PEER_HEADS),
        in_specs=[
            pl.BlockSpec((2, tm, PEER_DKEY // 2), lambda i, h: (h, i, 0)),
            pl.BlockSpec((2, PEER_NKEYS, PEER_DKEY // 2), lambda i, h: (0, 0, 0)),
        ],
        out_specs=[out_spec, out_spec],
        out_shape=[
            jax.ShapeDtypeStruct((PEER_HEADS, PEER_TOPK, t), jnp.int32),
            jax.ShapeDtypeStruct((PEER_HEADS, PEER_TOPK, t), F32),
        ],
        scratch_shapes=_route_scratch(tm),
        compiler_params=_cparams("parallel", "parallel"),
        name="peer_route",
    )(qp, sub_keys)


def _gelu_exact(a):
    return 0.5 * a * (1.0 + lax.erf(a * (2.0 ** -0.5)))


def _pack_expert_tables(table_u, table_v):
    e, d = table_u.shape
    te = PACK_ROWS
    n_steps = e // te
    return pl.pallas_call(
        _pack_kernel,
        grid=(n_steps,),
        in_specs=[pl.BlockSpec((te, d), lambda i: (i, 0)), pl.BlockSpec((te, d), lambda i: (i, 0))],
        out_specs=pl.BlockSpec(memory_space=pl.ANY),
        out_shape=jax.ShapeDtypeStruct((e, 1, d), jnp.uint32),
        scratch_shapes=[pltpu.VMEM((2, te, d), jnp.uint32), pltpu.SemaphoreType.DMA((2,))],
        compiler_params=_cparams("arbitrary"),
        name="pack_expert_tables",
    )(table_u, table_v)


PACK_ROWS = 128


def _pack_kernel(u_ref, v_ref, out_hbm, stage, sems):
    i = pl.program_id(0)
    n_steps = pl.num_programs(0)
    te, d = u_ref.shape
    half = d // 2
    slot = i % 2

    def bf16_high_bits(x):
        return lax.bitcast_convert_type(x.astype(BF16).astype(F32), jnp.uint32)

    def pack(ref):
        return bf16_high_bits(ref[:, half:]) | (bf16_high_bits(ref[:, :half]) >> 16)

    def wait_slot(sl):
        pltpu.make_async_copy(stage.at[sl], stage.at[sl], sems.at[sl]).wait()

    @pl.when(i >= 2)
    def _():
        wait_slot(slot)

    stage[slot] = jnp.concatenate([pack(u_ref), pack(v_ref)], axis=1)
    for r in range(te):
        pltpu.make_async_copy(stage.at[slot, pl.ds(r, 1)], out_hbm.at[i * te + r],
                              sems.at[slot]).start(priority=r % 2)

    @pl.when(i == n_steps - 1)
    def _():
        wait_slot(slot)

        @pl.when(n_steps > 1)
        def _():
            wait_slot(1 - slot)


def _unpack_words(w):
    lo = lax.bitcast_convert_type(w << 16, F32)
    hi = lax.bitcast_convert_type(w & jnp.uint32(0xFFFF0000), F32)
    return lo, hi


PEER_NBUF = 4
PEER_LOOKAHEAD = 2
ROUTE_AHEAD_BUFS = 3


def _expert_kernel(idx0_ref, gates0_ref, qnext_ref, keys_ref, x_ref, gain_ref, sc_ref, sh_ref, g2_ref,
                   tab_hbm, o_ref, *scratch):
    bufs, sems = scratch[:PEER_NBUF], scratch[PEER_NBUF]
    (idx_smem, gates_v, idx_t, gate_t, stage, route_sem), route_scratch = (
        scratch[PEER_NBUF + 1:PEER_NBUF + 7], scratch[PEER_NBUF + 7:])
    s = pl.program_id(0)
    n_steps = pl.num_programs(0)
    tp = x_ref.shape[0]
    gt = tp // PEER_NBUF
    d = x_ref.shape[1]
    ngroup = PEER_PICKS // SUBLANES
    nword = d // 2 // LANES
    spt = ROUTE_TM // tp
    r = s % spt
    par = (s // spt) % ROUTE_AHEAD_BUFS
    gain, sc, sh, g2 = gain_ref[...], sc_ref[...], sh_ref[...], g2_ref[...]
    eye = (lax.broadcasted_iota(jnp.int32, (PEER_PICKS, PEER_PICKS), 0)
           == lax.broadcasted_iota(jnp.int32, (PEER_PICKS, PEER_PICKS), 1))

    def park_routing(idx_rows, to_par):
        cp = pltpu.make_async_copy(idx_rows, idx_smem.at[to_par], route_sem.at[0])
        cp.start()
        cp.wait()

    @pl.when(s == 0)
    def _():
        for tile in range(ROUTE_AHEAD_BUFS - 1):
            park_routing(idx0_ref.at[pl.ds(tile * ROUTE_TM, ROUTE_TM)], tile)
            gates_v[tile] = gates0_ref[pl.ds(tile * ROUTE_TM, ROUTE_TM), :]

    head_rows = pl.ds(pl.multiple_of(r * PEER_TOPK, PEER_TOPK), PEER_TOPK)
    rv0, ri0, rv1, ri1, rbest = route_scratch
    route_stages = (
        lambda: _route_half(qnext_ref[2 * r], keys_ref[0], rv0, ri0),
        lambda: _route_half(qnext_ref[2 * r + 1], keys_ref[1], rv1, ri1),
        lambda: _route_combine(idx_t.at[head_rows], gate_t.at[head_rows], rv0, ri0, rv1, ri1, rbest),
    )

    def start_token(tile_par, row, j, ti):
        for g in range(ngroup):
            for k in range(SUBLANES):
                e = idx_smem[tile_par, row, g * SUBLANES + k]
                pltpu.make_async_copy(
                    tab_hbm.at[e],
                    bufs[j].at[pl.ds((ti * ngroup + g) * SUBLANES + k, 1)],
                    sems.at[j]).start(priority=k % 2)

    def wait_group(j):
        pltpu.make_async_copy(bufs[j], bufs[j], sems.at[j]).wait()

    def gated_activation(buf, ti, tok):
        h = _modulated_norm(x_ref[pl.ds(tok, 1), :], gain, sc, sh)
        hb = jnp.broadcast_to(h, (SUBLANES, d))
        parts = []
        for g in range(ngroup):
            acc = None
            for c in range(nword):
                lo, hi = _unpack_words(buf[pl.ds((ti * ngroup + g) * SUBLANES, SUBLANES),
                                           c * LANES:(c + 1) * LANES])
                term = (lo * hb[:, c * LANES:(c + 1) * LANES]
                        + hi * hb[:, (nword + c) * LANES:(nword + c + 1) * LANES])
                acc = term if acc is None else acc + term
            parts.append(acc)
        act = jnp.sum(jnp.concatenate(parts, axis=0), axis=-1, keepdims=True)
        gate_row = gates_v[par, pl.ds(r * tp + tok, 1), :]
        gate_col = jnp.sum(jnp.where(eye, gate_row, 0.0), axis=-1, keepdims=True)
        return gate_col * _gelu_exact(act)

    def weighted_values(buf, ti, tok, a):
        ylo = [None] * nword
        yhi = [None] * nword
        for g in range(ngroup):
            ag = a[g * SUBLANES:(g + 1) * SUBLANES, :]
            for c in range(nword):
                lo, hi = _unpack_words(buf[pl.ds((ti * ngroup + g) * SUBLANES, SUBLANES),
                                           (nword + c) * LANES:(nword + c + 1) * LANES])
                ylo[c] = lo * ag if ylo[c] is None else ylo[c] + lo * ag
                yhi[c] = hi * ag if yhi[c] is None else yhi[c] + hi * ag
        y = jnp.sum(jnp.concatenate(ylo + yhi, axis=1), axis=0, keepdims=True)
        o_ref[pl.ds(tok, 1), :] = x_ref[pl.ds(tok, 1), :] + g2 * y

    @pl.when(s == 0)
    def _():
        for j in range(PEER_LOOKAHEAD):
            def body(ti, carry, j=j):
                start_token(0, j * gt + ti, j, ti)
                return carry
            lax.fori_loop(0, gt, body, 0)

    last_of_tile = r == spt - 1
    next_par = jnp.where(last_of_tile, (par + 1) % ROUTE_AHEAD_BUFS, par)
    next_row = jnp.where(last_of_tile, 0, (r + 1) * tp)

    pending = None
    for j in range(PEER_NBUF):
        wait_group(j)
        if j < len(route_stages):
            route_stages[j]()
        ahead = j + PEER_LOOKAHEAD
        for ti in range(gt):
            tok = j * gt + ti
            if ahead < PEER_NBUF:
                start_token(par, r * tp + ahead * gt + ti, ahead, ti)
            else:
                start_token(next_par, next_row + (ahead - PEER_NBUF) * gt + ti, ahead - PEER_NBUF, ti)
            a = gated_activation(bufs[j], ti, tok)
            if pending is not None:
                weighted_values(*pending)
            pending = (bufs[j], ti, tok, a)
    weighted_values(*pending)

    @pl.when(last_of_tile)
    def _():
        free_par = (par + ROUTE_AHEAD_BUFS - 1) % ROUTE_AHEAD_BUFS
        stage[...] = idx_t[...].T
        gates_v[free_par] = gate_t[...].T
        park_routing(stage, free_par)

    @pl.when(s == n_steps - 1)
    def _():
        for j in range(PEER_LOOKAHEAD):
            wait_group(j)


def _peer_experts(x2, gain, sc, sh, g2, idx0, gates0, qp, sub_keys, packed, seq):
    t, d = x2.shape
    tp = PEER_TP
    tm = ROUTE_TM
    ahead = ROUTE_AHEAD_BUFS - 1
    assert tm == PEER_HEADS * tp
    n_steps = t // tp
    n_tiles = t // tm
    spt = tm // tp
    per_b = seq // tp
    whole = lambda s: (0, 0)
    return pl.pallas_call(
        _expert_kernel,
        grid=(n_steps,),
        in_specs=[
            pl.BlockSpec((ahead * tm, PEER_PICKS), whole),
            pl.BlockSpec((ahead * tm, PEER_PICKS), whole),
            pl.BlockSpec((2 * PEER_HEADS, tm, PEER_DKEY // 2),
                         lambda s: (0, jnp.minimum(s // spt + ahead, n_tiles - 1), 0)),
            pl.BlockSpec((2, PEER_NKEYS, PEER_DKEY // 2), lambda s: (0, 0, 0)),
            pl.BlockSpec((tp, d), lambda s: (s, 0)),
            pl.BlockSpec((1, d), lambda s: (0, 0)),
            pl.BlockSpec((None, 1, d), lambda s: (s // per_b, 0, 0)),
            pl.BlockSpec((None, 1, d), lambda s: (s // per_b, 0, 0)),
            pl.BlockSpec((None, 1, d), lambda s: (s // per_b, 0, 0)),
            pl.BlockSpec(memory_space=pl.ANY),
        ],
        out_specs=pl.BlockSpec((tp, d), lambda s: (s, 0)),
        out_shape=jax.ShapeDtypeStruct((t, d), F32),
        scratch_shapes=(
            [pltpu.VMEM((tp // PEER_NBUF * PEER_PICKS, d), jnp.uint32)] * PEER_NBUF
            + [pltpu.SemaphoreType.DMA((PEER_NBUF,)),
               pltpu.SMEM((ROUTE_AHEAD_BUFS, tm, PEER_PICKS), jnp.int32),
               pltpu.VMEM((ROUTE_AHEAD_BUFS, tm, PEER_PICKS), F32),
               pltpu.VMEM((PEER_PICKS, tm), jnp.int32),
               pltpu.VMEM((PEER_PICKS, tm), F32),
               pltpu.VMEM((tm, PEER_PICKS), jnp.int32),
               pltpu.SemaphoreType.DMA((1,))]
            + _route_scratch(tm)),
        compiler_params=_cparams("arbitrary"),
        name="peer_experts",
    )(idx0, gates0, qp, sub_keys, x2, gain.reshape(1, d), sc, sh, g2, packed)


def _peer_ffn(x2, gain, sc, sh, g2, w_query, sub_keys, table_u, table_v, seq):
    no_gain = jnp.ones((1, w_query.shape[1]), F32)
    qp = _norm_proj(x2, gain, sc, sh, w_query.astype(BF16), no_gain, (), seq)
    keys = sub_keys.astype(BF16)
    lead = (ROUTE_AHEAD_BUFS - 1) * ROUTE_TM
    idx0, gates0 = _peer_route(qp, keys, lead)
    idx0 = idx0.transpose(2, 0, 1).reshape(lead, PEER_PICKS)
    gates0 = gates0.transpose(2, 0, 1).reshape(lead, PEER_PICKS)
    packed = _pack_expert_tables(table_u, table_v)
    return _peer_experts(x2, gain, sc, sh, g2, idx0, gates0, qp, keys, packed, seq)


def _tile_gain(gain, heads, scale=1.0):
    return jnp.tile(gain * scale, heads)


def _even_layer(x2, mods, norm_gain, w_in, b_forget, fox_qk_gain, diff_qk_gain, diff_lambda,
                diff_subln_gain, w_out, diff_bias, lam_init, bsz, seq):
    sh1, sc1, g1 = mods
    w_main = jnp.concatenate([w_in[:, :3 * FOX_W], w_in[:, 3 * FOX_W + FOX_HEADS:]], axis=1).astype(BF16)
    w_forget = jnp.zeros((D_MODEL, LANES), F32).at[:, :FOX_HEADS].set(
        w_in[:, 3 * FOX_W:3 * FOX_W + FOX_HEADS]).astype(BF16)
    ones = jnp.ones((FOX_W,), F32)
    head_gain = jnp.concatenate([
        _tile_gain(fox_qk_gain[0], FOX_HEADS, ATTN_SCALE), _tile_gain(fox_qk_gain[1], FOX_HEADS), ones,
        _tile_gain(diff_qk_gain[0], 2 * DIFF_HEADS, ATTN_SCALE), _tile_gain(diff_qk_gain[1], 2 * DIFF_HEADS),
        ones]).reshape(1, -1)
    proj, f_side = _norm_proj(x2, norm_gain, sc1, sh1, w_main, head_gain, (0, 1, 3, 4), seq,
                              w_side=w_forget)
    fcol, frow = _forget_cumsum(f_side, b_forget, bsz, seq)
    fox_o = _fox_attention(proj, frow, fcol, bsz, seq)
    bias_tiles = _causal_bias_tiles(diff_bias, seq, ATTN_T)
    diff_o = _diff_attention(proj, bias_tiles, diff_lambda, diff_subln_gain, lam_init, bsz, seq)
    return _out_proj([fox_o, diff_o], w_out.astype(BF16), x2, g1, seq)


def _odd_layer(x2, mods, norm_gain, w_qkv, qk_gain, w_out, bias_table, bsz, seq):
    sh1, sc1, g1 = mods
    ones = jnp.ones((D_MODEL,), F32)
    head_gain = jnp.concatenate([
        _tile_gain(qk_gain[0], DIL_HEADS, ATTN_SCALE), _tile_gain(qk_gain[1], DIL_HEADS), ones]).reshape(1, -1)
    qkv = _norm_proj(x2, norm_gain, sc1, sh1, w_qkv.astype(BF16), head_gain, (0, 1, 2, 3), seq)
    mixed = _dilated_attention(qkv, _dilated_bias_tiles(bias_table), bsz, seq)
    return _out_proj([mixed], w_out.astype(BF16), x2, g1, seq)


def kernel(x, c, rel_bias, norm_gain, w_ada, b_ada, even_w_in, even_b_forget, even_fox_qk_gain,
           even_diff_qk_gain, even_diff_lambda, even_diff_subln_gain, even_w_out, odd_w_qkv,
           odd_qk_gain, odd_w_out, peer_w_query, peer_sub_keys, peer_u, peer_v):
    bsz, seq, d = x.shape
    x2 = x.reshape(bsz * seq, d)
    mod = _adaln(c, w_ada, b_ada)
    for i in range(DEPTH):
        sh1, sc1, g1, sh2, sc2, g2 = [m.reshape(bsz, 1, d) for m in jnp.split(mod[i], 6, axis=-1)]
        j = i // 2
        if i % 2 == 0:
            lam_init = 0.8 - 0.6 * math.exp(-0.3 * i)
            x2 = _even_layer(x2, (sh1, sc1, g1), norm_gain[i, 0], even_w_in[j], even_b_forget[j],
                             even_fox_qk_gain[j], even_diff_qk_gain[j], even_diff_lambda[j],
                             even_diff_subln_gain[j], even_w_out[j], rel_bias[:, :DIFF_HEADS],
                             lam_init, bsz, seq)
        else:
            x2 = _odd_layer(x2, (sh1, sc1, g1), norm_gain[i, 0], odd_w_qkv[j], odd_qk_gain[j],
                            odd_w_out[j], rel_bias, bsz, seq)
        x2 = _peer_ffn(x2, norm_gain[i, 1], sc2, sh2, g2, peer_w_query[i], peer_sub_keys[i],
                       peer_u[i], peer_v[i], seq)
    return x2.reshape(bsz, seq, d)
```

```python
import functools
import math

import numpy as np
import jax
import jax.numpy as jnp
from jax import lax
from jax.experimental import pallas as pl
from jax.experimental.pallas import tpu as pltpu

F32 = jnp.float32
BF16 = jnp.bfloat16

D_MODEL = 2048
DEPTH = 2
HEAD_DIM = 128
FOX_HEADS = 8
DIFF_HEADS = 4
DIFF_V_DIM = 2 * HEAD_DIM
DIL_HEADS = D_MODEL // HEAD_DIM
DIL_PATTERNS = ((128, 1), (512, 4), (2048, 16))
DIL_BLOCK = 128
N_BUCKETS = 32
BUCKET_MAX_EXACT = 16
BUCKET_MAX_DIST = 2048
PEER_HEADS = 8
PEER_NKEYS = 128
PEER_EXPERTS = PEER_NKEYS * PEER_NKEYS
PEER_DKEY = 256
PEER_TOPK = 16
PEER_PICKS = PEER_HEADS * PEER_TOPK
NORM_EPS = 1e-6
FOX_W = FOX_HEADS * HEAD_DIM
DIFF_QK_W = DIFF_HEADS * 2 * HEAD_DIM
DIFF_V_W = DIFF_HEADS * DIFF_V_DIM
ATTN_SCALE = HEAD_DIM ** -0.5
MASKED = -1e30

LANES = 128
SUBLANES = 8
VMEM_LIMIT = 48 * 1024 * 1024

PROJ_TM = 512
PROJ_TN = 1024
ATTN_T = 512
ROUTE_TM = 128
PEER_TP = 16


def _cparams(*sem):
    return pltpu.CompilerParams(dimension_semantics=sem, vmem_limit_bytes=VMEM_LIMIT)


def _adaln_kernel(c_ref, w_ref, b_ref, o_ref):
    c = c_ref[...]
    cond = c * jax.nn.sigmoid(c)
    o_ref[...] = jnp.dot(cond.astype(BF16), w_ref[...].astype(BF16),
                         preferred_element_type=F32) + b_ref[...]


def _adaln(c, w_ada, b_ada):
    depth, d, n = w_ada.shape
    bsz = c.shape[0]
    tn = 1024
    return pl.pallas_call(
        _adaln_kernel,
        grid=(depth, n // tn),
        in_specs=[
            pl.BlockSpec((bsz, d), lambda i, j: (0, 0)),
            pl.BlockSpec((None, d, tn), lambda i, j: (i, 0, j)),
            pl.BlockSpec((None, 1, tn), lambda i, j: (i, 0, j)),
        ],
        out_specs=pl.BlockSpec((None, bsz, tn), lambda i, j: (i, 0, j)),
        out_shape=jax.ShapeDtypeStruct((depth, bsz, n), F32),
        compiler_params=_cparams("parallel", "parallel"),
        name="adaln",
    )(c, w_ada, b_ada.reshape(depth, 1, n))


def _modulated_norm(x, gain, sc, sh):
    ms = jnp.mean(x * x, axis=-1, keepdims=True)
    y = x * lax.rsqrt(ms + NORM_EPS) * gain
    return y * (1.0 + sc) + sh


def _norm_proj_kernel(*refs, norm_tiles, has_side):
    if has_side:
        x_ref, gain_ref, sc_ref, sh_ref, w_ref, hg_ref, ws_ref, o_ref, side_ref, h_ref = refs
    else:
        x_ref, gain_ref, sc_ref, sh_ref, w_ref, hg_ref, o_ref, h_ref = refs
    j = pl.program_id(1)

    @pl.when(j == 0)
    def _():
        h = _modulated_norm(x_ref[...], gain_ref[...], sc_ref[...], sh_ref[...])
        h_ref[...] = h.astype(BF16)
        if has_side:
            side_ref[...] = jnp.dot(h_ref[...], ws_ref[...], preferred_element_type=F32)

    acc = jnp.dot(h_ref[...], w_ref[...], preferred_element_type=F32)
    tn = acc.shape[1]

    def write_normed():
        for g in range(tn // HEAD_DIM):
            sl = slice(g * HEAD_DIM, (g + 1) * HEAD_DIM)
            blk = acc[:, sl]
            ms = jnp.mean(blk * blk, axis=-1, keepdims=True)
            o_ref[g] = (blk * lax.rsqrt(ms + NORM_EPS) * hg_ref[:, sl]).astype(o_ref.dtype)

    def write_raw():
        for g in range(tn // HEAD_DIM):
            o_ref[g] = acc[:, g * HEAD_DIM:(g + 1) * HEAD_DIM].astype(o_ref.dtype)

    if not norm_tiles:
        write_raw()
    else:
        is_norm = functools.reduce(jnp.logical_or, [j == t for t in norm_tiles])
        pl.when(is_norm)(write_normed)
        pl.when(jnp.logical_not(is_norm))(write_raw)


def _norm_proj(x2, gain, sc, sh, w, head_gain, norm_tiles, seq, w_side=None):
    t, d = x2.shape
    n = w.shape[1]
    tm, tn = PROJ_TM, PROJ_TN
    per_b = seq // tm
    has_side = w_side is not None
    gpt = tn // HEAD_DIM
    in_specs = [
        pl.BlockSpec((tm, d), lambda i, j: (i, 0)),
        pl.BlockSpec((1, d), lambda i, j: (0, 0)),
        pl.BlockSpec((None, 1, d), lambda i, j: (i // per_b, 0, 0)),
        pl.BlockSpec((None, 1, d), lambda i, j: (i // per_b, 0, 0)),
        pl.BlockSpec((d, tn), lambda i, j: (0, j)),
        pl.BlockSpec((1, tn), lambda i, j: (0, j)),
    ]
    args = [x2, gain.reshape(1, d), sc, sh, w, head_gain]
    out_specs = pl.BlockSpec((gpt, tm, HEAD_DIM), lambda i, j: (j, i, 0))
    out_shape = jax.ShapeDtypeStruct((n // HEAD_DIM, t, HEAD_DIM), BF16)
    if has_side:
        ns = w_side.shape[1]
        in_specs.append(pl.BlockSpec((d, ns), lambda i, j: (0, 0)))
        args.append(w_side)
        out_specs = [out_specs, pl.BlockSpec((tm, ns), lambda i, j: (i, 0))]
        out_shape = [out_shape, jax.ShapeDtypeStruct((t, ns), F32)]
    return pl.pallas_call(
        functools.partial(_norm_proj_kernel, norm_tiles=tuple(norm_tiles), has_side=has_side),
        grid=(t // tm, n // tn),
        in_specs=in_specs,
        out_specs=out_specs,
        out_shape=out_shape,
        scratch_shapes=[pltpu.VMEM((tm, d), BF16)],
        compiler_params=_cparams("parallel", "arbitrary"),
        name="norm_proj",
    )(*args)


def _out_proj_kernel(*refs):
    *a_refs, w_ref, x_ref, g_ref, o_ref = refs
    a = jnp.concatenate([a_ref[g] for a_ref in a_refs for g in range(a_ref.shape[0])], axis=1)
    y = jnp.dot(a, w_ref[...], preferred_element_type=F32)
    o_ref[...] = x_ref[...] + g_ref[...] * y


def _out_proj(heads_major, w, x2, g, seq):
    t = x2.shape[0]
    k, n = w.shape
    assert sum(a.shape[0] for a in heads_major) * HEAD_DIM == k
    tm, tn = PROJ_TM, PROJ_TN
    per_b = seq // tm
    return pl.pallas_call(
        _out_proj_kernel,
        grid=(t // tm, n // tn),
        in_specs=[pl.BlockSpec((a.shape[0], tm, HEAD_DIM), lambda i, j: (0, i, 0)) for a in heads_major] + [
            pl.BlockSpec((k, tn), lambda i, j: (0, j)),
            pl.BlockSpec((tm, tn), lambda i, j: (i, j)),
            pl.BlockSpec((None, 1, tn), lambda i, j: (i // per_b, 0, j)),
        ],
        out_specs=pl.BlockSpec((tm, tn), lambda i, j: (i, j)),
        out_shape=jax.ShapeDtypeStruct((t, n), F32),
        compiler_params=_cparams("parallel", "parallel"),
        name="out_proj",
    )(*heads_major, w, x2, g)


def _forget_kernel(f_ref, b_ref, col_ref, row_ref):
    z = f_ref[...] + b_ref[...]
    lf = jnp.minimum(z, 0.0) - jnp.log1p(jnp.exp(-jnp.abs(z)))
    s = lf.shape[0]
    pos = lax.broadcasted_iota(jnp.int32, lf.shape, 0)
    d = 1
    while d < s:
        lf = lf + jnp.where(pos >= d, pltpu.roll(lf, d, axis=0), 0.0)
        d *= 2
    col_ref[...] = lf
    row_ref[...] = lf.T[:FOX_HEADS, :]


def _forget_cumsum(f_side, b_forget, bsz, seq):
    bpad = jnp.zeros((1, LANES), F32).at[0, :FOX_HEADS].set(b_forget)
    col, row = pl.pallas_call(
        _forget_kernel,
        grid=(bsz,),
        in_specs=[
            pl.BlockSpec((None, seq, LANES), lambda b: (b, 0, 0)),
            pl.BlockSpec((1, LANES), lambda b: (0, 0)),
        ],
        out_specs=[
            pl.BlockSpec((None, seq, LANES), lambda b: (b, 0, 0)),
            pl.BlockSpec((None, FOX_HEADS, seq), lambda b: (b, 0, 0)),
        ],
        out_shape=[
            jax.ShapeDtypeStruct((bsz, seq, LANES), F32),
            jax.ShapeDtypeStruct((bsz, FOX_HEADS, seq), F32),
        ],
        compiler_params=_cparams("parallel"),
        name="forget_cumsum",
    )(f_side.reshape(bsz, seq, LANES), bpad)
    return col, row.reshape(bsz, FOX_HEADS, 1, seq)


def _online_softmax_step(s, v, m_ref, l_ref, acc_ref):
    m_prev = m_ref[...]
    m_new = jnp.maximum(m_prev, jnp.max(s, axis=-1, keepdims=True))
    alpha = jnp.exp(m_prev - m_new)
    p = jnp.exp(s - m_new)
    l_ref[...] = alpha * l_ref[...] + jnp.sum(p, axis=-1, keepdims=True)
    acc_ref[...] = alpha * acc_ref[...] + jnp.dot(p.astype(v.dtype), v, preferred_element_type=F32)
    m_ref[...] = m_new


def _fox_kernel(q_ref, k_ref, v_ref, frow_ref, fcol_ref, o_ref, m_ref, l_ref, acc_ref, fq_ref):
    h = pl.program_id(1)
    qi = pl.program_id(2)
    ki = pl.program_id(3)
    tq, tk = q_ref.shape[0], k_ref.shape[0]

    @pl.when(ki == 0)
    def _():
        m_ref[...] = jnp.full_like(m_ref, MASKED)
        l_ref[...] = jnp.zeros_like(l_ref)
        acc_ref[...] = jnp.zeros_like(acc_ref)
        lane = lax.broadcasted_iota(jnp.int32, fcol_ref.shape, 1)
        fq_ref[...] = jnp.sum(jnp.where(lane == h, fcol_ref[...], 0.0), axis=-1, keepdims=True)

    @pl.when(ki <= qi)
    def _():
        s = lax.dot_general(q_ref[...], k_ref[...], (((1,), (1,)), ((), ())),
                            preferred_element_type=F32)
        s = s + fq_ref[...] - frow_ref[...]
        qpos = qi * tq + lax.broadcasted_iota(jnp.int32, s.shape, 0)
        kpos = ki * tk + lax.broadcasted_iota(jnp.int32, s.shape, 1)
        s = jnp.where(qpos >= kpos, s, MASKED)
        _online_softmax_step(s, v_ref[...], m_ref, l_ref, acc_ref)

    @pl.when(ki == qi)
    def _():
        o_ref[...] = (acc_ref[...] / l_ref[...]).astype(o_ref.dtype)


def _fox_attention(proj, frow, fcol, bsz, seq):
    t = proj.shape[1]
    tt = ATTN_T
    nb = seq // tt
    kcol = FOX_W // HEAD_DIM
    blk = (None, tt, HEAD_DIM)
    return pl.pallas_call(
        _fox_kernel,
        grid=(bsz, FOX_HEADS, nb, nb),
        in_specs=[
            pl.BlockSpec(blk, lambda b, h, qi, ki: (h, b * nb + qi, 0)),
            pl.BlockSpec(blk, lambda b, h, qi, ki: (kcol + h, b * nb + jnp.minimum(ki, qi), 0)),
            pl.BlockSpec(blk, lambda b, h, qi, ki: (2 * kcol + h, b * nb + jnp.minimum(ki, qi), 0)),
            pl.BlockSpec((None, None, 1, tt), lambda b, h, qi, ki: (b, h, 0, jnp.minimum(ki, qi))),
            pl.BlockSpec((None, tt, LANES), lambda b, h, qi, ki: (b, qi, 0)),
        ],
        out_specs=pl.BlockSpec(blk, lambda b, h, qi, ki: (h, b * nb + qi, 0)),
        out_shape=jax.ShapeDtypeStruct((FOX_HEADS, t, HEAD_DIM), BF16),
        scratch_shapes=[
            pltpu.VMEM((tt, 1), F32), pltpu.VMEM((tt, 1), F32),
            pltpu.VMEM((tt, HEAD_DIM), F32), pltpu.VMEM((tt, 1), F32),
        ],
        compiler_params=_cparams("parallel", "parallel", "parallel", "arbitrary"),
        name="fox_attention",
    )(proj, proj, proj, frow, fcol)


def _diff_kernel(q_ref, k_ref, v_ref, bias_ref, lam_ref, sg_ref, o_ref,
                 m0, l0, a0, m1, l1, a1, *, lam_init):
    qi = pl.program_id(2)
    ki = pl.program_id(3)

    @pl.when(ki == 0)
    def _():
        for m_ref, l_ref, a_ref in ((m0, l0, a0), (m1, l1, a1)):
            m_ref[...] = jnp.full_like(m_ref, MASKED)
            l_ref[...] = jnp.zeros_like(l_ref)
            a_ref[...] = jnp.zeros_like(a_ref)

    @pl.when(ki <= qi)
    def _():
        bias = bias_ref[...]
        v = jnp.concatenate([v_ref[0], v_ref[1]], axis=1)
        for mi, (m_ref, l_ref, a_ref) in enumerate(((m0, l0, a0), (m1, l1, a1))):
            s = lax.dot_general(q_ref[mi], k_ref[mi], (((1,), (1,)), ((), ())),
                                preferred_element_type=F32) + bias
            _online_softmax_step(s, v, m_ref, l_ref, a_ref)

    @pl.when(ki == qi)
    def _():
        lf = lam_ref[...]
        lam = (jnp.exp(jnp.sum(lf[0:1] * lf[1:2], axis=-1, keepdims=True))
               - jnp.exp(jnp.sum(lf[2:3] * lf[3:4], axis=-1, keepdims=True)) + lam_init)
        o = a0[...] / l0[...] - lam * (a1[...] / l1[...])
        ms = jnp.mean(o * o, axis=-1, keepdims=True)
        o = (o * lax.rsqrt(ms + NORM_EPS) * sg_ref[...] * (1.0 - lam_init)).astype(o_ref.dtype)
        o_ref[0] = o[:, :HEAD_DIM]
        o_ref[1] = o[:, HEAD_DIM:]


def _diff_attention(proj, bias_tiles, lam_param, subln_gain, lam_init, bsz, seq):
    t = proj.shape[1]
    tt = ATTN_T
    nb = seq // tt
    qcol = 3 * FOX_W // DIFF_V_DIM
    kcol = qcol + DIFF_QK_W // DIFF_V_DIM
    vcol = kcol + DIFF_QK_W // DIFF_V_DIM
    blk = (2, tt, HEAD_DIM)
    return pl.pallas_call(
        functools.partial(_diff_kernel, lam_init=lam_init),
        grid=(bsz, DIFF_HEADS, nb, nb),
        in_specs=[
            pl.BlockSpec(blk, lambda b, h, qi, ki: (qcol + h, b * nb + qi, 0)),
            pl.BlockSpec(blk, lambda b, h, qi, ki: (kcol + h, b * nb + jnp.minimum(ki, qi), 0)),
            pl.BlockSpec(blk, lambda b, h, qi, ki: (vcol + h, b * nb + jnp.minimum(ki, qi), 0)),
            pl.BlockSpec((None, None, tt, tt), lambda b, h, qi, ki: (h, jnp.maximum(qi - ki, 0), 0, 0)),
            pl.BlockSpec((4, HEAD_DIM), lambda b, h, qi, ki: (0, 0)),
            pl.BlockSpec((1, DIFF_V_DIM), lambda b, h, qi, ki: (0, 0)),
        ],
        out_specs=pl.BlockSpec(blk, lambda b, h, qi, ki: (h, b * nb + qi, 0)),
        out_shape=jax.ShapeDtypeStruct((2 * DIFF_HEADS, t, HEAD_DIM), BF16),
        scratch_shapes=[
            pltpu.VMEM((tt, 1), F32), pltpu.VMEM((tt, 1), F32), pltpu.VMEM((tt, DIFF_V_DIM), F32),
            pltpu.VMEM((tt, 1), F32), pltpu.VMEM((tt, 1), F32), pltpu.VMEM((tt, DIFF_V_DIM), F32),
        ],
        compiler_params=_cparams("parallel", "parallel", "parallel", "arbitrary"),
        name="diff_attention",
    )(proj, proj, proj, bias_tiles, lam_param, subln_gain.reshape(1, DIFF_V_DIM))


def _t5_bucket_np(dist):
    n = np.maximum(dist, 0)
    nf = np.maximum(n, 1).astype(np.float32)
    large = BUCKET_MAX_EXACT + (np.log(nf / np.float32(BUCKET_MAX_EXACT))
                                / np.float32(math.log(BUCKET_MAX_DIST / BUCKET_MAX_EXACT))
                                * np.float32(N_BUCKETS - BUCKET_MAX_EXACT)).astype(np.int32)
    large = np.minimum(large, N_BUCKETS - 1)
    return np.where(n < BUCKET_MAX_EXACT, n, large).astype(np.int32)


def _bucket_thresholds():
    buckets = _t5_bucket_np(np.arange(BUCKET_MAX_DIST + 1))
    assert np.all(np.diff(buckets) >= 0)
    return [int(np.argmax(buckets >= b)) for b in range(N_BUCKETS)]


_BUCKET_THRESHOLDS = _bucket_thresholds()


def _bias_from_distance(dist, valid, table_ref, h):
    val = jnp.full(dist.shape, table_ref[0, h], F32)
    for b in range(1, N_BUCKETS):
        val = jnp.where(dist >= _BUCKET_THRESHOLDS[b], table_ref[b, h], val)
    return jnp.where(valid, val, MASKED)


def _causal_bias_kernel(table_ref, o_ref):
    h, delta = pl.program_id(0), pl.program_id(1)
    tt = o_ref.shape[0]
    dist = (delta * tt + lax.broadcasted_iota(jnp.int32, o_ref.shape, 0)
            - lax.broadcasted_iota(jnp.int32, o_ref.shape, 1))
    o_ref[...] = _bias_from_distance(dist, dist >= 0, table_ref, h)


def _causal_bias_tiles(table, seq, tt):
    nb = seq // tt
    heads = table.shape[1]
    return pl.pallas_call(
        _causal_bias_kernel,
        grid=(heads, nb),
        in_specs=[pl.BlockSpec(memory_space=pltpu.SMEM)],
        out_specs=pl.BlockSpec((None, None, tt, tt), lambda h, dlt: (h, dlt, 0, 0)),
        out_shape=jax.ShapeDtypeStruct((heads, nb, tt, tt), F32),
        compiler_params=_cparams("parallel", "parallel"),
        name="causal_bias_tiles",
    )(table)


def _dilated_bias_kernel(table_ref, o_ref):
    h = pl.program_id(0)
    span = DIL_BLOCK
    shape = o_ref.shape[1:]
    step = lax.broadcasted_iota(jnp.int32, shape, 0) + span - lax.broadcasted_iota(jnp.int32, shape, 1)
    valid = jnp.logical_and(step >= 0, step <= span)
    for p, (_, dil) in enumerate(DIL_PATTERNS):
        o_ref[p] = _bias_from_distance(step * dil, valid, table_ref, h)


def _dilated_bias_tiles(table):
    heads = table.shape[1]
    npat = len(DIL_PATTERNS)
    return pl.pallas_call(
        _dilated_bias_kernel,
        grid=(heads,),
        in_specs=[pl.BlockSpec(memory_space=pltpu.SMEM)],
        out_specs=pl.BlockSpec((npat, None, DIL_BLOCK, 2 * DIL_BLOCK), lambda h: (0, h, 0, 0)),
        out_shape=jax.ShapeDtypeStruct((npat, heads, DIL_BLOCK, 2 * DIL_BLOCK), F32),
        compiler_params=_cparams("parallel"),
        name="dilated_bias_tiles",
    )(table)


DIL_GROUP = 4


def _dilated_kernel(q_ref, k_ref, v_ref, bias_ref, o_ref, qf, kf, vf, o_scr, lse_scr):
    seq = q_ref.shape[0]
    blk = DIL_BLOCK
    qf[...] = q_ref[...].astype(F32)
    for src, dst in ((k_ref, kf), (v_ref, vf)):
        dst[0:blk, :] = jnp.zeros((blk, HEAD_DIM), F32)
        dst[blk:, :] = src[...].astype(F32)
    dn_qk = (((2,), (2,)), ((0,), (0,)))
    dn_pv = (((2,), (1,)), ((0,), (0,)))

    def rows(ref, start, dil):
        if dil == 1:
            return ref[pl.ds(start, blk), :]
        return ref[pl.ds(start, blk, stride=dil), :]

    for p, (_, dil) in enumerate(DIL_PATTERNS):
        nblk = seq // dil // blk
        blocks = [(jb, r) for jb in range(nblk) for r in range(dil)]
        bias = bias_ref[p]
        for g0 in range(0, len(blocks), DIL_GROUP):
            group = blocks[g0:g0 + DIL_GROUP]
            starts = [jb * blk * dil + r for jb, r in group]
            has_prev = [jb > 0 for jb, _ in group]
            q = jnp.stack([rows(qf, st, dil) for st in starts]).astype(BF16)
            kc = jnp.stack([rows(kf, blk + st, dil) for st in starts]).astype(BF16)
            vc = jnp.stack([rows(vf, blk + st, dil) for st in starts]).astype(BF16)
            s_cur = lax.dot_general(q, kc, dn_qk, preferred_element_type=F32) + bias[:, blk:]
            m = jnp.max(s_cur, axis=-1, keepdims=True)
            if any(has_prev):
                prev_starts = [max(blk + st - blk * dil, 0) for st in starts]
                kp = jnp.stack([rows(kf, st, dil) for st in prev_starts]).astype(BF16)
                vp = jnp.stack([rows(vf, st, dil) for st in prev_starts]).astype(BF16)
                s_prev = lax.dot_general(q, kp, dn_qk, preferred_element_type=F32) + bias[:, :blk]
                which = lax.broadcasted_iota(jnp.int32, s_prev.shape, 0)
                for gi, ok in enumerate(has_prev):
                    if not ok:
                        s_prev = jnp.where(which == gi, MASKED, s_prev)
                m = jnp.maximum(m, jnp.max(s_prev, axis=-1, keepdims=True))
                p_prev = jnp.exp(s_prev - m)
            p_cur = jnp.exp(s_cur - m)
            l = jnp.sum(p_cur, axis=-1, keepdims=True)
            acc = lax.dot_general(p_cur.astype(BF16), vc, dn_pv, preferred_element_type=F32)
            if any(has_prev):
                l = l + jnp.sum(p_prev, axis=-1, keepdims=True)
                acc = acc + lax.dot_general(p_prev.astype(BF16), vp, dn_pv, preferred_element_type=F32)
            out = acc / l
            lse = jnp.broadcast_to(m + jnp.log(l), out.shape)
            for gi, st in enumerate(starts):
                if dil == 1:
                    o_scr[p, pl.ds(st, blk), :] = out[gi]
                    lse_scr[p, pl.ds(st, blk), :] = lse[gi]
                else:
                    o_scr[p, pl.ds(st, blk, stride=dil), :] = out[gi]
                    lse_scr[p, pl.ds(st, blk, stride=dil), :] = lse[gi]

    lses = [lse_scr[p] for p in range(len(DIL_PATTERNS))]
    top = functools.reduce(jnp.maximum, lses)
    es = [jnp.exp(x - top) for x in lses]
    den = functools.reduce(lambda a, b: a + b, es)
    mixed = functools.reduce(lambda a, b: a + b, [(e / den) * o_scr[p] for p, e in enumerate(es)])
    o_ref[...] = mixed.astype(o_ref.dtype)


def _dilated_attention(qkv, bias_tiles, bsz, seq):
    t = qkv.shape[1]
    h16 = DIL_HEADS
    npat = len(DIL_PATTERNS)
    blk = (None, seq, HEAD_DIM)
    return pl.pallas_call(
        _dilated_kernel,
        grid=(bsz, DIL_HEADS),
        in_specs=[
            pl.BlockSpec(blk, lambda b, h: (h, b, 0)),
            pl.BlockSpec(blk, lambda b, h: (h16 + h, b, 0)),
            pl.BlockSpec(blk, lambda b, h: (2 * h16 + h, b, 0)),
            pl.BlockSpec((npat, None, DIL_BLOCK, 2 * DIL_BLOCK), lambda b, h: (0, h, 0, 0)),
        ],
        out_specs=pl.BlockSpec(blk, lambda b, h: (h, b, 0)),
        out_shape=jax.ShapeDtypeStruct((DIL_HEADS, t, HEAD_DIM), BF16),
        scratch_shapes=[
            pltpu.VMEM((seq, HEAD_DIM), F32),
            pltpu.VMEM((DIL_BLOCK + seq, HEAD_DIM), F32),
            pltpu.VMEM((DIL_BLOCK + seq, HEAD_DIM), F32),
            pltpu.VMEM((npat, seq, HEAD_DIM), F32),
            pltpu.VMEM((npat, seq, HEAD_DIM), F32),
        ],
        compiler_params=_cparams("parallel", "parallel"),
        name="dilated_attention",
    )(qkv, qkv, qkv, bias_tiles)


def _topk_rows(s, val_ref, pick_ref, payload=None):
    neg_rows = -lax.broadcasted_iota(jnp.int32, s.shape, 0).astype(F32)
    for k in range(PEER_TOPK):
        m = jnp.max(s, axis=0, keepdims=True)
        am = jnp.max(jnp.where(s == m, neg_rows, -jnp.inf), axis=0, keepdims=True)
        hit = neg_rows == am
        val_ref[k:k + 1, :] = m
        if payload is None:
            pick_ref[k:k + 1, :] = (-am).astype(jnp.int32)
        else:
            pick_ref[k:k + 1, :] = jnp.sum(jnp.where(hit, payload, 0), axis=0, keepdims=True)
        s = jnp.where(hit, -jnp.inf, s)


def _candidate_rows(a):
    need = PEER_TOPK // (a + 1)
    if need > SUBLANES:
        return PEER_TOPK
    return SUBLANES if need > 1 else 1


def _route_kernel(q_ref, keys_ref, idx_ref, gate_ref, v0, i0, v1, i1, best_ref):
    _route_half(q_ref[0], keys_ref[0], v0, i0)
    _route_half(q_ref[1], keys_ref[1], v1, i1)
    _route_combine(idx_ref, gate_ref, v0, i0, v1, i1, best_ref)


def _route_half(q, keys, val_ref, idx_ref):
    scores = lax.dot_general(keys, q, (((1,), (1,)), ((), ())), preferred_element_type=F32)
    _topk_rows(scores, val_ref, idx_ref)


def _route_combine(idx_ref, gate_ref, v0, i0, v1, i1, best_ref):
    first_single = min(a for a in range(PEER_TOPK) if _candidate_rows(a) == 1)
    cand_s, cand_i = [], []
    for a in range(first_single):
        nb = _candidate_rows(a)
        cand_s.append(v0[a:a + 1, :] + v1[0:nb, :])
        cand_i.append(i0[a:a + 1, :] * PEER_NKEYS + i1[0:nb, :])
    cand_s.append(v0[first_single:, :] + v1[0:1, :])
    cand_i.append(i0[first_single:, :] * PEER_NKEYS + i1[0:1, :])
    _topk_rows(jnp.concatenate(cand_s, axis=0), best_ref, idx_ref, jnp.concatenate(cand_i, axis=0))
    best_s = best_ref[...]
    e = jnp.exp(best_s - best_s[0:1, :])
    gate_ref[...] = e / jnp.sum(e, axis=0, keepdims=True)


def _route_scratch(tm):
    return [pltpu.VMEM((PEER_TOPK, tm), F32), pltpu.VMEM((PEER_TOPK, tm), jnp.int32),
            pltpu.VMEM((PEER_TOPK, tm), F32), pltpu.VMEM((PEER_TOPK, tm), jnp.int32),
            pltpu.VMEM((PEER_TOPK, tm), F32)]


def _peer_route(qp, sub_keys, t):
    tm = ROUTE_TM
    out_spec = pl.BlockSpec((None, PEER_TOPK, tm), lambda i, h: (h, 0, i))
    return pl.pallas_call(
        _route_kernel,
        grid=(t // tm, PEER_HEADS),
        in_specs=[
            pl.BlockSpec((2, tm, PEER_DKEY // 2), lambda i, h: (h, i, 0)),
            pl.BlockSpec((2, PEER_NKEYS, PEER_DKEY // 2), lambda i, h: (0, 0, 0)),
        ],
        out_specs=[out_spec, out_spec],
        out_shape=[
            jax.ShapeDtypeStruct((PEER_HEADS, PEER_TOPK, t), jnp.int32),
            jax.ShapeDtypeStruct((PEER_HEADS, PEER_TOPK, t), F32),
        ],
        scratch_shapes=_route_scratch(tm),
        compiler_params=_cparams("parallel", "parallel"),
        name="peer_route",
    )(qp, sub_keys)


def _gelu_exact(a):
    return 0.5 * a * (1.0 + lax.erf(a * (2.0 ** -0.5)))


def _pack_expert_tables(tables_u, tables_v, layer):
    _, e, d = tables_u.shape
    te = PACK_ROWS
    n_steps = e // te
    layer_rows = pl.BlockSpec((None, te, d), lambda i: (layer, i, 0))
    return pl.pallas_call(
        _pack_kernel,
        grid=(n_steps,),
        in_specs=[layer_rows, layer_rows],
        out_specs=pl.BlockSpec(memory_space=pl.ANY),
        out_shape=jax.ShapeDtypeStruct((e, 1, d), jnp.uint32),
        scratch_shapes=[pltpu.VMEM((2, te, d), jnp.uint32), pltpu.SemaphoreType.DMA((2,))],
        compiler_params=_cparams("arbitrary"),
        name="pack_expert_tables",
    )(tables_u, tables_v)


PACK_ROWS = 128


def _pack_kernel(u_ref, v_ref, out_hbm, stage, sems):
    i = pl.program_id(0)
    n_steps = pl.num_programs(0)
    te, d = u_ref.shape
    half = d // 2
    slot = i % 2

    def bf16_high_bits(x):
        return lax.bitcast_convert_type(x.astype(BF16).astype(F32), jnp.uint32)

    def pack(ref):
        return bf16_high_bits(ref[:, half:]) | (bf16_high_bits(ref[:, :half]) >> 16)

    def wait_slot(sl):
        pltpu.make_async_copy(stage.at[sl], stage.at[sl], sems.at[sl]).wait()

    @pl.when(i >= 2)
    def _():
        wait_slot(slot)

    stage[slot] = jnp.concatenate([pack(u_ref), pack(v_ref)], axis=1)
    for r in range(te):
        pltpu.make_async_copy(stage.at[slot, pl.ds(r, 1)], out_hbm.at[i * te + r],
                              sems.at[slot]).start(priority=r % 2)

    @pl.when(i == n_steps - 1)
    def _():
        wait_slot(slot)

        @pl.when(n_steps > 1)
        def _():
            wait_slot(1 - slot)


def _unpack_words(w):
    lo = lax.bitcast_convert_type(w << 16, F32)
    hi = lax.bitcast_convert_type(w & jnp.uint32(0xFFFF0000), F32)
    return lo, hi


PEER_NBUF = 4
PEER_LOOKAHEAD = 2
ROUTE_AHEAD_BUFS = 3


def _expert_kernel(idx0_ref, gates0_ref, qnext_ref, keys_ref, x_ref, gain_ref, sc_ref, sh_ref, g2_ref,
                   tab_hbm, o_ref, *scratch):
    bufs, sems = scratch[:PEER_NBUF], scratch[PEER_NBUF]
    (idx_smem, gates_v, idx_t, gate_t, stage, route_sem), route_scratch = (
        scratch[PEER_NBUF + 1:PEER_NBUF + 7], scratch[PEER_NBUF + 7:])
    s = pl.program_id(0)
    n_steps = pl.num_programs(0)
    tp = x_ref.shape[0]
    gt = tp // PEER_NBUF
    d = x_ref.shape[1]
    ngroup = PEER_PICKS // SUBLANES
    nword = d // 2 // LANES
    spt = ROUTE_TM // tp
    r = s % spt
    par = (s // spt) % ROUTE_AHEAD_BUFS
    gain, sc, sh, g2 = gain_ref[...], sc_ref[...], sh_ref[...], g2_ref[...]
    eye = (lax.broadcasted_iota(jnp.int32, (PEER_PICKS, PEER_PICKS), 0)
           == lax.broadcasted_iota(jnp.int32, (PEER_PICKS, PEER_PICKS), 1))

    def park_routing(idx_rows, to_par):
        cp = pltpu.make_async_copy(idx_rows, idx_smem.at[to_par], route_sem.at[0])
        cp.start()
        cp.wait()

    @pl.when(s == 0)
    def _():
        for tile in range(ROUTE_AHEAD_BUFS - 1):
            park_routing(idx0_ref.at[pl.ds(tile * ROUTE_TM, ROUTE_TM)], tile)
            gates_v[tile] = gates0_ref[pl.ds(tile * ROUTE_TM, ROUTE_TM), :]

    head_rows = pl.ds(pl.multiple_of(r * PEER_TOPK, PEER_TOPK), PEER_TOPK)
    rv0, ri0, rv1, ri1, rbest = route_scratch
    route_stages = (
        lambda: _route_half(qnext_ref[2 * r], keys_ref[0], rv0, ri0),
        lambda: _route_half(qnext_ref[2 * r + 1], keys_ref[1], rv1, ri1),
        lambda: _route_combine(idx_t.at[head_rows], gate_t.at[head_rows], rv0, ri0, rv1, ri1, rbest),
    )

    def start_token(tile_par, row, j, ti):
        for g in range(ngroup):
            for k in range(SUBLANES):
                e = idx_smem[tile_par, row, g * SUBLANES + k]
                pltpu.make_async_copy(
                    tab_hbm.at[e],
                    bufs[j].at[pl.ds((ti * ngroup + g) * SUBLANES + k, 1)],
                    sems.at[j]).start(priority=k % 2)

    def wait_group(j):
        pltpu.make_async_copy(bufs[j], bufs[j], sems.at[j]).wait()

    def gated_activation(buf, ti, tok):
        h = _modulated_norm(x_ref[pl.ds(tok, 1), :], gain, sc, sh)
        hb = jnp.broadcast_to(h, (SUBLANES, d))
        parts = []
        for g in range(ngroup):
            acc = None
            for c in range(nword):
                lo, hi = _unpack_words(buf[pl.ds((ti * ngroup + g) * SUBLANES, SUBLANES),
                                           c * LANES:(c + 1) * LANES])
                term = (lo * hb[:, c * LANES:(c + 1) * LANES]
                        + hi * hb[:, (nword + c) * LANES:(nword + c + 1) * LANES])
                acc = term if acc is None else acc + term
            parts.append(acc)
        act = jnp.sum(jnp.concatenate(parts, axis=0), axis=-1, keepdims=True)
        gate_row = gates_v[par, pl.ds(r * tp + tok, 1), :]
        gate_col = jnp.sum(jnp.where(eye, gate_row, 0.0), axis=-1, keepdims=True)
        return gate_col * _gelu_exact(act)

    def weighted_values(buf, ti, tok, a):
        ylo = [None] * nword
        yhi = [None] * nword
        for g in range(ngroup):
            ag = a[g * SUBLANES:(g + 1) * SUBLANES, :]
            for c in range(nword):
                lo, hi = _unpack_words(buf[pl.ds((ti * ngroup + g) * SUBLANES, SUBLANES),
                                           (nword + c) * LANES:(nword + c + 1) * LANES])
                ylo[c] = lo * ag if ylo[c] is None else ylo[c] + lo * ag
                yhi[c] = hi * ag if yhi[c] is None else yhi[c] + hi * ag
        y = jnp.sum(jnp.concatenate(ylo + yhi, axis=1), axis=0, keepdims=True)
        o_ref[pl.ds(tok, 1), :] = x_ref[pl.ds(tok, 1), :] + g2 * y

    @pl.when(s == 0)
    def _():
        for j in range(PEER_LOOKAHEAD):
            def body(ti, carry, j=j):
                start_token(0, j * gt + ti, j, ti)
                return carry
            lax.fori_loop(0, gt, body, 0)

    last_of_tile = r == spt - 1
    next_par = jnp.where(last_of_tile, (par + 1) % ROUTE_AHEAD_BUFS, par)
    next_row = jnp.where(last_of_tile, 0, (r + 1) * tp)

    pending = None
    for j in range(PEER_NBUF):
        wait_group(j)
        if j < len(route_stages):
            route_stages[j]()
        ahead = j + PEER_LOOKAHEAD
        for ti in range(gt):
            tok = j * gt + ti
            if ahead < PEER_NBUF:
                start_token(par, r * tp + ahead * gt + ti, ahead, ti)
            else:
                start_token(next_par, next_row + (ahead - PEER_NBUF) * gt + ti, ahead - PEER_NBUF, ti)
            a = gated_activation(bufs[j], ti, tok)
            if pending is not None:
                weighted_values(*pending)
            pending = (bufs[j], ti, tok, a)
    weighted_values(*pending)

    @pl.when(last_of_tile)
    def _():
        free_par = (par + ROUTE_AHEAD_BUFS - 1) % ROUTE_AHEAD_BUFS
        stage[...] = idx_t[...].T
        gates_v[free_par] = gate_t[...].T
        park_routing(stage, free_par)

    @pl.when(s == n_steps - 1)
    def _():
        for j in range(PEER_LOOKAHEAD):
            wait_group(j)


def _peer_experts(x2, gain, sc, sh, g2, idx0, gates0, qp, sub_keys, packed, seq):
    t, d = x2.shape
    tp = PEER_TP
    tm = ROUTE_TM
    ahead = ROUTE_AHEAD_BUFS - 1
    assert tm == PEER_HEADS * tp
    n_steps = t // tp
    n_tiles = t // tm
    spt = tm // tp
    per_b = seq // tp
    whole = lambda s: (0, 0)
    return pl.pallas_call(
        _expert_kernel,
        grid=(n_steps,),
        in_specs=[
            pl.BlockSpec((ahead * tm, PEER_PICKS), whole),
            pl.BlockSpec((ahead * tm, PEER_PICKS), whole),
            pl.BlockSpec((2 * PEER_HEADS, tm, PEER_DKEY // 2),
                         lambda s: (0, jnp.minimum(s // spt + ahead, n_tiles - 1), 0)),
            pl.BlockSpec((2, PEER_NKEYS, PEER_DKEY // 2), lambda s: (0, 0, 0)),
            pl.BlockSpec((tp, d), lambda s: (s, 0)),
            pl.BlockSpec((1, d), lambda s: (0, 0)),
            pl.BlockSpec((None, 1, d), lambda s: (s // per_b, 0, 0)),
            pl.BlockSpec((None, 1, d), lambda s: (s // per_b, 0, 0)),
            pl.BlockSpec((None, 1, d), lambda s: (s // per_b, 0, 0)),
            pl.BlockSpec(memory_space=pl.ANY),
        ],
        out_specs=pl.BlockSpec((tp, d), lambda s: (s, 0)),
        out_shape=jax.ShapeDtypeStruct((t, d), F32),
        scratch_shapes=(
            [pltpu.VMEM((tp // PEER_NBUF * PEER_PICKS, d), jnp.uint32)] * PEER_NBUF
            + [pltpu.SemaphoreType.DMA((PEER_NBUF,)),
               pltpu.SMEM((ROUTE_AHEAD_BUFS, tm, PEER_PICKS), jnp.int32),
               pltpu.VMEM((ROUTE_AHEAD_BUFS, tm, PEER_PICKS), F32),
               pltpu.VMEM((PEER_PICKS, tm), jnp.int32),
               pltpu.VMEM((PEER_PICKS, tm), F32),
               pltpu.VMEM((tm, PEER_PICKS), jnp.int32),
               pltpu.SemaphoreType.DMA((1,))]
            + _route_scratch(tm)),
        compiler_params=_cparams("arbitrary"),
        name="peer_experts",
    )(idx0, gates0, qp, sub_keys, x2, gain.reshape(1, d), sc, sh, g2, packed)


def _peer_ffn(x2, gain, sc, sh, g2, w_query, sub_keys, tables_u, tables_v, layer, seq):
    no_gain = jnp.ones((1, w_query.shape[1]), F32)
    qp = _norm_proj(x2, gain, sc, sh, w_query.astype(BF16), no_gain, (), seq)
    keys = sub_keys.astype(BF16)
    lead = (ROUTE_AHEAD_BUFS - 1) * ROUTE_TM
    idx0, gates0 = _peer_route(qp, keys, lead)
    idx0 = idx0.transpose(2, 0, 1).reshape(lead, PEER_PICKS)
    gates0 = gates0.transpose(2, 0, 1).reshape(lead, PEER_PICKS)
    packed = _pack_expert_tables(tables_u, tables_v, layer)
    return _peer_experts(x2, gain, sc, sh, g2, idx0, gates0, qp, keys, packed, seq)


def _tile_gain(gain, heads, scale=1.0):
    return jnp.tile(gain * scale, heads)


def _even_layer(x2, mods, norm_gain, w_in, b_forget, fox_qk_gain, diff_qk_gain, diff_lambda,
                diff_subln_gain, w_out, diff_bias, lam_init, bsz, seq):
    sh1, sc1, g1 = mods
    w_main = jnp.concatenate([w_in[:, :3 * FOX_W], w_in[:, 3 * FOX_W + FOX_HEADS:]], axis=1).astype(BF16)
    w_forget = jnp.zeros((D_MODEL, LANES), F32).at[:, :FOX_HEADS].set(
        w_in[:, 3 * FOX_W:3 * FOX_W + FOX_HEADS]).astype(BF16)
    ones = jnp.ones((FOX_W,), F32)
    head_gain = jnp.concatenate([
        _tile_gain(fox_qk_gain[0], FOX_HEADS, ATTN_SCALE), _tile_gain(fox_qk_gain[1], FOX_HEADS), ones,
        _tile_gain(diff_qk_gain[0], 2 * DIFF_HEADS, ATTN_SCALE), _tile_gain(diff_qk_gain[1], 2 * DIFF_HEADS),
        ones]).reshape(1, -1)
    proj, f_side = _norm_proj(x2, norm_gain, sc1, sh1, w_main, head_gain, (0, 1, 3, 4), seq,
                              w_side=w_forget)
    fcol, frow = _forget_cumsum(f_side, b_forget, bsz, seq)
    fox_o = _fox_attention(proj, frow, fcol, bsz, seq)
    bias_tiles = _causal_bias_tiles(diff_bias, seq, ATTN_T)
    diff_o = _diff_attention(proj, bias_tiles, diff_lambda, diff_subln_gain, lam_init, bsz, seq)
    return _out_proj([fox_o, diff_o], w_out.astype(BF16), x2, g1, seq)


def _odd_layer(x2, mods, norm_gain, w_qkv, qk_gain, w_out, bias_table, bsz, seq):
    sh1, sc1, g1 = mods
    ones = jnp.ones((D_MODEL,), F32)
    head_gain = jnp.concatenate([
        _tile_gain(qk_gain[0], DIL_HEADS, ATTN_SCALE), _tile_gain(qk_gain[1], DIL_HEADS), ones]).reshape(1, -1)
    qkv = _norm_proj(x2, norm_gain, sc1, sh1, w_qkv.astype(BF16), head_gain, (0, 1, 2, 3), seq)
    mixed = _dilated_attention(qkv, _dilated_bias_tiles(bias_table), bsz, seq)
    return _out_proj([mixed], w_out.astype(BF16), x2, g1, seq)


def kernel(x, c, rel_bias, norm_gain, w_ada, b_ada, even_w_in, even_b_forget, even_fox_qk_gain,
           even_diff_qk_gain, even_diff_lambda, even_diff_subln_gain, even_w_out, odd_w_qkv,
           odd_qk_gain, odd_w_out, peer_w_query, peer_sub_keys, peer_u, peer_v):
    bsz, seq, d = x.shape
    x2 = x.reshape(bsz * seq, d)
    mod = _adaln(c, w_ada, b_ada)
    for i in range(DEPTH):
        sh1, sc1, g1, sh2, sc2, g2 = [m.reshape(bsz, 1, d) for m in jnp.split(mod[i], 6, axis=-1)]
        j = i // 2
        if i % 2 == 0:
            lam_init = 0.8 - 0.6 * math.exp(-0.3 * i)
            x2 = _even_layer(x2, (sh1, sc1, g1), norm_gain[i, 0], even_w_in[j], even_b_forget[j],
                             even_fox_qk_gain[j], even_diff_qk_gain[j], even_diff_lambda[j],
                             even_diff_subln_gain[j], even_w_out[j], rel_bias[:, :DIFF_HEADS],
                             lam_init, bsz, seq)
        else:
            x2 = _odd_layer(x2, (sh1, sc1, g1), norm_gain[i, 0], odd_w_qkv[j], odd_qk_gain[j],
                            odd_w_out[j], rel_bias, bsz, seq)
        x2 = _peer_ffn(x2, norm_gain[i, 1], sc2, sh2, g2, peer_w_query[i], peer_sub_keys[i],
                       peer_u, peer_v, i, seq)
    return x2.reshape(bsz, seq, d)
```

```python
import functools
import math

import numpy as np
import jax
import jax.numpy as jnp
from jax import lax
from jax.experimental import pallas as pl
from jax.experimental.pallas import tpu as pltpu

F32 = jnp.float32
BF16 = jnp.bfloat16

D_MODEL = 2048
DEPTH = 2
HEAD_DIM = 128
FOX_HEADS = 8
DIFF_HEADS = 4
DIFF_V_DIM = 2 * HEAD_DIM
DIL_HEADS = D_MODEL // HEAD_DIM
DIL_PATTERNS = ((128, 1), (512, 4), (2048, 16))
DIL_BLOCK = 128
N_BUCKETS = 32
BUCKET_MAX_EXACT = 16
BUCKET_MAX_DIST = 2048
PEER_HEADS = 8
PEER_NKEYS = 128
PEER_EXPERTS = PEER_NKEYS * PEER_NKEYS
PEER_DKEY = 256
PEER_TOPK = 16
PEER_PICKS = PEER_HEADS * PEER_TOPK
NORM_EPS = 1e-6
FOX_W = FOX_HEADS * HEAD_DIM
DIFF_QK_W = DIFF_HEADS * 2 * HEAD_DIM
DIFF_V_W = DIFF_HEADS * DIFF_V_DIM
ATTN_SCALE = HEAD_DIM ** -0.5
MASKED = -1e30

LANES = 128
SUBLANES = 8
VMEM_LIMIT = 48 * 1024 * 1024

PROJ_TM = 512
PROJ_TN = 1024
ATTN_T = 512
ROUTE_TM = 128
PEER_TP = 16


def _cparams(*sem):
    return pltpu.CompilerParams(dimension_semantics=sem, vmem_limit_bytes=VMEM_LIMIT)


def _adaln_kernel(c_ref, w_ref, b_ref, o_ref):
    c = c_ref[...]
    cond = c * jax.nn.sigmoid(c)
    o_ref[...] = jnp.dot(cond.astype(BF16), w_ref[...].astype(BF16),
                         preferred_element_type=F32) + b_ref[...]


def _adaln(c, w_ada, b_ada):
    depth, d, n = w_ada.shape
    bsz = c.shape[0]
    tn = 1024
    return pl.pallas_call(
        _adaln_kernel,
        grid=(depth, n // tn),
        in_specs=[
            pl.BlockSpec((bsz, d), lambda i, j: (0, 0)),
            pl.BlockSpec((None, d, tn), lambda i, j: (i, 0, j)),
            pl.BlockSpec((None, 1, tn), lambda i, j: (i, 0, j)),
        ],
        out_specs=pl.BlockSpec((None, bsz, tn), lambda i, j: (i, 0, j)),
        out_shape=jax.ShapeDtypeStruct((depth, bsz, n), F32),
        compiler_params=_cparams("parallel", "parallel"),
        name="adaln",
    )(c, w_ada, b_ada.reshape(depth, 1, n))


def _modulated_norm(x, gain, sc, sh):
    ms = jnp.mean(x * x, axis=-1, keepdims=True)
    y = x * lax.rsqrt(ms + NORM_EPS) * gain
    return y * (1.0 + sc) + sh


def _norm_proj_kernel(*refs, norm_tiles, has_side):
    if has_side:
        x_ref, gain_ref, sc_ref, sh_ref, w_ref, hg_ref, ws_ref, o_ref, side_ref, h_ref = refs
    else:
        x_ref, gain_ref, sc_ref, sh_ref, w_ref, hg_ref, o_ref, h_ref = refs
    j = pl.program_id(1)

    @pl.when(j == 0)
    def _():
        h = _modulated_norm(x_ref[...], gain_ref[...], sc_ref[...], sh_ref[...])
        h_ref[...] = h.astype(BF16)
        if has_side:
            side_ref[...] = jnp.dot(h_ref[...], ws_ref[...], preferred_element_type=F32)

    acc = jnp.dot(h_ref[...], w_ref[...], preferred_element_type=F32)
    tn = acc.shape[1]

    def write_normed():
        for g in range(tn // HEAD_DIM):
            sl = slice(g * HEAD_DIM, (g + 1) * HEAD_DIM)
            blk = acc[:, sl]
            ms = jnp.mean(blk * blk, axis=-1, keepdims=True)
            o_ref[g] = (blk * lax.rsqrt(ms + NORM_EPS) * hg_ref[:, sl]).astype(o_ref.dtype)

    def write_raw():
        for g in range(tn // HEAD_DIM):
            o_ref[g] = acc[:, g * HEAD_DIM:(g + 1) * HEAD_DIM].astype(o_ref.dtype)

    if not norm_tiles:
        write_raw()
    else:
        is_norm = functools.reduce(jnp.logical_or, [j == t for t in norm_tiles])
        pl.when(is_norm)(write_normed)
        pl.when(jnp.logical_not(is_norm))(write_raw)


def _norm_proj(x2, gain, sc, sh, w, head_gain, norm_tiles, seq, w_side=None):
    t, d = x2.shape
    n = w.shape[1]
    tm, tn = PROJ_TM, PROJ_TN
    per_b = seq // tm
    has_side = w_side is not None
    gpt = tn // HEAD_DIM
    in_specs = [
        pl.BlockSpec((tm, d), lambda i, j: (i, 0)),
        pl.BlockSpec((1, d), lambda i, j: (0, 0)),
        pl.BlockSpec((None, 1, d), lambda i, j: (i // per_b, 0, 0)),
        pl.BlockSpec((None, 1, d), lambda i, j: (i // per_b, 0, 0)),
        pl.BlockSpec((d, tn), lambda i, j: (0, j)),
        pl.BlockSpec((1, tn), lambda i, j: (0, j)),
    ]
    args = [x2, gain.reshape(1, d), sc, sh, w, head_gain]
    out_specs = pl.BlockSpec((gpt, tm, HEAD_DIM), lambda i, j: (j, i, 0))
    out_shape = jax.ShapeDtypeStruct((n // HEAD_DIM, t, HEAD_DIM), BF16)
    if has_side:
        ns = w_side.shape[1]
        in_specs.append(pl.BlockSpec((d, ns), lambda i, j: (0, 0)))
        args.append(w_side)
        out_specs = [out_specs, pl.BlockSpec((tm, ns), lambda i, j: (i, 0))]
        out_shape = [out_shape, jax.ShapeDtypeStruct((t, ns), F32)]
    return pl.pallas_call(
        functools.partial(_norm_proj_kernel, norm_tiles=tuple(norm_tiles), has_side=has_side),
        grid=(t // tm, n // tn),
        in_specs=in_specs,
        out_specs=out_specs,
        out_shape=out_shape,
        scratch_shapes=[pltpu.VMEM((tm, d), BF16)],
        compiler_params=_cparams("parallel", "arbitrary"),
        name="norm_proj",
    )(*args)


def _out_proj_kernel(*refs):
    *a_refs, w_ref, x_ref, g_ref, o_ref = refs
    a = jnp.concatenate([a_ref[g] for a_ref in a_refs for g in range(a_ref.shape[0])], axis=1)
    y = jnp.dot(a, w_ref[...], preferred_element_type=F32)
    o_ref[...] = x_ref[...] + g_ref[...] * y


def _out_proj(heads_major, w, x2, g, seq):
    t = x2.shape[0]
    k, n = w.shape
    assert sum(a.shape[0] for a in heads_major) * HEAD_DIM == k
    tm, tn = PROJ_TM, PROJ_TN
    per_b = seq // tm
    return pl.pallas_call(
        _out_proj_kernel,
        grid=(t // tm, n // tn),
        in_specs=[pl.BlockSpec((a.shape[0], tm, HEAD_DIM), lambda i, j: (0, i, 0)) for a in heads_major] + [
            pl.BlockSpec((k, tn), lambda i, j: (0, j)),
            pl.BlockSpec((tm, tn), lambda i, j: (i, j)),
            pl.BlockSpec((None, 1, tn), lambda i, j: (i // per_b, 0, j)),
        ],
        out_specs=pl.BlockSpec((tm, tn), lambda i, j: (i, j)),
        out_shape=jax.ShapeDtypeStruct((t, n), F32),
        compiler_params=_cparams("parallel", "parallel"),
        name="out_proj",
    )(*heads_major, w, x2, g)


def _forget_kernel(f_ref, b_ref, col_ref, row_ref):
    z = f_ref[...] + b_ref[...]
    lf = jnp.minimum(z, 0.0) - jnp.log1p(jnp.exp(-jnp.abs(z)))
    s = lf.shape[0]
    pos = lax.broadcasted_iota(jnp.int32, lf.shape, 0)
    d = 1
    while d < s:
        lf = lf + jnp.where(pos >= d, pltpu.roll(lf, d, axis=0), 0.0)
        d *= 2
    col_ref[...] = lf
    row_ref[...] = lf.T[:FOX_HEADS, :]


def _forget_cumsum(f_side, b_forget, bsz, seq):
    bpad = jnp.zeros((1, LANES), F32).at[0, :FOX_HEADS].set(b_forget)
    col, row = pl.pallas_call(
        _forget_kernel,
        grid=(bsz,),
        in_specs=[
            pl.BlockSpec((None, seq, LANES), lambda b: (b, 0, 0)),
            pl.BlockSpec((1, LANES), lambda b: (0, 0)),
        ],
        out_specs=[
            pl.BlockSpec((None, seq, LANES), lambda b: (b, 0, 0)),
            pl.BlockSpec((None, FOX_HEADS, seq), lambda b: (b, 0, 0)),
        ],
        out_shape=[
            jax.ShapeDtypeStruct((bsz, seq, LANES), F32),
            jax.ShapeDtypeStruct((bsz, FOX_HEADS, seq), F32),
        ],
        compiler_params=_cparams("parallel"),
        name="forget_cumsum",
    )(f_side.reshape(bsz, seq, LANES), bpad)
    return col, row.reshape(bsz, FOX_HEADS, 1, seq)


def _online_softmax_step(s, v, m_ref, l_ref, acc_ref):
    m_prev = m_ref[...]
    m_new = jnp.maximum(m_prev, jnp.max(s, axis=-1, keepdims=True))
    alpha = jnp.exp(m_prev - m_new)
    p = jnp.exp(s - m_new)
    l_ref[...] = alpha * l_ref[...] + jnp.sum(p, axis=-1, keepdims=True)
    acc_ref[...] = alpha * acc_ref[...] + jnp.dot(p.astype(v.dtype), v, preferred_element_type=F32)
    m_ref[...] = m_new


def _fox_kernel(q_ref, k_ref, v_ref, frow_ref, fcol_ref, o_ref, m_ref, l_ref, acc_ref):
    h = pl.program_id(1)
    qi = pl.program_id(2)
    tt = q_ref.shape[0]
    m_ref[...] = jnp.full_like(m_ref, MASKED)
    l_ref[...] = jnp.zeros_like(l_ref)
    acc_ref[...] = jnp.zeros_like(acc_ref)
    lane = lax.broadcasted_iota(jnp.int32, fcol_ref.shape, 1)
    fq = jnp.sum(jnp.where(lane == h, fcol_ref[...], 0.0), axis=-1, keepdims=True)
    q = q_ref[...]

    def kv_block(j, diagonal):
        rows = pl.ds(pl.multiple_of(j * tt, tt), tt)
        s = lax.dot_general(q, k_ref[rows, :], (((1,), (1,)), ((), ())), preferred_element_type=F32)
        s = s + fq - frow_ref[j]
        if diagonal:
            keep = (lax.broadcasted_iota(jnp.int32, s.shape, 0)
                    >= lax.broadcasted_iota(jnp.int32, s.shape, 1))
            s = jnp.where(keep, s, MASKED)
        _online_softmax_step(s, v_ref[rows, :], m_ref, l_ref, acc_ref)

    def below_diagonal(j, carry):
        kv_block(j, False)
        return carry

    lax.fori_loop(0, qi, below_diagonal, 0)
    kv_block(qi, True)
    o_ref[...] = (acc_ref[...] / l_ref[...]).astype(o_ref.dtype)


def _fox_attention(proj, frow, fcol, bsz, seq):
    t = proj.shape[1]
    tt = ATTN_T
    nb = seq // tt
    kcol = FOX_W // HEAD_DIM
    blk = (None, tt, HEAD_DIM)
    whole = (None, seq, HEAD_DIM)
    return pl.pallas_call(
        _fox_kernel,
        grid=(bsz, FOX_HEADS, nb),
        in_specs=[
            pl.BlockSpec(blk, lambda b, h, qi: (h, b * nb + qi, 0)),
            pl.BlockSpec(whole, lambda b, h, qi: (kcol + h, b, 0)),
            pl.BlockSpec(whole, lambda b, h, qi: (2 * kcol + h, b, 0)),
            pl.BlockSpec((None, None, nb, 1, tt), lambda b, h, qi: (b, h, 0, 0, 0)),
            pl.BlockSpec((None, tt, LANES), lambda b, h, qi: (b, qi, 0)),
        ],
        out_specs=pl.BlockSpec(blk, lambda b, h, qi: (h, b * nb + qi, 0)),
        out_shape=jax.ShapeDtypeStruct((FOX_HEADS, t, HEAD_DIM), BF16),
        scratch_shapes=[
            pltpu.VMEM((tt, 1), F32), pltpu.VMEM((tt, 1), F32), pltpu.VMEM((tt, HEAD_DIM), F32),
        ],
        compiler_params=_cparams("parallel", "parallel", "parallel"),
        name="fox_attention",
    )(proj, proj, proj, frow.reshape(bsz, FOX_HEADS, nb, 1, tt), fcol)


def _diff_kernel(q_ref, k_ref, v_ref, bias_ref, lam_ref, sg_ref, o_ref,
                 m0, l0, a0, m1, l1, a1, *, lam_init):
    qi = pl.program_id(2)
    tt = q_ref.shape[1]
    for m_ref, l_ref, a_ref in ((m0, l0, a0), (m1, l1, a1)):
        m_ref[...] = jnp.full_like(m_ref, MASKED)
        l_ref[...] = jnp.zeros_like(l_ref)
        a_ref[...] = jnp.zeros_like(a_ref)

    def kv_block(j, carry):
        rows = pl.ds(pl.multiple_of(j * tt, tt), tt)
        bias = bias_ref[qi - j]
        v = jnp.concatenate([v_ref[0, rows, :], v_ref[1, rows, :]], axis=1)
        for mi, (m_ref, l_ref, a_ref) in enumerate(((m0, l0, a0), (m1, l1, a1))):
            s = lax.dot_general(q_ref[mi], k_ref[mi, rows, :], (((1,), (1,)), ((), ())),
                                preferred_element_type=F32) + bias
            _online_softmax_step(s, v, m_ref, l_ref, a_ref)
        return carry

    lax.fori_loop(0, qi + 1, kv_block, 0)

    lf = lam_ref[...]
    lam = (jnp.exp(jnp.sum(lf[0:1] * lf[1:2], axis=-1, keepdims=True))
           - jnp.exp(jnp.sum(lf[2:3] * lf[3:4], axis=-1, keepdims=True)) + lam_init)
    o = a0[...] / l0[...] - lam * (a1[...] / l1[...])
    ms = jnp.mean(o * o, axis=-1, keepdims=True)
    o = (o * lax.rsqrt(ms + NORM_EPS) * sg_ref[...] * (1.0 - lam_init)).astype(o_ref.dtype)
    o_ref[0] = o[:, :HEAD_DIM]
    o_ref[1] = o[:, HEAD_DIM:]


def _diff_attention(proj, bias_tiles, lam_param, subln_gain, lam_init, bsz, seq):
    t = proj.shape[1]
    tt = ATTN_T
    nb = seq // tt
    qcol = 3 * FOX_W // DIFF_V_DIM
    kcol = qcol + DIFF_QK_W // DIFF_V_DIM
    vcol = kcol + DIFF_QK_W // DIFF_V_DIM
    blk = (2, tt, HEAD_DIM)
    whole = (2, seq, HEAD_DIM)
    return pl.pallas_call(
        functools.partial(_diff_kernel, lam_init=lam_init),
        grid=(bsz, DIFF_HEADS, nb),
        in_specs=[
            pl.BlockSpec(blk, lambda b, h, qi: (qcol + h, b * nb + qi, 0)),
            pl.BlockSpec(whole, lambda b, h, qi: (kcol + h, b, 0)),
            pl.BlockSpec(whole, lambda b, h, qi: (vcol + h, b, 0)),
            pl.BlockSpec((None, nb, tt, tt), lambda b, h, qi: (h, 0, 0, 0)),
            pl.BlockSpec((4, HEAD_DIM), lambda b, h, qi: (0, 0)),
            pl.BlockSpec((1, DIFF_V_DIM), lambda b, h, qi: (0, 0)),
        ],
        out_specs=pl.BlockSpec(blk, lambda b, h, qi: (h, b * nb + qi, 0)),
        out_shape=jax.ShapeDtypeStruct((2 * DIFF_HEADS, t, HEAD_DIM), BF16),
        scratch_shapes=[
            pltpu.VMEM((tt, 1), F32), pltpu.VMEM((tt, 1), F32), pltpu.VMEM((tt, DIFF_V_DIM), F32),
            pltpu.VMEM((tt, 1), F32), pltpu.VMEM((tt, 1), F32), pltpu.VMEM((tt, DIFF_V_DIM), F32),
        ],
        compiler_params=_cparams("parallel", "parallel", "parallel"),
        name="diff_attention",
    )(proj, proj, proj, bias_tiles, lam_param, subln_gain.reshape(1, DIFF_V_DIM))


def _t5_bucket_np(dist):
    n = np.maximum(dist, 0)
    nf = np.maximum(n, 1).astype(np.float32)
    large = BUCKET_MAX_EXACT + (np.log(nf / np.float32(BUCKET_MAX_EXACT))
                                / np.float32(math.log(BUCKET_MAX_DIST / BUCKET_MAX_EXACT))
                                * np.float32(N_BUCKETS - BUCKET_MAX_EXACT)).astype(np.int32)
    large = np.minimum(large, N_BUCKETS - 1)
    return np.where(n < BUCKET_MAX_EXACT, n, large).astype(np.int32)


def _bucket_thresholds():
    buckets = _t5_bucket_np(np.arange(BUCKET_MAX_DIST + 1))
    assert np.all(np.diff(buckets) >= 0)
    return [int(np.argmax(buckets >= b)) for b in range(N_BUCKETS)]


_BUCKET_THRESHOLDS = _bucket_thresholds()


def _bias_from_distance(dist, valid, table_ref, h):
    val = jnp.full(dist.shape, table_ref[0, h], F32)
    for b in range(1, N_BUCKETS):
        val = jnp.where(dist >= _BUCKET_THRESHOLDS[b], table_ref[b, h], val)
    return jnp.where(valid, val, MASKED)


def _causal_bias_kernel(table_ref, o_ref):
    h, delta = pl.program_id(0), pl.program_id(1)
    tt = o_ref.shape[0]
    dist = (delta * tt + lax.broadcasted_iota(jnp.int32, o_ref.shape, 0)
            - lax.broadcasted_iota(jnp.int32, o_ref.shape, 1))
    o_ref[...] = _bias_from_distance(dist, dist >= 0, table_ref, h)


def _causal_bias_tiles(table, seq, tt):
    nb = seq // tt
    heads = table.shape[1]
    return pl.pallas_call(
        _causal_bias_kernel,
        grid=(heads, nb),
        in_specs=[pl.BlockSpec(memory_space=pltpu.SMEM)],
        out_specs=pl.BlockSpec((None, None, tt, tt), lambda h, dlt: (h, dlt, 0, 0)),
        out_shape=jax.ShapeDtypeStruct((heads, nb, tt, tt), F32),
        compiler_params=_cparams("parallel", "parallel"),
        name="causal_bias_tiles",
    )(table)


def _dilated_bias_kernel(table_ref, o_ref):
    h = pl.program_id(0)
    span = DIL_BLOCK
    shape = o_ref.shape[1:]
    step = lax.broadcasted_iota(jnp.int32, shape, 0) + span - lax.broadcasted_iota(jnp.int32, shape, 1)
    valid = jnp.logical_and(step >= 0, step <= span)
    for p, (_, dil) in enumerate(DIL_PATTERNS):
        o_ref[p] = _bias_from_distance(step * dil, valid, table_ref, h)


def _dilated_bias_tiles(table):
    heads = table.shape[1]
    npat = len(DIL_PATTERNS)
    return pl.pallas_call(
        _dilated_bias_kernel,
        grid=(heads,),
        in_specs=[pl.BlockSpec(memory_space=pltpu.SMEM)],
        out_specs=pl.BlockSpec((npat, None, DIL_BLOCK, 2 * DIL_BLOCK), lambda h: (0, h, 0, 0)),
        out_shape=jax.ShapeDtypeStruct((npat, heads, DIL_BLOCK, 2 * DIL_BLOCK), F32),
        compiler_params=_cparams("parallel"),
        name="dilated_bias_tiles",
    )(table)


DIL_GROUP = 4


def _dilated_kernel(q_ref, k_ref, v_ref, bias_ref, o_ref, qf, kf, vf, o_scr, lse_scr):
    seq = q_ref.shape[0]
    blk = DIL_BLOCK
    qf[...] = q_ref[...].astype(F32)
    for src, dst in ((k_ref, kf), (v_ref, vf)):
        dst[0:blk, :] = jnp.zeros((blk, HEAD_DIM), F32)
        dst[blk:, :] = src[...].astype(F32)
    dn_qk = (((2,), (2,)), ((0,), (0,)))
    dn_pv = (((2,), (1,)), ((0,), (0,)))

    def rows(ref, start, dil):
        if dil == 1:
            return ref[pl.ds(start, blk), :]
        return ref[pl.ds(start, blk, stride=dil), :]

    for p, (_, dil) in enumerate(DIL_PATTERNS):
        nblk = seq // dil // blk
        blocks = [(jb, r) for jb in range(nblk) for r in range(dil)]
        bias = bias_ref[p]
        for g0 in range(0, len(blocks), DIL_GROUP):
            group = blocks[g0:g0 + DIL_GROUP]
            starts = [jb * blk * dil + r for jb, r in group]
            has_prev = [jb > 0 for jb, _ in group]
            q = jnp.stack([rows(qf, st, dil) for st in starts]).astype(BF16)
            kc = jnp.stack([rows(kf, blk + st, dil) for st in starts]).astype(BF16)
            vc = jnp.stack([rows(vf, blk + st, dil) for st in starts]).astype(BF16)
            s_cur = lax.dot_general(q, kc, dn_qk, preferred_element_type=F32) + bias[:, blk:]
            m = jnp.max(s_cur, axis=-1, keepdims=True)
            if any(has_prev):
                prev_starts = [max(blk + st - blk * dil, 0) for st in starts]
                kp = jnp.stack([rows(kf, st, dil) for st in prev_starts]).astype(BF16)
                vp = jnp.stack([rows(vf, st, dil) for st in prev_starts]).astype(BF16)
                s_prev = lax.dot_general(q, kp, dn_qk, preferred_element_type=F32) + bias[:, :blk]
                which = lax.broadcasted_iota(jnp.int32, s_prev.shape, 0)
                for gi, ok in enumerate(has_prev):
                    if not ok:
                        s_prev = jnp.where(which == gi, MASKED, s_prev)
                m = jnp.maximum(m, jnp.max(s_prev, axis=-1, keepdims=True))
                p_prev = jnp.exp(s_prev - m)
            p_cur = jnp.exp(s_cur - m)
            l = jnp.sum(p_cur, axis=-1, keepdims=True)
            acc = lax.dot_general(p_cur.astype(BF16), vc, dn_pv, preferred_element_type=F32)
            if any(has_prev):
                l = l + jnp.sum(p_prev, axis=-1, keepdims=True)
                acc = acc + lax.dot_general(p_prev.astype(BF16), vp, dn_pv, preferred_element_type=F32)
            out = acc / l
            lse = jnp.broadcast_to(m + jnp.log(l), out.shape)
            for gi, st in enumerate(starts):
                if dil == 1:
                    o_scr[p, pl.ds(st, blk), :] = out[gi]
                    lse_scr[p, pl.ds(st, blk), :] = lse[gi]
                else:
                    o_scr[p, pl.ds(st, blk, stride=dil), :] = out[gi]
                    lse_scr[p, pl.ds(st, blk, stride=dil), :] = lse[gi]

    lses = [lse_scr[p] for p in range(len(DIL_PATTERNS))]
    top = functools.reduce(jnp.maximum, lses)
    es = [jnp.exp(x - top) for x in lses]
    den = functools.reduce(lambda a, b: a + b, es)
    mixed = functools.reduce(lambda a, b: a + b, [(e / den) * o_scr[p] for p, e in enumerate(es)])
    o_ref[...] = mixed.astype(o_ref.dtype)


def _dilated_attention(qkv, bias_tiles, bsz, seq):
    t = qkv.shape[1]
    h16 = DIL_HEADS
    npat = len(DIL_PATTERNS)
    blk = (None, seq, HEAD_DIM)
    return pl.pallas_call(
        _dilated_kernel,
        grid=(bsz, DIL_HEADS),
        in_specs=[
            pl.BlockSpec(blk, lambda b, h: (h, b, 0)),
            pl.BlockSpec(blk, lambda b, h: (h16 + h, b, 0)),
            pl.BlockSpec(blk, lambda b, h: (2 * h16 + h, b, 0)),
            pl.BlockSpec((npat, None, DIL_BLOCK, 2 * DIL_BLOCK), lambda b, h: (0, h, 0, 0)),
        ],
        out_specs=pl.BlockSpec(blk, lambda b, h: (h, b, 0)),
        out_shape=jax.ShapeDtypeStruct((DIL_HEADS, t, HEAD_DIM), BF16),
        scratch_shapes=[
            pltpu.VMEM((seq, HEAD_DIM), F32),
            pltpu.VMEM((DIL_BLOCK + seq, HEAD_DIM), F32),
            pltpu.VMEM((DIL_BLOCK + seq, HEAD_DIM), F32),
            pltpu.VMEM((npat, seq, HEAD_DIM), F32),
            pltpu.VMEM((npat, seq, HEAD_DIM), F32),
        ],
        compiler_params=_cparams("parallel", "parallel"),
        name="dilated_attention",
    )(qkv, qkv, qkv, bias_tiles)


def _topk_rows(s, val_ref, pick_ref, payload=None):
    neg_rows = -lax.broadcasted_iota(jnp.int32, s.shape, 0).astype(F32)
    for k in range(PEER_TOPK):
        m = jnp.max(s, axis=0, keepdims=True)
        am = jnp.max(jnp.where(s == m, neg_rows, -jnp.inf), axis=0, keepdims=True)
        hit = neg_rows == am
        val_ref[k:k + 1, :] = m
        if payload is None:
            pick_ref[k:k + 1, :] = (-am).astype(jnp.int32)
        else:
            pick_ref[k:k + 1, :] = jnp.sum(jnp.where(hit, payload, 0), axis=0, keepdims=True)
        s = jnp.where(hit, -jnp.inf, s)


def _candidate_rows(a):
    need = PEER_TOPK // (a + 1)
    if need > SUBLANES:
        return PEER_TOPK
    return SUBLANES if need > 1 else 1


def _route_kernel(q_ref, keys_ref, idx_ref, gate_ref, v0, i0, v1, i1, best_ref):
    _route_half(q_ref[0], keys_ref[0], v0, i0)
    _route_half(q_ref[1], keys_ref[1], v1, i1)
    _route_combine(idx_ref, gate_ref, v0, i0, v1, i1, best_ref)


def _route_half(q, keys, val_ref, idx_ref):
    scores = lax.dot_general(keys, q, (((1,), (1,)), ((), ())), preferred_element_type=F32)
    _topk_rows(scores, val_ref, idx_ref)


def _route_combine(idx_ref, gate_ref, v0, i0, v1, i1, best_ref):
    first_single = min(a for a in range(PEER_TOPK) if _candidate_rows(a) == 1)
    cand_s, cand_i = [], []
    for a in range(first_single):
        nb = _candidate_rows(a)
        cand_s.append(v0[a:a + 1, :] + v1[0:nb, :])
        cand_i.append(i0[a:a + 1, :] * PEER_NKEYS + i1[0:nb, :])
    cand_s.append(v0[first_single:, :] + v1[0:1, :])
    cand_i.append(i0[first_single:, :] * PEER_NKEYS + i1[0:1, :])
    _topk_rows(jnp.concatenate(cand_s, axis=0), best_ref, idx_ref, jnp.concatenate(cand_i, axis=0))
    best_s = best_ref[...]
    e = jnp.exp(best_s - best_s[0:1, :])
    gate_ref[...] = e / jnp.sum(e, axis=0, keepdims=True)


def _route_scratch(tm):
    return [pltpu.VMEM((PEER_TOPK, tm), F32), pltpu.VMEM((PEER_TOPK, tm), jnp.int32),
            pltpu.VMEM((PEER_TOPK, tm), F32), pltpu.VMEM((PEER_TOPK, tm), jnp.int32),
            pltpu.VMEM((PEER_TOPK, tm), F32)]


def _peer_route(qp, sub_keys, t):
    tm = ROUTE_TM
    out_spec = pl.BlockSpec((None, PEER_TOPK, tm), lambda i, h: (h, 0, i))
    return pl.pallas_call(
        _route_kernel,
        grid=(t // tm, PEER_HEADS),
        in_specs=[
            pl.BlockSpec((2, tm, PEER_DKEY // 2), lambda i, h: (h, i, 0)),
            pl.BlockSpec((2, PEER_NKEYS, PEER_DKEY // 2), lambda i, h: (0, 0, 0)),
        ],
        out_specs=[out_spec, out_spec],
        out_shape=[
            jax.ShapeDtypeStruct((PEER_HEADS, PEER_TOPK, t), jnp.int32),
            jax.ShapeDtypeStruct((PEER_HEADS, PEER_TOPK, t), F32),
        ],
        scratch_shapes=_route_scratch(tm),
        compiler_params=_cparams("parallel", "parallel"),
        name="peer_route",
    )(qp, sub_keys)


def _gelu_exact(a):
    return 0.5 * a * (1.0 + lax.erf(a * (2.0 ** -0.5)))


def _pack_expert_tables(tables_u, tables_v, layer):
    _, e, d = tables_u.shape
    te = PACK_ROWS
    n_steps = e // te
    layer_rows = pl.BlockSpec((None, te, d), lambda i: (layer, i, 0))
    return pl.pallas_call(
        _pack_kernel,
        grid=(n_steps,),
        in_specs=[layer_rows, layer_rows],
        out_specs=pl.BlockSpec(memory_space=pl.ANY),
        out_shape=jax.ShapeDtypeStruct((e, 1, d), jnp.uint32),
        scratch_shapes=[pltpu.VMEM((2, te, d), jnp.uint32), pltpu.SemaphoreType.DMA((2,))],
        compiler_params=_cparams("arbitrary"),
        name="pack_expert_tables",
    )(tables_u, tables_v)


PACK_ROWS = 128


def _pack_kernel(u_ref, v_ref, out_hbm, stage, sems):
    i = pl.program_id(0)
    n_steps = pl.num_programs(0)
    te, d = u_ref.shape
    half = d // 2
    slot = i % 2

    def bf16_high_bits(x):
        return lax.bitcast_convert_type(x.astype(BF16).astype(F32), jnp.uint32)

    def pack(ref):
        return bf16_high_bits(ref[:, half:]) | (bf16_high_bits(ref[:, :half]) >> 16)

    def wait_slot(sl):
        pltpu.make_async_copy(stage.at[sl], stage.at[sl], sems.at[sl]).wait()

    @pl.when(i >= 2)
    def _():
        wait_slot(slot)

    stage[slot] = jnp.concatenate([pack(u_ref), pack(v_ref)], axis=1)
    for r in range(te):
        pltpu.make_async_copy(stage.at[slot, pl.ds(r, 1)], out_hbm.at[i * te + r],
                              sems.at[slot]).start(priority=r % 2)

    @pl.when(i == n_steps - 1)
    def _():
        wait_slot(slot)

        @pl.when(n_steps > 1)
        def _():
            wait_slot(1 - slot)


def _unpack_words(w):
    lo = lax.bitcast_convert_type(w << 16, F32)
    hi = lax.bitcast_convert_type(w & jnp.uint32(0xFFFF0000), F32)
    return lo, hi


PEER_NBUF = 4
PEER_LOOKAHEAD = 2
ROUTE_AHEAD_BUFS = 3


def _expert_kernel(idx0_ref, gates0_ref, qnext_ref, keys_ref, x_ref, gain_ref, sc_ref, sh_ref, g2_ref,
                   tab_hbm, o_ref, *scratch):
    bufs, sems = scratch[:PEER_NBUF], scratch[PEER_NBUF]
    (idx_smem, gates_v, idx_t, gate_t, stage, route_sem), route_scratch = (
        scratch[PEER_NBUF + 1:PEER_NBUF + 7], scratch[PEER_NBUF + 7:])
    s = pl.program_id(0)
    n_steps = pl.num_programs(0)
    tp = x_ref.shape[0]
    gt = tp // PEER_NBUF
    d = x_ref.shape[1]
    ngroup = PEER_PICKS // SUBLANES
    nword = d // 2 // LANES
    spt = ROUTE_TM // tp
    r = s % spt
    par = (s // spt) % ROUTE_AHEAD_BUFS
    gain, sc, sh, g2 = gain_ref[...], sc_ref[...], sh_ref[...], g2_ref[...]
    eye = (lax.broadcasted_iota(jnp.int32, (PEER_PICKS, PEER_PICKS), 0)
           == lax.broadcasted_iota(jnp.int32, (PEER_PICKS, PEER_PICKS), 1))

    def park_routing(idx_rows, to_par):
        cp = pltpu.make_async_copy(idx_rows, idx_smem.at[to_par], route_sem.at[0])
        cp.start()
        cp.wait()

    @pl.when(s == 0)
    def _():
        for tile in range(ROUTE_AHEAD_BUFS - 1):
            park_routing(idx0_ref.at[pl.ds(tile * ROUTE_TM, ROUTE_TM)], tile)
            gates_v[tile] = gates0_ref[pl.ds(tile * ROUTE_TM, ROUTE_TM), :]

    head_rows = pl.ds(pl.multiple_of(r * PEER_TOPK, PEER_TOPK), PEER_TOPK)
    rv0, ri0, rv1, ri1, rbest = route_scratch
    route_stages = (
        lambda: _route_half(qnext_ref[2 * r], keys_ref[0], rv0, ri0),
        lambda: _route_half(qnext_ref[2 * r + 1], keys_ref[1], rv1, ri1),
        lambda: _route_combine(idx_t.at[head_rows], gate_t.at[head_rows], rv0, ri0, rv1, ri1, rbest),
    )

    def start_token(tile_par, row, j, ti):
        for g in range(ngroup):
            for k in range(SUBLANES):
                e = idx_smem[tile_par, row, g * SUBLANES + k]
                pltpu.make_async_copy(
                    tab_hbm.at[e],
                    bufs[j].at[pl.ds((ti * ngroup + g) * SUBLANES + k, 1)],
                    sems.at[j]).start(priority=k % 2)

    def wait_group(j):
        pltpu.make_async_copy(bufs[j], bufs[j], sems.at[j]).wait()

    def gated_activation(buf, ti, tok):
        h = _modulated_norm(x_ref[pl.ds(tok, 1), :], gain, sc, sh)
        hb = jnp.broadcast_to(h, (SUBLANES, d))
        parts = []
        for g in range(ngroup):
            acc = None
            for c in range(nword):
                lo, hi = _unpack_words(buf[pl.ds((ti * ngroup + g) * SUBLANES, SUBLANES),
                                           c * LANES:(c + 1) * LANES])
                term = (lo * hb[:, c * LANES:(c + 1) * LANES]
                        + hi * hb[:, (nword + c) * LANES:(nword + c + 1) * LANES])
                acc = term if acc is None else acc + term
            parts.append(acc)
        act = jnp.sum(jnp.concatenate(parts, axis=0), axis=-1, keepdims=True)
        gate_row = gates_v[par, pl.ds(r * tp + tok, 1), :]
        gate_col = jnp.sum(jnp.where(eye, gate_row, 0.0), axis=-1, keepdims=True)
        return gate_col * _gelu_exact(act)

    def weighted_values(buf, ti, tok, a):
        ylo = [None] * nword
        yhi = [None] * nword
        for g in range(ngroup):
            ag = a[g * SUBLANES:(g + 1) * SUBLANES, :]
            for c in range(nword):
                lo, hi = _unpack_words(buf[pl.ds((ti * ngroup + g) * SUBLANES, SUBLANES),
                                           (nword + c) * LANES:(nword + c + 1) * LANES])
                ylo[c] = lo * ag if ylo[c] is None else ylo[c] + lo * ag
                yhi[c] = hi * ag if yhi[c] is None else yhi[c] + hi * ag
        y = jnp.sum(jnp.concatenate(ylo + yhi, axis=1), axis=0, keepdims=True)
        o_ref[pl.ds(tok, 1), :] = x_ref[pl.ds(tok, 1), :] + g2 * y

    @pl.when(s == 0)
    def _():
        for j in range(PEER_LOOKAHEAD):
            def body(ti, carry, j=j):
                start_token(0, j * gt + ti, j, ti)
                return carry
            lax.fori_loop(0, gt, body, 0)

    last_of_tile = r == spt - 1
    next_par = jnp.where(last_of_tile, (par + 1) % ROUTE_AHEAD_BUFS, par)
    next_row = jnp.where(last_of_tile, 0, (r + 1) * tp)

    pending = None
    for j in range(PEER_NBUF):
        wait_group(j)
        if j < len(route_stages):
            route_stages[j]()
        ahead = j + PEER_LOOKAHEAD
        for ti in range(gt):
            tok = j * gt + ti
            if ahead < PEER_NBUF:
                start_token(par, r * tp + ahead * gt + ti, ahead, ti)
            else:
                start_token(next_par, next_row + (ahead - PEER_NBUF) * gt + ti, ahead - PEER_NBUF, ti)
            a = gated_activation(bufs[j], ti, tok)
            if pending is not None:
                weighted_values(*pending)
            pending = (bufs[j], ti, tok, a)
    weighted_values(*pending)

    @pl.when(last_of_tile)
    def _():
        free_par = (par + ROUTE_AHEAD_BUFS - 1) % ROUTE_AHEAD_BUFS
        stage[...] = idx_t[...].T
        gates_v[free_par] = gate_t[...].T
        park_routing(stage, free_par)

    @pl.when(s == n_steps - 1)
    def _():
        for j in range(PEER_LOOKAHEAD):
            wait_group(j)


def _peer_experts(x2, gain, sc, sh, g2, idx0, gates0, qp, sub_keys, packed, seq):
    t, d = x2.shape
    tp = PEER_TP
    tm = ROUTE_TM
    ahead = ROUTE_AHEAD_BUFS - 1
    assert tm == PEER_HEADS * tp
    n_steps = t // tp
    n_tiles = t // tm
    spt = tm // tp
    per_b = seq // tp
    whole = lambda s: (0, 0)
    return pl.pallas_call(
        _expert_kernel,
        grid=(n_steps,),
        in_specs=[
            pl.BlockSpec((ahead * tm, PEER_PICKS), whole),
            pl.BlockSpec((ahead * tm, PEER_PICKS), whole),
            pl.BlockSpec((2 * PEER_HEADS, tm, PEER_DKEY // 2),
                         lambda s: (0, jnp.minimum(s // spt + ahead, n_tiles - 1), 0)),
            pl.BlockSpec((2, PEER_NKEYS, PEER_DKEY // 2), lambda s: (0, 0, 0)),
            pl.BlockSpec((tp, d), lambda s: (s, 0)),
            pl.BlockSpec((1, d), lambda s: (0, 0)),
            pl.BlockSpec((None, 1, d), lambda s: (s // per_b, 0, 0)),
            pl.BlockSpec((None, 1, d), lambda s: (s // per_b, 0, 0)),
            pl.BlockSpec((None, 1, d), lambda s: (s // per_b, 0, 0)),
            pl.BlockSpec(memory_space=pl.ANY),
        ],
        out_specs=pl.BlockSpec((tp, d), lambda s: (s, 0)),
        out_shape=jax.ShapeDtypeStruct((t, d), F32),
        scratch_shapes=(
            [pltpu.VMEM((tp // PEER_NBUF * PEER_PICKS, d), jnp.uint32)] * PEER_NBUF
            + [pltpu.SemaphoreType.DMA((PEER_NBUF,)),
               pltpu.SMEM((ROUTE_AHEAD_BUFS, tm, PEER_PICKS), jnp.int32),
               pltpu.VMEM((ROUTE_AHEAD_BUFS, tm, PEER_PICKS), F32),
               pltpu.VMEM((PEER_PICKS, tm), jnp.int32),
               pltpu.VMEM((PEER_PICKS, tm), F32),
               pltpu.VMEM((tm, PEER_PICKS), jnp.int32),
               pltpu.SemaphoreType.DMA((1,))]
            + _route_scratch(tm)),
        compiler_params=_cparams("arbitrary"),
        name="peer_experts",
    )(idx0, gates0, qp, sub_keys, x2, gain.reshape(1, d), sc, sh, g2, packed)


def _peer_ffn(x2, gain, sc, sh, g2, w_query, sub_keys, tables_u, tables_v, layer, seq):
    no_gain = jnp.ones((1, w_query.shape[1]), F32)
    qp = _norm_proj(x2, gain, sc, sh, w_query.astype(BF16), no_gain, (), seq)
    keys = sub_keys.astype(BF16)
    lead = (ROUTE_AHEAD_BUFS - 1) * ROUTE_TM
    idx0, gates0 = _peer_route(qp, keys, lead)
    idx0 = idx0.transpose(2, 0, 1).reshape(lead, PEER_PICKS)
    gates0 = gates0.transpose(2, 0, 1).reshape(lead, PEER_PICKS)
    packed = _pack_expert_tables(tables_u, tables_v, layer)
    return _peer_experts(x2, gain, sc, sh, g2, idx0, gates0, qp, keys, packed, seq)


def _tile_gain(gain, heads, scale=1.0):
    return jnp.tile(gain * scale, heads)


def _even_layer(x2, mods, norm_gain, w_in, b_forget, fox_qk_gain, diff_qk_gain, diff_lambda,
                diff_subln_gain, w_out, diff_bias, lam_init, bsz, seq):
    sh1, sc1, g1 = mods
    w_main = jnp.concatenate([w_in[:, :3 * FOX_W], w_in[:, 3 * FOX_W + FOX_HEADS:]], axis=1).astype(BF16)
    w_forget = jnp.zeros((D_MODEL, LANES), F32).at[:, :FOX_HEADS].set(
        w_in[:, 3 * FOX_W:3 * FOX_W + FOX_HEADS]).astype(BF16)
    ones = jnp.ones((FOX_W,), F32)
    head_gain = jnp.concatenate([
        _tile_gain(fox_qk_gain[0], FOX_HEADS, ATTN_SCALE), _tile_gain(fox_qk_gain[1], FOX_HEADS), ones,
        _tile_gain(diff_qk_gain[0], 2 * DIFF_HEADS, ATTN_SCALE), _tile_gain(diff_qk_gain[1], 2 * DIFF_HEADS),
        ones]).reshape(1, -1)
    proj, f_side = _norm_proj(x2, norm_gain, sc1, sh1, w_main, head_gain, (0, 1, 3, 4), seq,
                              w_side=w_forget)
    fcol, frow = _forget_cumsum(f_side, b_forget, bsz, seq)
    fox_o = _fox_attention(proj, frow, fcol, bsz, seq)
    bias_tiles = _causal_bias_tiles(diff_bias, seq, ATTN_T)
    diff_o = _diff_attention(proj, bias_tiles, diff_lambda, diff_subln_gain, lam_init, bsz, seq)
    return _out_proj([fox_o, diff_o], w_out.astype(BF16), x2, g1, seq)


def _odd_layer(x2, mods, norm_gain, w_qkv, qk_gain, w_out, bias_table, bsz, seq):
    sh1, sc1, g1 = mods
    ones = jnp.ones((D_MODEL,), F32)
    head_gain = jnp.concatenate([
        _tile_gain(qk_gain[0], DIL_HEADS, ATTN_SCALE), _tile_gain(qk_gain[1], DIL_HEADS), ones]).reshape(1, -1)
    qkv = _norm_proj(x2, norm_gain, sc1, sh1, w_qkv.astype(BF16), head_gain, (0, 1, 2, 3), seq)
    mixed = _dilated_attention(qkv, _dilated_bias_tiles(bias_table), bsz, seq)
    return _out_proj([mixed], w_out.astype(BF16), x2, g1, seq)


def kernel(x, c, rel_bias, norm_gain, w_ada, b_ada, even_w_in, even_b_forget, even_fox_qk_gain,
           even_diff_qk_gain, even_diff_lambda, even_diff_subln_gain, even_w_out, odd_w_qkv,
           odd_qk_gain, odd_w_out, peer_w_query, peer_sub_keys, peer_u, peer_v):
    bsz, seq, d = x.shape
    x2 = x.reshape(bsz * seq, d)
    mod = _adaln(c, w_ada, b_ada)
    for i in range(DEPTH):
        sh1, sc1, g1, sh2, sc2, g2 = [m.reshape(bsz, 1, d) for m in jnp.split(mod[i], 6, axis=-1)]
        j = i // 2
        if i % 2 == 0:
            lam_init = 0.8 - 0.6 * math.exp(-0.3 * i)
            x2 = _even_layer(x2, (sh1, sc1, g1), norm_gain[i, 0], even_w_in[j], even_b_forget[j],
                             even_fox_qk_gain[j], even_diff_qk_gain[j], even_diff_lambda[j],
                             even_diff_subln_gain[j], even_w_out[j], rel_bias[:, :DIFF_HEADS],
                             lam_init, bsz, seq)
        else:
            x2 = _odd_layer(x2, (sh1, sc1, g1), norm_gain[i, 0], odd_w_qkv[j], odd_qk_gain[j],
                            odd_w_out[j], rel_bias, bsz, seq)
        x2 = _peer_ffn(x2, norm_gain[i, 1], sc2, sh2, g2, peer_w_query[i], peer_sub_keys[i],
                       peer_u, peer_v, i, seq)
    return x2.reshape(bsz, seq, d)
```

```python
import functools
import math

import numpy as np
import jax
import jax.numpy as jnp
from jax import lax
from jax.experimental import pallas as pl
from jax.experimental.pallas import tpu as pltpu

F32 = jnp.float32
BF16 = jnp.bfloat16

D_MODEL = 2048
DEPTH = 2
HEAD_DIM = 128
FOX_HEADS = 8
DIFF_HEADS = 4
DIFF_V_DIM = 2 * HEAD_DIM
DIL_HEADS = D_MODEL // HEAD_DIM
DIL_PATTERNS = ((128, 1), (512, 4), (2048, 16))
DIL_BLOCK = 128
N_BUCKETS = 32
BUCKET_MAX_EXACT = 16
BUCKET_MAX_DIST = 2048
PEER_HEADS = 8
PEER_NKEYS = 128
PEER_EXPERTS = PEER_NKEYS * PEER_NKEYS
PEER_DKEY = 256
PEER_TOPK = 16
PEER_PICKS = PEER_HEADS * PEER_TOPK
NORM_EPS = 1e-6
FOX_W = FOX_HEADS * HEAD_DIM
DIFF_QK_W = DIFF_HEADS * 2 * HEAD_DIM
DIFF_V_W = DIFF_HEADS * DIFF_V_DIM
ATTN_SCALE = HEAD_DIM ** -0.5
MASKED = -1e30

LANES = 128
SUBLANES = 8
VMEM_LIMIT = 48 * 1024 * 1024

PROJ_TM = 512
PROJ_TN = 1024
ATTN_T = 512
ROUTE_TM = 128
PEER_TP = 16


def _cparams(*sem):
    return pltpu.CompilerParams(dimension_semantics=sem, vmem_limit_bytes=VMEM_LIMIT)


def _adaln_kernel(c_ref, w_ref, b_ref, o_ref):
    c = c_ref[...]
    cond = c * jax.nn.sigmoid(c)
    o_ref[...] = jnp.dot(cond.astype(BF16), w_ref[...].astype(BF16),
                         preferred_element_type=F32) + b_ref[...]


def _adaln(c, w_ada, b_ada):
    depth, d, n = w_ada.shape
    bsz = c.shape[0]
    tn = 1024
    return pl.pallas_call(
        _adaln_kernel,
        grid=(depth, n // tn),
        in_specs=[
            pl.BlockSpec((bsz, d), lambda i, j: (0, 0)),
            pl.BlockSpec((None, d, tn), lambda i, j: (i, 0, j)),
            pl.BlockSpec((None, 1, tn), lambda i, j: (i, 0, j)),
        ],
        out_specs=pl.BlockSpec((None, bsz, tn), lambda i, j: (i, 0, j)),
        out_shape=jax.ShapeDtypeStruct((depth, bsz, n), F32),
        compiler_params=_cparams("parallel", "parallel"),
        name="adaln",
    )(c, w_ada, b_ada.reshape(depth, 1, n))


def _modulated_norm(x, gain, sc, sh):
    ms = jnp.mean(x * x, axis=-1, keepdims=True)
    y = x * lax.rsqrt(ms + NORM_EPS) * gain
    return y * (1.0 + sc) + sh


def _norm_proj_kernel(*refs, norm_tiles, has_side):
    if has_side:
        x_ref, gain_ref, sc_ref, sh_ref, w_ref, hg_ref, ws_ref, o_ref, side_ref, h_ref = refs
    else:
        x_ref, gain_ref, sc_ref, sh_ref, w_ref, hg_ref, o_ref, h_ref = refs
    j = pl.program_id(1)

    @pl.when(j == 0)
    def _():
        h = _modulated_norm(x_ref[...], gain_ref[...], sc_ref[...], sh_ref[...])
        h_ref[...] = h.astype(BF16)
        if has_side:
            side_ref[...] = jnp.dot(h_ref[...], ws_ref[...], preferred_element_type=F32)

    acc = jnp.dot(h_ref[...], w_ref[...], preferred_element_type=F32)
    tn = acc.shape[1]

    def write_normed():
        for g in range(tn // HEAD_DIM):
            sl = slice(g * HEAD_DIM, (g + 1) * HEAD_DIM)
            blk = acc[:, sl]
            ms = jnp.mean(blk * blk, axis=-1, keepdims=True)
            o_ref[g] = (blk * lax.rsqrt(ms + NORM_EPS) * hg_ref[:, sl]).astype(o_ref.dtype)

    def write_raw():
        for g in range(tn // HEAD_DIM):
            o_ref[g] = acc[:, g * HEAD_DIM:(g + 1) * HEAD_DIM].astype(o_ref.dtype)

    if not norm_tiles:
        write_raw()
    else:
        is_norm = functools.reduce(jnp.logical_or, [j == t for t in norm_tiles])
        pl.when(is_norm)(write_normed)
        pl.when(jnp.logical_not(is_norm))(write_raw)


def _norm_proj(x2, gain, sc, sh, w, head_gain, norm_tiles, seq, w_side=None):
    t, d = x2.shape
    n = w.shape[1]
    tm, tn = PROJ_TM, PROJ_TN
    per_b = seq // tm
    has_side = w_side is not None
    gpt = tn // HEAD_DIM
    in_specs = [
        pl.BlockSpec((tm, d), lambda i, j: (i, 0)),
        pl.BlockSpec((1, d), lambda i, j: (0, 0)),
        pl.BlockSpec((None, 1, d), lambda i, j: (i // per_b, 0, 0)),
        pl.BlockSpec((None, 1, d), lambda i, j: (i // per_b, 0, 0)),
        pl.BlockSpec((d, tn), lambda i, j: (0, j)),
        pl.BlockSpec((1, tn), lambda i, j: (0, j)),
    ]
    args = [x2, gain.reshape(1, d), sc, sh, w, head_gain]
    out_specs = pl.BlockSpec((gpt, tm, HEAD_DIM), lambda i, j: (j, i, 0))
    out_shape = jax.ShapeDtypeStruct((n // HEAD_DIM, t, HEAD_DIM), BF16)
    if has_side:
        ns = w_side.shape[1]
        in_specs.append(pl.BlockSpec((d, ns), lambda i, j: (0, 0)))
        args.append(w_side)
        out_specs = [out_specs, pl.BlockSpec((tm, ns), lambda i, j: (i, 0))]
        out_shape = [out_shape, jax.ShapeDtypeStruct((t, ns), F32)]
    return pl.pallas_call(
        functools.partial(_norm_proj_kernel, norm_tiles=tuple(norm_tiles), has_side=has_side),
        grid=(t // tm, n // tn),
        in_specs=in_specs,
        out_specs=out_specs,
        out_shape=out_shape,
        scratch_shapes=[pltpu.VMEM((tm, d), BF16)],
        compiler_params=_cparams("parallel", "arbitrary"),
        name="norm_proj",
    )(*args)


def _out_proj_kernel(*refs):
    *a_refs, w_ref, x_ref, g_ref, o_ref = refs
    a = jnp.concatenate([a_ref[g] for a_ref in a_refs for g in range(a_ref.shape[0])], axis=1)
    y = jnp.dot(a, w_ref[...], preferred_element_type=F32)
    o_ref[...] = x_ref[...] + g_ref[...] * y


def _out_proj(heads_major, w, x2, g, seq):
    t = x2.shape[0]
    k, n = w.shape
    assert sum(a.shape[0] for a in heads_major) * HEAD_DIM == k
    tm, tn = PROJ_TM, PROJ_TN
    per_b = seq // tm
    return pl.pallas_call(
        _out_proj_kernel,
        grid=(t // tm, n // tn),
        in_specs=[pl.BlockSpec((a.shape[0], tm, HEAD_DIM), lambda i, j: (0, i, 0)) for a in heads_major] + [
            pl.BlockSpec((k, tn), lambda i, j: (0, j)),
            pl.BlockSpec((tm, tn), lambda i, j: (i, j)),
            pl.BlockSpec((None, 1, tn), lambda i, j: (i // per_b, 0, j)),
        ],
        out_specs=pl.BlockSpec((tm, tn), lambda i, j: (i, j)),
        out_shape=jax.ShapeDtypeStruct((t, n), F32),
        compiler_params=_cparams("parallel", "parallel"),
        name="out_proj",
    )(*heads_major, w, x2, g)


def _forget_kernel(f_ref, b_ref, col_ref, row_ref):
    z = f_ref[...] + b_ref[...]
    lf = jnp.minimum(z, 0.0) - jnp.log1p(jnp.exp(-jnp.abs(z)))
    s = lf.shape[0]
    pos = lax.broadcasted_iota(jnp.int32, lf.shape, 0)
    d = 1
    while d < s:
        lf = lf + jnp.where(pos >= d, pltpu.roll(lf, d, axis=0), 0.0)
        d *= 2
    col_ref[...] = lf
    row_ref[...] = lf.T[:FOX_HEADS, :]


def _forget_cumsum(f_side, b_forget, bsz, seq):
    bpad = jnp.zeros((1, LANES), F32).at[0, :FOX_HEADS].set(b_forget)
    col, row = pl.pallas_call(
        _forget_kernel,
        grid=(bsz,),
        in_specs=[
            pl.BlockSpec((None, seq, LANES), lambda b: (b, 0, 0)),
            pl.BlockSpec((1, LANES), lambda b: (0, 0)),
        ],
        out_specs=[
            pl.BlockSpec((None, seq, LANES), lambda b: (b, 0, 0)),
            pl.BlockSpec((None, FOX_HEADS, seq), lambda b: (b, 0, 0)),
        ],
        out_shape=[
            jax.ShapeDtypeStruct((bsz, seq, LANES), F32),
            jax.ShapeDtypeStruct((bsz, FOX_HEADS, seq), F32),
        ],
        compiler_params=_cparams("parallel"),
        name="forget_cumsum",
    )(f_side.reshape(bsz, seq, LANES), bpad)
    return col, row.reshape(bsz, FOX_HEADS, 1, seq)


def _online_softmax_step(s, v, m_ref, l_ref, acc_ref):
    m_prev = m_ref[...]
    m_new = jnp.maximum(m_prev, jnp.max(s, axis=-1, keepdims=True))
    alpha = jnp.exp(m_prev - m_new)
    p = jnp.exp(s - m_new)
    l_ref[...] = alpha * l_ref[...] + jnp.sum(p, axis=-1, keepdims=True)
    acc_ref[...] = alpha * acc_ref[...] + jnp.dot(p.astype(v.dtype), v, preferred_element_type=F32)
    m_ref[...] = m_new


def _fox_kernel(q_ref, k_ref, v_ref, frow_ref, fcol_ref, o_ref, m_ref, l_ref, acc_ref):
    h = pl.program_id(1)
    qi = pl.program_id(2)
    tt = q_ref.shape[0]
    m_ref[...] = jnp.full_like(m_ref, MASKED)
    l_ref[...] = jnp.zeros_like(l_ref)
    acc_ref[...] = jnp.zeros_like(acc_ref)
    lane = lax.broadcasted_iota(jnp.int32, fcol_ref.shape, 1)
    fq = jnp.sum(jnp.where(lane == h, fcol_ref[...], 0.0), axis=-1, keepdims=True)
    q = q_ref[...]

    def kv_block(j, diagonal):
        rows = pl.ds(pl.multiple_of(j * tt, tt), tt)
        s = lax.dot_general(q, k_ref[rows, :], (((1,), (1,)), ((), ())), preferred_element_type=F32)
        s = s + fq - frow_ref[j]
        if diagonal:
            keep = (lax.broadcasted_iota(jnp.int32, s.shape, 0)
                    >= lax.broadcasted_iota(jnp.int32, s.shape, 1))
            s = jnp.where(keep, s, MASKED)
        _online_softmax_step(s, v_ref[rows, :], m_ref, l_ref, acc_ref)

    def below_diagonal(j, carry):
        kv_block(j, False)
        return carry

    lax.fori_loop(0, qi, below_diagonal, 0)
    kv_block(qi, True)
    o_ref[...] = (acc_ref[...] / l_ref[...]).astype(o_ref.dtype)


def _fox_attention(proj, frow, fcol, bsz, seq):
    t = proj.shape[1]
    tt = ATTN_T
    nb = seq // tt
    kcol = FOX_W // HEAD_DIM
    blk = (None, tt, HEAD_DIM)
    whole = (None, seq, HEAD_DIM)
    return pl.pallas_call(
        _fox_kernel,
        grid=(bsz, FOX_HEADS, nb),
        in_specs=[
            pl.BlockSpec(blk, lambda b, h, qi: (h, b * nb + qi, 0)),
            pl.BlockSpec(whole, lambda b, h, qi: (kcol + h, b, 0)),
            pl.BlockSpec(whole, lambda b, h, qi: (2 * kcol + h, b, 0)),
            pl.BlockSpec((None, None, nb, 1, tt), lambda b, h, qi: (b, h, 0, 0, 0)),
            pl.BlockSpec((None, tt, LANES), lambda b, h, qi: (b, qi, 0)),
        ],
        out_specs=pl.BlockSpec(blk, lambda b, h, qi: (h, b * nb + qi, 0)),
        out_shape=jax.ShapeDtypeStruct((FOX_HEADS, t, HEAD_DIM), BF16),
        scratch_shapes=[
            pltpu.VMEM((tt, 1), F32), pltpu.VMEM((tt, 1), F32), pltpu.VMEM((tt, HEAD_DIM), F32),
        ],
        compiler_params=_cparams("parallel", "parallel", "parallel"),
        name="fox_attention",
    )(proj, proj, proj, frow.reshape(bsz, FOX_HEADS, nb, 1, tt), fcol)


def _diff_kernel(q_ref, k_ref, v_ref, bias_ref, lam_ref, sg_ref, o_ref,
                 m0, l0, a0, m1, l1, a1, *, lam_init):
    qi = pl.program_id(2)
    tt = q_ref.shape[1]
    for m_ref, l_ref, a_ref in ((m0, l0, a0), (m1, l1, a1)):
        m_ref[...] = jnp.full_like(m_ref, MASKED)
        l_ref[...] = jnp.zeros_like(l_ref)
        a_ref[...] = jnp.zeros_like(a_ref)

    def kv_block(j, carry):
        rows = pl.ds(pl.multiple_of(j * tt, tt), tt)
        bias = bias_ref[qi - j]
        v = jnp.concatenate([v_ref[0, rows, :], v_ref[1, rows, :]], axis=1)
        for mi, (m_ref, l_ref, a_ref) in enumerate(((m0, l0, a0), (m1, l1, a1))):
            s = lax.dot_general(q_ref[mi], k_ref[mi, rows, :], (((1,), (1,)), ((), ())),
                                preferred_element_type=F32) + bias
            _online_softmax_step(s, v, m_ref, l_ref, a_ref)
        return carry

    lax.fori_loop(0, qi + 1, kv_block, 0)

    lf = lam_ref[...]
    lam = (jnp.exp(jnp.sum(lf[0:1] * lf[1:2], axis=-1, keepdims=True))
           - jnp.exp(jnp.sum(lf[2:3] * lf[3:4], axis=-1, keepdims=True)) + lam_init)
    o = a0[...] / l0[...] - lam * (a1[...] / l1[...])
    ms = jnp.mean(o * o, axis=-1, keepdims=True)
    o = (o * lax.rsqrt(ms + NORM_EPS) * sg_ref[...] * (1.0 - lam_init)).astype(o_ref.dtype)
    o_ref[0] = o[:, :HEAD_DIM]
    o_ref[1] = o[:, HEAD_DIM:]


def _diff_attention(proj, bias_tiles, lam_param, subln_gain, lam_init, bsz, seq):
    t = proj.shape[1]
    tt = ATTN_T
    nb = seq // tt
    qcol = 3 * FOX_W // DIFF_V_DIM
    kcol = qcol + DIFF_QK_W // DIFF_V_DIM
    vcol = kcol + DIFF_QK_W // DIFF_V_DIM
    blk = (2, tt, HEAD_DIM)
    whole = (2, seq, HEAD_DIM)
    return pl.pallas_call(
        functools.partial(_diff_kernel, lam_init=lam_init),
        grid=(bsz, DIFF_HEADS, nb),
        in_specs=[
            pl.BlockSpec(blk, lambda b, h, qi: (qcol + h, b * nb + qi, 0)),
            pl.BlockSpec(whole, lambda b, h, qi: (kcol + h, b, 0)),
            pl.BlockSpec(whole, lambda b, h, qi: (vcol + h, b, 0)),
            pl.BlockSpec((None, nb, tt, tt), lambda b, h, qi: (h, 0, 0, 0)),
            pl.BlockSpec((4, HEAD_DIM), lambda b, h, qi: (0, 0)),
            pl.BlockSpec((1, DIFF_V_DIM), lambda b, h, qi: (0, 0)),
        ],
        out_specs=pl.BlockSpec(blk, lambda b, h, qi: (h, b * nb + qi, 0)),
        out_shape=jax.ShapeDtypeStruct((2 * DIFF_HEADS, t, HEAD_DIM), BF16),
        scratch_shapes=[
            pltpu.VMEM((tt, 1), F32), pltpu.VMEM((tt, 1), F32), pltpu.VMEM((tt, DIFF_V_DIM), F32),
            pltpu.VMEM((tt, 1), F32), pltpu.VMEM((tt, 1), F32), pltpu.VMEM((tt, DIFF_V_DIM), F32),
        ],
        compiler_params=_cparams("parallel", "parallel", "parallel"),
        name="diff_attention",
    )(proj, proj, proj, bias_tiles, lam_param, subln_gain.reshape(1, DIFF_V_DIM))


def _t5_bucket_np(dist):
    n = np.maximum(dist, 0)
    nf = np.maximum(n, 1).astype(np.float32)
    large = BUCKET_MAX_EXACT + (np.log(nf / np.float32(BUCKET_MAX_EXACT))
                                / np.float32(math.log(BUCKET_MAX_DIST / BUCKET_MAX_EXACT))
                                * np.float32(N_BUCKETS - BUCKET_MAX_EXACT)).astype(np.int32)
    large = np.minimum(large, N_BUCKETS - 1)
    return np.where(n < BUCKET_MAX_EXACT, n, large).astype(np.int32)


def _bucket_thresholds():
    buckets = _t5_bucket_np(np.arange(BUCKET_MAX_DIST + 1))
    assert np.all(np.diff(buckets) >= 0)
    return [int(np.argmax(buckets >= b)) for b in range(N_BUCKETS)]


_BUCKET_THRESHOLDS = _bucket_thresholds()


def _bias_from_distance(dist, valid, table_ref, h):
    val = jnp.full(dist.shape, table_ref[0, h], F32)
    for b in range(1, N_BUCKETS):
        val = jnp.where(dist >= _BUCKET_THRESHOLDS[b], table_ref[b, h], val)
    return jnp.where(valid, val, MASKED)


def _causal_bias_kernel(table_ref, o_ref):
    h, delta = pl.program_id(0), pl.program_id(1)
    tt = o_ref.shape[0]
    dist = (delta * tt + lax.broadcasted_iota(jnp.int32, o_ref.shape, 0)
            - lax.broadcasted_iota(jnp.int32, o_ref.shape, 1))
    o_ref[...] = _bias_from_distance(dist, dist >= 0, table_ref, h)


def _causal_bias_tiles(table, seq, tt):
    nb = seq // tt
    heads = table.shape[1]
    return pl.pallas_call(
        _causal_bias_kernel,
        grid=(heads, nb),
        in_specs=[pl.BlockSpec(memory_space=pltpu.SMEM)],
        out_specs=pl.BlockSpec((None, None, tt, tt), lambda h, dlt: (h, dlt, 0, 0)),
        out_shape=jax.ShapeDtypeStruct((heads, nb, tt, tt), F32),
        compiler_params=_cparams("parallel", "parallel"),
        name="causal_bias_tiles",
    )(table)


def _dilated_bias_kernel(table_ref, o_ref):
    h = pl.program_id(0)
    span = DIL_BLOCK
    shape = o_ref.shape[1:]
    step = lax.broadcasted_iota(jnp.int32, shape, 0) + span - lax.broadcasted_iota(jnp.int32, shape, 1)
    valid = jnp.logical_and(step >= 0, step <= span)
    for p, (_, dil) in enumerate(DIL_PATTERNS):
        o_ref[p] = _bias_from_distance(step * dil, valid, table_ref, h)


def _dilated_bias_tiles(table):
    heads = table.shape[1]
    npat = len(DIL_PATTERNS)
    return pl.pallas_call(
        _dilated_bias_kernel,
        grid=(heads,),
        in_specs=[pl.BlockSpec(memory_space=pltpu.SMEM)],
        out_specs=pl.BlockSpec((npat, None, DIL_BLOCK, 2 * DIL_BLOCK), lambda h: (0, h, 0, 0)),
        out_shape=jax.ShapeDtypeStruct((npat, heads, DIL_BLOCK, 2 * DIL_BLOCK), F32),
        compiler_params=_cparams("parallel"),
        name="dilated_bias_tiles",
    )(table)


DIL_GROUP = 8


def _dilated_kernel(q_ref, k_ref, v_ref, bias_ref, o_ref, qf, kf, vf, o_scr, lse_scr):
    seq = q_ref.shape[0]
    blk = DIL_BLOCK
    qf[...] = q_ref[...].astype(F32)
    for src, dst in ((k_ref, kf), (v_ref, vf)):
        dst[0:blk, :] = jnp.zeros((blk, HEAD_DIM), F32)
        dst[blk:, :] = src[...].astype(F32)
    dn_qk = (((2,), (2,)), ((0,), (0,)))
    dn_pv = (((2,), (1,)), ((0,), (0,)))

    def rows(ref, start, dil):
        if dil == 1:
            return ref[pl.ds(start, blk), :]
        return ref[pl.ds(start, blk, stride=dil), :]

    for p, (_, dil) in enumerate(DIL_PATTERNS):
        nblk = seq // dil // blk
        blocks = [(jb, r) for jb in range(nblk) for r in range(dil)]
        bias = bias_ref[p]
        for g0 in range(0, len(blocks), DIL_GROUP):
            group = blocks[g0:g0 + DIL_GROUP]
            starts = [jb * blk * dil + r for jb, r in group]
            has_prev = [jb > 0 for jb, _ in group]
            q = jnp.stack([rows(qf, st, dil) for st in starts]).astype(BF16)
            kc = jnp.stack([rows(kf, blk + st, dil) for st in starts]).astype(BF16)
            vc = jnp.stack([rows(vf, blk + st, dil) for st in starts]).astype(BF16)
            s_cur = lax.dot_general(q, kc, dn_qk, preferred_element_type=F32) + bias[:, blk:]
            m = jnp.max(s_cur, axis=-1, keepdims=True)
            if any(has_prev):
                prev_starts = [max(blk + st - blk * dil, 0) for st in starts]
                kp = jnp.stack([rows(kf, st, dil) for st in prev_starts]).astype(BF16)
                vp = jnp.stack([rows(vf, st, dil) for st in prev_starts]).astype(BF16)
                s_prev = lax.dot_general(q, kp, dn_qk, preferred_element_type=F32) + bias[:, :blk]
                which = lax.broadcasted_iota(jnp.int32, s_prev.shape, 0)
                for gi, ok in enumerate(has_prev):
                    if not ok:
                        s_prev = jnp.where(which == gi, MASKED, s_prev)
                m = jnp.maximum(m, jnp.max(s_prev, axis=-1, keepdims=True))
                p_prev = jnp.exp(s_prev - m)
            p_cur = jnp.exp(s_cur - m)
            l = jnp.sum(p_cur, axis=-1, keepdims=True)
            acc = lax.dot_general(p_cur.astype(BF16), vc, dn_pv, preferred_element_type=F32)
            if any(has_prev):
                l = l + jnp.sum(p_prev, axis=-1, keepdims=True)
                acc = acc + lax.dot_general(p_prev.astype(BF16), vp, dn_pv, preferred_element_type=F32)
            out = acc / l
            lse = jnp.broadcast_to(m + jnp.log(l), out.shape)
            for gi, st in enumerate(starts):
                if dil == 1:
                    o_scr[p, pl.ds(st, blk), :] = out[gi]
                    lse_scr[p, pl.ds(st, blk), :] = lse[gi]
                else:
                    o_scr[p, pl.ds(st, blk, stride=dil), :] = out[gi]
                    lse_scr[p, pl.ds(st, blk, stride=dil), :] = lse[gi]

    lses = [lse_scr[p] for p in range(len(DIL_PATTERNS))]
    top = functools.reduce(jnp.maximum, lses)
    es = [jnp.exp(x - top) for x in lses]
    den = functools.reduce(lambda a, b: a + b, es)
    mixed = functools.reduce(lambda a, b: a + b, [(e / den) * o_scr[p] for p, e in enumerate(es)])
    o_ref[...] = mixed.astype(o_ref.dtype)


def _dilated_attention(qkv, bias_tiles, bsz, seq):
    t = qkv.shape[1]
    h16 = DIL_HEADS
    npat = len(DIL_PATTERNS)
    blk = (None, seq, HEAD_DIM)
    return pl.pallas_call(
        _dilated_kernel,
        grid=(bsz, DIL_HEADS),
        in_specs=[
            pl.BlockSpec(blk, lambda b, h: (h, b, 0)),
            pl.BlockSpec(blk, lambda b, h: (h16 + h, b, 0)),
            pl.BlockSpec(blk, lambda b, h: (2 * h16 + h, b, 0)),
            pl.BlockSpec((npat, None, DIL_BLOCK, 2 * DIL_BLOCK), lambda b, h: (0, h, 0, 0)),
        ],
        out_specs=pl.BlockSpec(blk, lambda b, h: (h, b, 0)),
        out_shape=jax.ShapeDtypeStruct((DIL_HEADS, t, HEAD_DIM), BF16),
        scratch_shapes=[
            pltpu.VMEM((seq, HEAD_DIM), F32),
            pltpu.VMEM((DIL_BLOCK + seq, HEAD_DIM), F32),
            pltpu.VMEM((DIL_BLOCK + seq, HEAD_DIM), F32),
            pltpu.VMEM((npat, seq, HEAD_DIM), F32),
            pltpu.VMEM((npat, seq, HEAD_DIM), F32),
        ],
        compiler_params=_cparams("parallel", "parallel"),
        name="dilated_attention",
    )(qkv, qkv, qkv, bias_tiles)


def _topk_rows(s, val_ref, pick_ref, payload=None):
    neg_rows = -lax.broadcasted_iota(jnp.int32, s.shape, 0).astype(F32)
    for k in range(PEER_TOPK):
        m = jnp.max(s, axis=0, keepdims=True)
        am = jnp.max(jnp.where(s == m, neg_rows, -jnp.inf), axis=0, keepdims=True)
        hit = neg_rows == am
        val_ref[k:k + 1, :] = m
        if payload is None:
            pick_ref[k:k + 1, :] = (-am).astype(jnp.int32)
        else:
            pick_ref[k:k + 1, :] = jnp.sum(jnp.where(hit, payload, 0), axis=0, keepdims=True)
        s = jnp.where(hit, -jnp.inf, s)


def _candidate_rows(a):
    need = PEER_TOPK // (a + 1)
    if need > SUBLANES:
        return PEER_TOPK
    return SUBLANES if need > 1 else 1


def _route_kernel(q_ref, keys_ref, idx_ref, gate_ref, v0, i0, v1, i1, best_ref):
    _route_half(q_ref[0], keys_ref[0], v0, i0)
    _route_half(q_ref[1], keys_ref[1], v1, i1)
    _route_combine(idx_ref, gate_ref, v0, i0, v1, i1, best_ref)


def _route_half(q, keys, val_ref, idx_ref):
    scores = lax.dot_general(keys, q, (((1,), (1,)), ((), ())), preferred_element_type=F32)
    _topk_rows(scores, val_ref, idx_ref)


def _route_combine(idx_ref, gate_ref, v0, i0, v1, i1, best_ref):
    first_single = min(a for a in range(PEER_TOPK) if _candidate_rows(a) == 1)
    cand_s, cand_i = [], []
    for a in range(first_single):
        nb = _candidate_rows(a)
        cand_s.append(v0[a:a + 1, :] + v1[0:nb, :])
        cand_i.append(i0[a:a + 1, :] * PEER_NKEYS + i1[0:nb, :])
    cand_s.append(v0[first_single:, :] + v1[0:1, :])
    cand_i.append(i0[first_single:, :] * PEER_NKEYS + i1[0:1, :])
    _topk_rows(jnp.concatenate(cand_s, axis=0), best_ref, idx_ref, jnp.concatenate(cand_i, axis=0))
    best_s = best_ref[...]
    e = jnp.exp(best_s - best_s[0:1, :])
    gate_ref[...] = e / jnp.sum(e, axis=0, keepdims=True)


def _route_scratch(tm):
    return [pltpu.VMEM((PEER_TOPK, tm), F32), pltpu.VMEM((PEER_TOPK, tm), jnp.int32),
            pltpu.VMEM((PEER_TOPK, tm), F32), pltpu.VMEM((PEER_TOPK, tm), jnp.int32),
            pltpu.VMEM((PEER_TOPK, tm), F32)]


def _peer_route(qp, sub_keys, t):
    tm = ROUTE_TM
    out_spec = pl.BlockSpec((None, PEER_TOPK, tm), lambda i, h: (h, 0, i))
    return pl.pallas_call(
        _route_kernel,
        grid=(t // tm, PEER_HEADS),
        in_specs=[
            pl.BlockSpec((2, tm, PEER_DKEY // 2), lambda i, h: (h, i, 0)),
            pl.BlockSpec((2, PEER_NKEYS, PEER_DKEY // 2), lambda i, h: (0, 0, 0)),
        ],
        out_specs=[out_spec, out_spec],
        out_shape=[
            jax.ShapeDtypeStruct((PEER_HEADS, PEER_TOPK, t), jnp.int32),
            jax.ShapeDtypeStruct((PEER_HEADS, PEER_TOPK, t), F32),
        ],
        scratch_shapes=_route_scratch(tm),
        compiler_params=_cparams("parallel", "parallel"),
        name="peer_route",
    )(qp, sub_keys)


def _gelu_exact(a):
    return 0.5 * a * (1.0 + lax.erf(a * (2.0 ** -0.5)))


def _pack_expert_tables(tables_u, tables_v, layer):
    _, e, d = tables_u.shape
    te = PACK_ROWS
    n_steps = e // te
    layer_rows = pl.BlockSpec((None, te, d), lambda i: (layer, i, 0))
    return pl.pallas_call(
        _pack_kernel,
        grid=(n_steps,),
        in_specs=[layer_rows, layer_rows],
        out_specs=pl.BlockSpec(memory_space=pl.ANY),
        out_shape=jax.ShapeDtypeStruct((e, 1, d), jnp.uint32),
        scratch_shapes=[pltpu.VMEM((2, te, d), jnp.uint32), pltpu.SemaphoreType.DMA((2,))],
        compiler_params=_cparams("arbitrary"),
        name="pack_expert_tables",
    )(tables_u, tables_v)


PACK_ROWS = 128


def _pack_kernel(u_ref, v_ref, out_hbm, stage, sems):
    i = pl.program_id(0)
    n_steps = pl.num_programs(0)
    te, d = u_ref.shape
    half = d // 2
    slot = i % 2

    def bf16_high_bits(x):
        return lax.bitcast_convert_type(x.astype(BF16).astype(F32), jnp.uint32)

    def pack(ref):
        return bf16_high_bits(ref[:, half:]) | (bf16_high_bits(ref[:, :half]) >> 16)

    def wait_slot(sl):
        pltpu.make_async_copy(stage.at[sl], stage.at[sl], sems.at[sl]).wait()

    @pl.when(i >= 2)
    def _():
        wait_slot(slot)

    stage[slot] = jnp.concatenate([pack(u_ref), pack(v_ref)], axis=1)
    for r in range(te):
        pltpu.make_async_copy(stage.at[slot, pl.ds(r, 1)], out_hbm.at[i * te + r],
                              sems.at[slot]).start(priority=r % 2)

    @pl.when(i == n_steps - 1)
    def _():
        wait_slot(slot)

        @pl.when(n_steps > 1)
        def _():
            wait_slot(1 - slot)


def _unpack_words(w):
    lo = lax.bitcast_convert_type(w << 16, F32)
    hi = lax.bitcast_convert_type(w & jnp.uint32(0xFFFF0000), F32)
    return lo, hi


PEER_NBUF = 4
PEER_LOOKAHEAD = 2
ROUTE_AHEAD_BUFS = 3


def _expert_kernel(idx0_ref, gates0_ref, qnext_ref, keys_ref, x_ref, gain_ref, sc_ref, sh_ref, g2_ref,
                   tab_hbm, o_ref, *scratch):
    bufs, sems = scratch[:PEER_NBUF], scratch[PEER_NBUF]
    (idx_smem, gates_v, idx_t, gate_t, stage, route_sem), route_scratch = (
        scratch[PEER_NBUF + 1:PEER_NBUF + 7], scratch[PEER_NBUF + 7:])
    s = pl.program_id(0)
    n_steps = pl.num_programs(0)
    tp = x_ref.shape[0]
    gt = tp // PEER_NBUF
    d = x_ref.shape[1]
    ngroup = PEER_PICKS // SUBLANES
    nword = d // 2 // LANES
    spt = ROUTE_TM // tp
    r = s % spt
    par = (s // spt) % ROUTE_AHEAD_BUFS
    gain, sc, sh, g2 = gain_ref[...], sc_ref[...], sh_ref[...], g2_ref[...]
    eye = (lax.broadcasted_iota(jnp.int32, (PEER_PICKS, PEER_PICKS), 0)
           == lax.broadcasted_iota(jnp.int32, (PEER_PICKS, PEER_PICKS), 1))

    def park_routing(idx_rows, to_par):
        cp = pltpu.make_async_copy(idx_rows, idx_smem.at[to_par], route_sem.at[0])
        cp.start()
        cp.wait()

    @pl.when(s == 0)
    def _():
        for tile in range(ROUTE_AHEAD_BUFS - 1):
            park_routing(idx0_ref.at[pl.ds(tile * ROUTE_TM, ROUTE_TM)], tile)
            gates_v[tile] = gates0_ref[pl.ds(tile * ROUTE_TM, ROUTE_TM), :]

    head_rows = pl.ds(pl.multiple_of(r * PEER_TOPK, PEER_TOPK), PEER_TOPK)
    rv0, ri0, rv1, ri1, rbest = route_scratch
    route_stages = (
        lambda: _route_half(qnext_ref[2 * r], keys_ref[0], rv0, ri0),
        lambda: _route_half(qnext_ref[2 * r + 1], keys_ref[1], rv1, ri1),
        lambda: _route_combine(idx_t.at[head_rows], gate_t.at[head_rows], rv0, ri0, rv1, ri1, rbest),
    )

    def start_token(tile_par, row, j, ti):
        for g in range(ngroup):
            for k in range(SUBLANES):
                e = idx_smem[tile_par, row, g * SUBLANES + k]
                pltpu.make_async_copy(
                    tab_hbm.at[e],
                    bufs[j].at[pl.ds((ti * ngroup + g) * SUBLANES + k, 1)],
                    sems.at[j]).start(priority=k % 2)

    def wait_group(j):
        pltpu.make_async_copy(bufs[j], bufs[j], sems.at[j]).wait()

    def gated_activation(buf, ti, tok):
        h = _modulated_norm(x_ref[pl.ds(tok, 1), :], gain, sc, sh)
        hb = jnp.broadcast_to(h, (SUBLANES, d))
        parts = []
        for g in range(ngroup):
            acc = None
            for c in range(nword):
                lo, hi = _unpack_words(buf[pl.ds((ti * ngroup + g) * SUBLANES, SUBLANES),
                                           c * LANES:(c + 1) * LANES])
                term = (lo * hb[:, c * LANES:(c + 1) * LANES]
                        + hi * hb[:, (nword + c) * LANES:(nword + c + 1) * LANES])
                acc = term if acc is None else acc + term
            parts.append(acc)
        act = jnp.sum(jnp.concatenate(parts, axis=0), axis=-1, keepdims=True)
        gate_row = gates_v[par, pl.ds(r * tp + tok, 1), :]
        gate_col = jnp.sum(jnp.where(eye, gate_row, 0.0), axis=-1, keepdims=True)
        return gate_col * _gelu_exact(act)

    def weighted_values(buf, ti, tok, a):
        ylo = [None] * nword
        yhi = [None] * nword
        for g in range(ngroup):
            ag = a[g * SUBLANES:(g + 1) * SUBLANES, :]
            for c in range(nword):
                lo, hi = _unpack_words(buf[pl.ds((ti * ngroup + g) * SUBLANES, SUBLANES),
                                           (nword + c) * LANES:(nword + c + 1) * LANES])
                ylo[c] = lo * ag if ylo[c] is None else ylo[c] + lo * ag
                yhi[c] = hi * ag if yhi[c] is None else yhi[c] + hi * ag
        y = jnp.sum(jnp.concatenate(ylo + yhi, axis=1), axis=0, keepdims=True)
        o_ref[pl.ds(tok, 1), :] = x_ref[pl.ds(tok, 1), :] + g2 * y

    @pl.when(s == 0)
    def _():
        for j in range(PEER_LOOKAHEAD):
            def body(ti, carry, j=j):
                start_token(0, j * gt + ti, j, ti)
                return carry
            lax.fori_loop(0, gt, body, 0)

    last_of_tile = r == spt - 1
    next_par = jnp.where(last_of_tile, (par + 1) % ROUTE_AHEAD_BUFS, par)
    next_row = jnp.where(last_of_tile, 0, (r + 1) * tp)

    pending = None
    for j in range(PEER_NBUF):
        wait_group(j)
        if j < len(route_stages):
            route_stages[j]()
        ahead = j + PEER_LOOKAHEAD
        for ti in range(gt):
            tok = j * gt + ti
            if ahead < PEER_NBUF:
                start_token(par, r * tp + ahead * gt + ti, ahead, ti)
            else:
                start_token(next_par, next_row + (ahead - PEER_NBUF) * gt + ti, ahead - PEER_NBUF, ti)
            a = gated_activation(bufs[j], ti, tok)
            if pending is not None:
                weighted_values(*pending)
            pending = (bufs[j], ti, tok, a)
    weighted_values(*pending)

    @pl.when(last_of_tile)
    def _():
        free_par = (par + ROUTE_AHEAD_BUFS - 1) % ROUTE_AHEAD_BUFS
        stage[...] = idx_t[...].T
        gates_v[free_par] = gate_t[...].T
        park_routing(stage, free_par)

    @pl.when(s == n_steps - 1)
    def _():
        for j in range(PEER_LOOKAHEAD):
            wait_group(j)


def _peer_experts(x2, gain, sc, sh, g2, idx0, gates0, qp, sub_keys, packed, seq):
    t, d = x2.shape
    tp = PEER_TP
    tm = ROUTE_TM
    ahead = ROUTE_AHEAD_BUFS - 1
    assert tm == PEER_HEADS * tp
    n_steps = t // tp
    n_tiles = t // tm
    spt = tm // tp
    per_b = seq // tp
    whole = lambda s: (0, 0)
    return pl.pallas_call(
        _expert_kernel,
        grid=(n_steps,),
        in_specs=[
            pl.BlockSpec((ahead * tm, PEER_PICKS), whole),
            pl.BlockSpec((ahead * tm, PEER_PICKS), whole),
            pl.BlockSpec((2 * PEER_HEADS, tm, PEER_DKEY // 2),
                         lambda s: (0, jnp.minimum(s // spt + ahead, n_tiles - 1), 0)),
            pl.BlockSpec((2, PEER_NKEYS, PEER_DKEY // 2), lambda s: (0, 0, 0)),
            pl.BlockSpec((tp, d), lambda s: (s, 0)),
            pl.BlockSpec((1, d), lambda s: (0, 0)),
            pl.BlockSpec((None, 1, d), lambda s: (s // per_b, 0, 0)),
            pl.BlockSpec((None, 1, d), lambda s: (s // per_b, 0, 0)),
            pl.BlockSpec((None, 1, d), lambda s: (s // per_b, 0, 0)),
            pl.BlockSpec(memory_space=pl.ANY),
        ],
        out_specs=pl.BlockSpec((tp, d), lambda s: (s, 0)),
        out_shape=jax.ShapeDtypeStruct((t, d), F32),
        scratch_shapes=(
            [pltpu.VMEM((tp // PEER_NBUF * PEER_PICKS, d), jnp.uint32)] * PEER_NBUF
            + [pltpu.SemaphoreType.DMA((PEER_NBUF,)),
               pltpu.SMEM((ROUTE_AHEAD_BUFS, tm, PEER_PICKS), jnp.int32),
               pltpu.VMEM((ROUTE_AHEAD_BUFS, tm, PEER_PICKS), F32),
               pltpu.VMEM((PEER_PICKS, tm), jnp.int32),
               pltpu.VMEM((PEER_PICKS, tm), F32),
               pltpu.VMEM((tm, PEER_PICKS), jnp.int32),
               pltpu.SemaphoreType.DMA((1,))]
            + _route_scratch(tm)),
        compiler_params=_cparams("arbitrary"),
        name="peer_experts",
    )(idx0, gates0, qp, sub_keys, x2, gain.reshape(1, d), sc, sh, g2, packed)


def _peer_ffn(x2, gain, sc, sh, g2, w_query, sub_keys, tables_u, tables_v, layer, seq):
    no_gain = jnp.ones((1, w_query.shape[1]), F32)
    qp = _norm_proj(x2, gain, sc, sh, w_query.astype(BF16), no_gain, (), seq)
    keys = sub_keys.astype(BF16)
    lead = (ROUTE_AHEAD_BUFS - 1) * ROUTE_TM
    idx0, gates0 = _peer_route(qp, keys, lead)
    idx0 = idx0.transpose(2, 0, 1).reshape(lead, PEER_PICKS)
    gates0 = gates0.transpose(2, 0, 1).reshape(lead, PEER_PICKS)
    packed = _pack_expert_tables(tables_u, tables_v, layer)
    return _peer_experts(x2, gain, sc, sh, g2, idx0, gates0, qp, keys, packed, seq)


def _tile_gain(gain, heads, scale=1.0):
    return jnp.tile(gain * scale, heads)


def _even_layer(x2, mods, norm_gain, w_in, b_forget, fox_qk_gain, diff_qk_gain, diff_lambda,
                diff_subln_gain, w_out, diff_bias, lam_init, bsz, seq):
    sh1, sc1, g1 = mods
    w_main = jnp.concatenate([w_in[:, :3 * FOX_W], w_in[:, 3 * FOX_W + FOX_HEADS:]], axis=1).astype(BF16)
    w_forget = jnp.zeros((D_MODEL, LANES), F32).at[:, :FOX_HEADS].set(
        w_in[:, 3 * FOX_W:3 * FOX_W + FOX_HEADS]).astype(BF16)
    ones = jnp.ones((FOX_W,), F32)
    head_gain = jnp.concatenate([
        _tile_gain(fox_qk_gain[0], FOX_HEADS, ATTN_SCALE), _tile_gain(fox_qk_gain[1], FOX_HEADS), ones,
        _tile_gain(diff_qk_gain[0], 2 * DIFF_HEADS, ATTN_SCALE), _tile_gain(diff_qk_gain[1], 2 * DIFF_HEADS),
        ones]).reshape(1, -1)
    proj, f_side = _norm_proj(x2, norm_gain, sc1, sh1, w_main, head_gain, (0, 1, 3, 4), seq,
                              w_side=w_forget)
    fcol, frow = _forget_cumsum(f_side, b_forget, bsz, seq)
    fox_o = _fox_attention(proj, frow, fcol, bsz, seq)
    bias_tiles = _causal_bias_tiles(diff_bias, seq, ATTN_T)
    diff_o = _diff_attention(proj, bias_tiles, diff_lambda, diff_subln_gain, lam_init, bsz, seq)
    return _out_proj([fox_o, diff_o], w_out.astype(BF16), x2, g1, seq)


def _odd_layer(x2, mods, norm_gain, w_qkv, qk_gain, w_out, bias_table, bsz, seq):
    sh1, sc1, g1 = mods
    ones = jnp.ones((D_MODEL,), F32)
    head_gain = jnp.concatenate([
        _tile_gain(qk_gain[0], DIL_HEADS, ATTN_SCALE), _tile_gain(qk_gain[1], DIL_HEADS), ones]).reshape(1, -1)
    qkv = _norm_proj(x2, norm_gain, sc1, sh1, w_qkv.astype(BF16), head_gain, (0, 1, 2, 3), seq)
    mixed = _dilated_attention(qkv, _dilated_bias_tiles(bias_table), bsz, seq)
    return _out_proj([mixed], w_out.astype(BF16), x2, g1, seq)


def kernel(x, c, rel_bias, norm_gain, w_ada, b_ada, even_w_in, even_b_forget, even_fox_qk_gain,
           even_diff_qk_gain, even_diff_lambda, even_diff_subln_gain, even_w_out, odd_w_qkv,
           odd_qk_gain, odd_w_out, peer_w_query, peer_sub_keys, peer_u, peer_v):
    bsz, seq, d = x.shape
    x2 = x.reshape(bsz * seq, d)
    mod = _adaln(c, w_ada, b_ada)
    for i in range(DEPTH):
        sh1, sc1, g1, sh2, sc2, g2 = [m.reshape(bsz, 1, d) for m in jnp.split(mod[i], 6, axis=-1)]
        j = i // 2
        if i % 2 == 0:
            lam_init = 0.8 - 0.6 * math.exp(-0.3 * i)
            x2 = _even_layer(x2, (sh1, sc1, g1), norm_gain[i, 0], even_w_in[j], even_b_forget[j],
                             even_fox_qk_gain[j], even_diff_qk_gain[j], even_diff_lambda[j],
                             even_diff_subln_gain[j], even_w_out[j], rel_bias[:, :DIFF_HEADS],
                             lam_init, bsz, seq)
        else:
            x2 = _odd_layer(x2, (sh1, sc1, g1), norm_gain[i, 0], odd_w_qkv[j], odd_qk_gain[j],
                            odd_w_out[j], rel_bias, bsz, seq)
        x2 = _peer_ffn(x2, norm_gain[i, 1], sc2, sh2, g2, peer_w_query[i], peer_sub_keys[i],
                       peer_u, peer_v, i, seq)
    return x2.reshape(bsz, seq, d)
```

```python
import functools
import math

import numpy as np
import jax
import jax.numpy as jnp
from jax import lax
from jax.experimental import pallas as pl
from jax.experimental.pallas import tpu as pltpu

F32 = jnp.float32
BF16 = jnp.bfloat16

D_MODEL = 2048
DEPTH = 2
HEAD_DIM = 128
FOX_HEADS = 8
DIFF_HEADS = 4
DIFF_V_DIM = 2 * HEAD_DIM
DIL_HEADS = D_MODEL // HEAD_DIM
DIL_PATTERNS = ((128, 1), (512, 4), (2048, 16))
DIL_BLOCK = 128
N_BUCKETS = 32
BUCKET_MAX_EXACT = 16
BUCKET_MAX_DIST = 2048
PEER_HEADS = 8
PEER_NKEYS = 128
PEER_EXPERTS = PEER_NKEYS * PEER_NKEYS
PEER_DKEY = 256
PEER_TOPK = 16
PEER_PICKS = PEER_HEADS * PEER_TOPK
NORM_EPS = 1e-6
FOX_W = FOX_HEADS * HEAD_DIM
DIFF_QK_W = DIFF_HEADS * 2 * HEAD_DIM
DIFF_V_W = DIFF_HEADS * DIFF_V_DIM
ATTN_SCALE = HEAD_DIM ** -0.5
MASKED = -1e30

LANES = 128
SUBLANES = 8
VMEM_LIMIT = 48 * 1024 * 1024

PROJ_TM = 512
PROJ_TN = 1024
ATTN_T = 1024
ROUTE_TM = 128
PEER_TP = 16


def _cparams(*sem):
    return pltpu.CompilerParams(dimension_semantics=sem, vmem_limit_bytes=VMEM_LIMIT)


def _adaln_kernel(c_ref, w_ref, b_ref, o_ref):
    c = c_ref[...]
    cond = c * jax.nn.sigmoid(c)
    o_ref[...] = jnp.dot(cond.astype(BF16), w_ref[...].astype(BF16),
                         preferred_element_type=F32) + b_ref[...]


def _adaln(c, w_ada, b_ada):
    depth, d, n = w_ada.shape
    bsz = c.shape[0]
    tn = 1024
    return pl.pallas_call(
        _adaln_kernel,
        grid=(depth, n // tn),
        in_specs=[
            pl.BlockSpec((bsz, d), lambda i, j: (0, 0)),
            pl.BlockSpec((None, d, tn), lambda i, j: (i, 0, j)),
            pl.BlockSpec((None, 1, tn), lambda i, j: (i, 0, j)),
        ],
        out_specs=pl.BlockSpec((None, bsz, tn), lambda i, j: (i, 0, j)),
        out_shape=jax.ShapeDtypeStruct((depth, bsz, n), F32),
        compiler_params=_cparams("parallel", "parallel"),
        name="adaln",
    )(c, w_ada, b_ada.reshape(depth, 1, n))


def _modulated_norm(x, gain, sc, sh):
    ms = jnp.mean(x * x, axis=-1, keepdims=True)
    y = x * lax.rsqrt(ms + NORM_EPS) * gain
    return y * (1.0 + sc) + sh


def _norm_proj_kernel(*refs, norm_tiles, has_side):
    if has_side:
        x_ref, gain_ref, sc_ref, sh_ref, w_ref, hg_ref, ws_ref, o_ref, side_ref, h_ref = refs
    else:
        x_ref, gain_ref, sc_ref, sh_ref, w_ref, hg_ref, o_ref, h_ref = refs
    j = pl.program_id(1)

    @pl.when(j == 0)
    def _():
        h = _modulated_norm(x_ref[...], gain_ref[...], sc_ref[...], sh_ref[...])
        h_ref[...] = h.astype(BF16)
        if has_side:
            side_ref[...] = jnp.dot(h_ref[...], ws_ref[...], preferred_element_type=F32)

    acc = jnp.dot(h_ref[...], w_ref[...], preferred_element_type=F32)
    tn = acc.shape[1]

    def write_normed():
        for g in range(tn // HEAD_DIM):
            sl = slice(g * HEAD_DIM, (g + 1) * HEAD_DIM)
            blk = acc[:, sl]
            ms = jnp.mean(blk * blk, axis=-1, keepdims=True)
            o_ref[g] = (blk * lax.rsqrt(ms + NORM_EPS) * hg_ref[:, sl]).astype(o_ref.dtype)

    def write_raw():
        for g in range(tn // HEAD_DIM):
            o_ref[g] = acc[:, g * HEAD_DIM:(g + 1) * HEAD_DIM].astype(o_ref.dtype)

    if not norm_tiles:
        write_raw()
    else:
        is_norm = functools.reduce(jnp.logical_or, [j == t for t in norm_tiles])
        pl.when(is_norm)(write_normed)
        pl.when(jnp.logical_not(is_norm))(write_raw)


def _norm_proj(x2, gain, sc, sh, w, head_gain, norm_tiles, seq, w_side=None):
    t, d = x2.shape
    n = w.shape[1]
    tm, tn = PROJ_TM, PROJ_TN
    per_b = seq // tm
    has_side = w_side is not None
    gpt = tn // HEAD_DIM
    in_specs = [
        pl.BlockSpec((tm, d), lambda i, j: (i, 0)),
        pl.BlockSpec((1, d), lambda i, j: (0, 0)),
        pl.BlockSpec((None, 1, d), lambda i, j: (i // per_b, 0, 0)),
        pl.BlockSpec((None, 1, d), lambda i, j: (i // per_b, 0, 0)),
        pl.BlockSpec((d, tn), lambda i, j: (0, j)),
        pl.BlockSpec((1, tn), lambda i, j: (0, j)),
    ]
    args = [x2, gain.reshape(1, d), sc, sh, w, head_gain]
    out_specs = pl.BlockSpec((gpt, tm, HEAD_DIM), lambda i, j: (j, i, 0))
    out_shape = jax.ShapeDtypeStruct((n // HEAD_DIM, t, HEAD_DIM), BF16)
    if has_side:
        ns = w_side.shape[1]
        in_specs.append(pl.BlockSpec((d, ns), lambda i, j: (0, 0)))
        args.append(w_side)
        out_specs = [out_specs, pl.BlockSpec((tm, ns), lambda i, j: (i, 0))]
        out_shape = [out_shape, jax.ShapeDtypeStruct((t, ns), F32)]
    return pl.pallas_call(
        functools.partial(_norm_proj_kernel, norm_tiles=tuple(norm_tiles), has_side=has_side),
        grid=(t // tm, n // tn),
        in_specs=in_specs,
        out_specs=out_specs,
        out_shape=out_shape,
        scratch_shapes=[pltpu.VMEM((tm, d), BF16)],
        compiler_params=_cparams("parallel", "arbitrary"),
        name="norm_proj",
    )(*args)


def _out_proj_kernel(*refs):
    *a_refs, w_ref, x_ref, g_ref, o_ref = refs
    a = jnp.concatenate([a_ref[g] for a_ref in a_refs for g in range(a_ref.shape[0])], axis=1)
    y = jnp.dot(a, w_ref[...], preferred_element_type=F32)
    o_ref[...] = x_ref[...] + g_ref[...] * y


def _out_proj(heads_major, w, x2, g, seq):
    t = x2.shape[0]
    k, n = w.shape
    assert sum(a.shape[0] for a in heads_major) * HEAD_DIM == k
    tm, tn = PROJ_TM, PROJ_TN
    per_b = seq // tm
    return pl.pallas_call(
        _out_proj_kernel,
        grid=(t // tm, n // tn),
        in_specs=[pl.BlockSpec((a.shape[0], tm, HEAD_DIM), lambda i, j: (0, i, 0)) for a in heads_major] + [
            pl.BlockSpec((k, tn), lambda i, j: (0, j)),
            pl.BlockSpec((tm, tn), lambda i, j: (i, j)),
            pl.BlockSpec((None, 1, tn), lambda i, j: (i // per_b, 0, j)),
        ],
        out_specs=pl.BlockSpec((tm, tn), lambda i, j: (i, j)),
        out_shape=jax.ShapeDtypeStruct((t, n), F32),
        compiler_params=_cparams("parallel", "parallel"),
        name="out_proj",
    )(*heads_major, w, x2, g)


def _forget_kernel(f_ref, b_ref, col_ref, row_ref):
    z = f_ref[...] + b_ref[...]
    lf = jnp.minimum(z, 0.0) - jnp.log1p(jnp.exp(-jnp.abs(z)))
    s = lf.shape[0]
    pos = lax.broadcasted_iota(jnp.int32, lf.shape, 0)
    d = 1
    while d < s:
        lf = lf + jnp.where(pos >= d, pltpu.roll(lf, d, axis=0), 0.0)
        d *= 2
    col_ref[...] = lf
    row_ref[...] = lf.T[:FOX_HEADS, :]


def _forget_cumsum(f_side, b_forget, bsz, seq):
    bpad = jnp.zeros((1, LANES), F32).at[0, :FOX_HEADS].set(b_forget)
    col, row = pl.pallas_call(
        _forget_kernel,
        grid=(bsz,),
        in_specs=[
            pl.BlockSpec((None, seq, LANES), lambda b: (b, 0, 0)),
            pl.BlockSpec((1, LANES), lambda b: (0, 0)),
        ],
        out_specs=[
            pl.BlockSpec((None, seq, LANES), lambda b: (b, 0, 0)),
            pl.BlockSpec((None, FOX_HEADS, seq), lambda b: (b, 0, 0)),
        ],
        out_shape=[
            jax.ShapeDtypeStruct((bsz, seq, LANES), F32),
            jax.ShapeDtypeStruct((bsz, FOX_HEADS, seq), F32),
        ],
        compiler_params=_cparams("parallel"),
        name="forget_cumsum",
    )(f_side.reshape(bsz, seq, LANES), bpad)
    return col, row.reshape(bsz, FOX_HEADS, 1, seq)


def _online_softmax_step(s, v, m_ref, l_ref, acc_ref):
    m_prev = m_ref[...]
    m_new = jnp.maximum(m_prev, jnp.max(s, axis=-1, keepdims=True))
    alpha = jnp.exp(m_prev - m_new)
    p = jnp.exp(s - m_new)
    l_ref[...] = alpha * l_ref[...] + jnp.sum(p, axis=-1, keepdims=True)
    acc_ref[...] = alpha * acc_ref[...] + jnp.dot(p.astype(v.dtype), v, preferred_element_type=F32)
    m_ref[...] = m_new


def _fox_kernel(q_ref, k_ref, v_ref, frow_ref, fcol_ref, o_ref, m_ref, l_ref, acc_ref):
    h = pl.program_id(1)
    qi = pl.program_id(2)
    tt = q_ref.shape[0]
    m_ref[...] = jnp.full_like(m_ref, MASKED)
    l_ref[...] = jnp.zeros_like(l_ref)
    acc_ref[...] = jnp.zeros_like(acc_ref)
    lane = lax.broadcasted_iota(jnp.int32, fcol_ref.shape, 1)
    fq = jnp.sum(jnp.where(lane == h, fcol_ref[...], 0.0), axis=-1, keepdims=True)
    q = q_ref[...]

    def kv_block(j, diagonal):
        rows = pl.ds(pl.multiple_of(j * tt, tt), tt)
        s = lax.dot_general(q, k_ref[rows, :], (((1,), (1,)), ((), ())), preferred_element_type=F32)
        s = s + fq - frow_ref[j]
        if diagonal:
            keep = (lax.broadcasted_iota(jnp.int32, s.shape, 0)
                    >= lax.broadcasted_iota(jnp.int32, s.shape, 1))
            s = jnp.where(keep, s, MASKED)
        _online_softmax_step(s, v_ref[rows, :], m_ref, l_ref, acc_ref)

    def below_diagonal(j, carry):
        kv_block(j, False)
        return carry

    lax.fori_loop(0, qi, below_diagonal, 0)
    kv_block(qi, True)
    o_ref[...] = (acc_ref[...] / l_ref[...]).astype(o_ref.dtype)


def _fox_attention(proj, frow, fcol, bsz, seq):
    t = proj.shape[1]
    tt = ATTN_T
    nb = seq // tt
    kcol = FOX_W // HEAD_DIM
    blk = (None, tt, HEAD_DIM)
    whole = (None, seq, HEAD_DIM)
    return pl.pallas_call(
        _fox_kernel,
        grid=(bsz, FOX_HEADS, nb),
        in_specs=[
            pl.BlockSpec(blk, lambda b, h, qi: (h, b * nb + qi, 0)),
            pl.BlockSpec(whole, lambda b, h, qi: (kcol + h, b, 0)),
            pl.BlockSpec(whole, lambda b, h, qi: (2 * kcol + h, b, 0)),
            pl.BlockSpec((None, None, nb, 1, tt), lambda b, h, qi: (b, h, 0, 0, 0)),
            pl.BlockSpec((None, tt, LANES), lambda b, h, qi: (b, qi, 0)),
        ],
        out_specs=pl.BlockSpec(blk, lambda b, h, qi: (h, b * nb + qi, 0)),
        out_shape=jax.ShapeDtypeStruct((FOX_HEADS, t, HEAD_DIM), BF16),
        scratch_shapes=[
            pltpu.VMEM((tt, 1), F32), pltpu.VMEM((tt, 1), F32), pltpu.VMEM((tt, HEAD_DIM), F32),
        ],
        compiler_params=_cparams("parallel", "parallel", "parallel"),
        name="fox_attention",
    )(proj, proj, proj, frow.reshape(bsz, FOX_HEADS, nb, 1, tt), fcol)


def _diff_kernel(q_ref, k_ref, v_ref, bias_ref, lam_ref, sg_ref, o_ref,
                 m0, l0, a0, m1, l1, a1, *, lam_init):
    qi = pl.program_id(2)
    tt = q_ref.shape[1]
    for m_ref, l_ref, a_ref in ((m0, l0, a0), (m1, l1, a1)):
        m_ref[...] = jnp.full_like(m_ref, MASKED)
        l_ref[...] = jnp.zeros_like(l_ref)
        a_ref[...] = jnp.zeros_like(a_ref)

    def kv_block(j, carry):
        rows = pl.ds(pl.multiple_of(j * tt, tt), tt)
        bias = bias_ref[qi - j]
        v = jnp.concatenate([v_ref[0, rows, :], v_ref[1, rows, :]], axis=1)
        for mi, (m_ref, l_ref, a_ref) in enumerate(((m0, l0, a0), (m1, l1, a1))):
            s = lax.dot_general(q_ref[mi], k_ref[mi, rows, :], (((1,), (1,)), ((), ())),
                                preferred_element_type=F32) + bias
            _online_softmax_step(s, v, m_ref, l_ref, a_ref)
        return carry

    lax.fori_loop(0, qi + 1, kv_block, 0)

    lf = lam_ref[...]
    lam = (jnp.exp(jnp.sum(lf[0:1] * lf[1:2], axis=-1, keepdims=True))
           - jnp.exp(jnp.sum(lf[2:3] * lf[3:4], axis=-1, keepdims=True)) + lam_init)
    o = a0[...] / l0[...] - lam * (a1[...] / l1[...])
    ms = jnp.mean(o * o, axis=-1, keepdims=True)
    o = (o * lax.rsqrt(ms + NORM_EPS) * sg_ref[...] * (1.0 - lam_init)).astype(o_ref.dtype)
    o_ref[0] = o[:, :HEAD_DIM]
    o_ref[1] = o[:, HEAD_DIM:]


def _diff_attention(proj, bias_tiles, lam_param, subln_gain, lam_init, bsz, seq):
    t = proj.shape[1]
    tt = ATTN_T
    nb = seq // tt
    qcol = 3 * FOX_W // DIFF_V_DIM
    kcol = qcol + DIFF_QK_W // DIFF_V_DIM
    vcol = kcol + DIFF_QK_W // DIFF_V_DIM
    blk = (2, tt, HEAD_DIM)
    whole = (2, seq, HEAD_DIM)
    return pl.pallas_call(
        functools.partial(_diff_kernel, lam_init=lam_init),
        grid=(bsz, DIFF_HEADS, nb),
        in_specs=[
            pl.BlockSpec(blk, lambda b, h, qi: (qcol + h, b * nb + qi, 0)),
            pl.BlockSpec(whole, lambda b, h, qi: (kcol + h, b, 0)),
            pl.BlockSpec(whole, lambda b, h, qi: (vcol + h, b, 0)),
            pl.BlockSpec((None, nb, tt, tt), lambda b, h, qi: (h, 0, 0, 0)),
            pl.BlockSpec((4, HEAD_DIM), lambda b, h, qi: (0, 0)),
            pl.BlockSpec((1, DIFF_V_DIM), lambda b, h, qi: (0, 0)),
        ],
        out_specs=pl.BlockSpec(blk, lambda b, h, qi: (h, b * nb + qi, 0)),
        out_shape=jax.ShapeDtypeStruct((2 * DIFF_HEADS, t, HEAD_DIM), BF16),
        scratch_shapes=[
            pltpu.VMEM((tt, 1), F32), pltpu.VMEM((tt, 1), F32), pltpu.VMEM((tt, DIFF_V_DIM), F32),
            pltpu.VMEM((tt, 1), F32), pltpu.VMEM((tt, 1), F32), pltpu.VMEM((tt, DIFF_V_DIM), F32),
        ],
        compiler_params=_cparams("parallel", "parallel", "parallel"),
        name="diff_attention",
    )(proj, proj, proj, bias_tiles, lam_param, subln_gain.reshape(1, DIFF_V_DIM))


def _t5_bucket_np(dist):
    n = np.maximum(dist, 0)
    nf = np.maximum(n, 1).astype(np.float32)
    large = BUCKET_MAX_EXACT + (np.log(nf / np.float32(BUCKET_MAX_EXACT))
                                / np.float32(math.log(BUCKET_MAX_DIST / BUCKET_MAX_EXACT))
                                * np.float32(N_BUCKETS - BUCKET_MAX_EXACT)).astype(np.int32)
    large = np.minimum(large, N_BUCKETS - 1)
    return np.where(n < BUCKET_MAX_EXACT, n, large).astype(np.int32)


def _bucket_thresholds():
    buckets = _t5_bucket_np(np.arange(BUCKET_MAX_DIST + 1))
    assert np.all(np.diff(buckets) >= 0)
    return [int(np.argmax(buckets >= b)) for b in range(N_BUCKETS)]


_BUCKET_THRESHOLDS = _bucket_thresholds()


def _bias_from_distance(dist, valid, table_ref, h):
    val = jnp.full(dist.shape, table_ref[0, h], F32)
    for b in range(1, N_BUCKETS):
        val = jnp.where(dist >= _BUCKET_THRESHOLDS[b], table_ref[b, h], val)
    return jnp.where(valid, val, MASKED)


def _causal_bias_kernel(table_ref, o_ref):
    h, delta = pl.program_id(0), pl.program_id(1)
    tt = o_ref.shape[0]
    dist = (delta * tt + lax.broadcasted_iota(jnp.int32, o_ref.shape, 0)
            - lax.broadcasted_iota(jnp.int32, o_ref.shape, 1))
    o_ref[...] = _bias_from_distance(dist, dist >= 0, table_ref, h)


def _causal_bias_tiles(table, seq, tt):
    nb = seq // tt
    heads = table.shape[1]
    return pl.pallas_call(
        _causal_bias_kernel,
        grid=(heads, nb),
        in_specs=[pl.BlockSpec(memory_space=pltpu.SMEM)],
        out_specs=pl.BlockSpec((None, None, tt, tt), lambda h, dlt: (h, dlt, 0, 0)),
        out_shape=jax.ShapeDtypeStruct((heads, nb, tt, tt), F32),
        compiler_params=_cparams("parallel", "parallel"),
        name="causal_bias_tiles",
    )(table)


def _dilated_bias_kernel(table_ref, o_ref):
    h = pl.program_id(0)
    span = DIL_BLOCK
    shape = o_ref.shape[1:]
    step = lax.broadcasted_iota(jnp.int32, shape, 0) + span - lax.broadcasted_iota(jnp.int32, shape, 1)
    valid = jnp.logical_and(step >= 0, step <= span)
    for p, (_, dil) in enumerate(DIL_PATTERNS):
        o_ref[p] = _bias_from_distance(step * dil, valid, table_ref, h)


def _dilated_bias_tiles(table):
    heads = table.shape[1]
    npat = len(DIL_PATTERNS)
    return pl.pallas_call(
        _dilated_bias_kernel,
        grid=(heads,),
        in_specs=[pl.BlockSpec(memory_space=pltpu.SMEM)],
        out_specs=pl.BlockSpec((npat, None, DIL_BLOCK, 2 * DIL_BLOCK), lambda h: (0, h, 0, 0)),
        out_shape=jax.ShapeDtypeStruct((npat, heads, DIL_BLOCK, 2 * DIL_BLOCK), F32),
        compiler_params=_cparams("parallel"),
        name="dilated_bias_tiles",
    )(table)


DIL_GROUP = 8


def _dilated_kernel(q_ref, k_ref, v_ref, bias_ref, o_ref, qf, kf, vf, o_scr, lse_scr):
    seq = q_ref.shape[0]
    blk = DIL_BLOCK
    qf[...] = q_ref[...].astype(F32)
    for src, dst in ((k_ref, kf), (v_ref, vf)):
        dst[0:blk, :] = jnp.zeros((blk, HEAD_DIM), F32)
        dst[blk:, :] = src[...].astype(F32)
    dn_qk = (((2,), (2,)), ((0,), (0,)))
    dn_pv = (((2,), (1,)), ((0,), (0,)))

    def rows(ref, start, dil):
        if dil == 1:
            return ref[pl.ds(start, blk), :]
        return ref[pl.ds(start, blk, stride=dil), :]

    for p, (_, dil) in enumerate(DIL_PATTERNS):
        nblk = seq // dil // blk
        blocks = [(jb, r) for jb in range(nblk) for r in range(dil)]
        bias = bias_ref[p]
        for g0 in range(0, len(blocks), DIL_GROUP):
            group = blocks[g0:g0 + DIL_GROUP]
            starts = [jb * blk * dil + r for jb, r in group]
            has_prev = [jb > 0 for jb, _ in group]
            q = jnp.stack([rows(qf, st, dil) for st in starts]).astype(BF16)
            kc = jnp.stack([rows(kf, blk + st, dil) for st in starts]).astype(BF16)
            vc = jnp.stack([rows(vf, blk + st, dil) for st in starts]).astype(BF16)
            s_cur = lax.dot_general(q, kc, dn_qk, preferred_element_type=F32) + bias[:, blk:]
            m = jnp.max(s_cur, axis=-1, keepdims=True)
            if any(has_prev):
                prev_starts = [max(blk + st - blk * dil, 0) for st in starts]
                kp = jnp.stack([rows(kf, st, dil) for st in prev_starts]).astype(BF16)
                vp = jnp.stack([rows(vf, st, dil) for st in prev_starts]).astype(BF16)
                s_prev = lax.dot_general(q, kp, dn_qk, preferred_element_type=F32) + bias[:, :blk]
                which = lax.broadcasted_iota(jnp.int32, s_prev.shape, 0)
                for gi, ok in enumerate(has_prev):
                    if not ok:
                        s_prev = jnp.where(which == gi, MASKED, s_prev)
                m = jnp.maximum(m, jnp.max(s_prev, axis=-1, keepdims=True))
                p_prev = jnp.exp(s_prev - m)
            p_cur = jnp.exp(s_cur - m)
            l = jnp.sum(p_cur, axis=-1, keepdims=True)
            acc = lax.dot_general(p_cur.astype(BF16), vc, dn_pv, preferred_element_type=F32)
            if any(has_prev):
                l = l + jnp.sum(p_prev, axis=-1, keepdims=True)
                acc = acc + lax.dot_general(p_prev.astype(BF16), vp, dn_pv, preferred_element_type=F32)
            out = acc / l
            lse = jnp.broadcast_to(m + jnp.log(l), out.shape)
            for gi, st in enumerate(starts):
                if dil == 1:
                    o_scr[p, pl.ds(st, blk), :] = out[gi]
                    lse_scr[p, pl.ds(st, blk), :] = lse[gi]
                else:
                    o_scr[p, pl.ds(st, blk, stride=dil), :] = out[gi]
                    lse_scr[p, pl.ds(st, blk, stride=dil), :] = lse[gi]

    lses = [lse_scr[p] for p in range(len(DIL_PATTERNS))]
    top = functools.reduce(jnp.maximum, lses)
    es = [jnp.exp(x - top) for x in lses]
    den = functools.reduce(lambda a, b: a + b, es)
    mixed = functools.reduce(lambda a, b: a + b, [(e / den) * o_scr[p] for p, e in enumerate(es)])
    o_ref[...] = mixed.astype(o_ref.dtype)


def _dilated_attention(qkv, bias_tiles, bsz, seq):
    t = qkv.shape[1]
    h16 = DIL_HEADS
    npat = len(DIL_PATTERNS)
    blk = (None, seq, HEAD_DIM)
    return pl.pallas_call(
        _dilated_kernel,
        grid=(bsz, DIL_HEADS),
        in_specs=[
            pl.BlockSpec(blk, lambda b, h: (h, b, 0)),
            pl.BlockSpec(blk, lambda b, h: (h16 + h, b, 0)),
            pl.BlockSpec(blk, lambda b, h: (2 * h16 + h, b, 0)),
            pl.BlockSpec((npat, None, DIL_BLOCK, 2 * DIL_BLOCK), lambda b, h: (0, h, 0, 0)),
        ],
        out_specs=pl.BlockSpec(blk, lambda b, h: (h, b, 0)),
        out_shape=jax.ShapeDtypeStruct((DIL_HEADS, t, HEAD_DIM), BF16),
        scratch_shapes=[
            pltpu.VMEM((seq, HEAD_DIM), F32),
            pltpu.VMEM((DIL_BLOCK + seq, HEAD_DIM), F32),
            pltpu.VMEM((DIL_BLOCK + seq, HEAD_DIM), F32),
            pltpu.VMEM((npat, seq, HEAD_DIM), F32),
            pltpu.VMEM((npat, seq, HEAD_DIM), F32),
        ],
        compiler_params=_cparams("parallel", "parallel"),
        name="dilated_attention",
    )(qkv, qkv, qkv, bias_tiles)


def _topk_rows(s, val_ref, pick_ref, payload=None):
    neg_rows = -lax.broadcasted_iota(jnp.int32, s.shape, 0).astype(F32)
    for k in range(PEER_TOPK):
        m = jnp.max(s, axis=0, keepdims=True)
        am = jnp.max(jnp.where(s == m, neg_rows, -jnp.inf), axis=0, keepdims=True)
        hit = neg_rows == am
        val_ref[k:k + 1, :] = m
        if payload is None:
            pick_ref[k:k + 1, :] = (-am).astype(jnp.int32)
        else:
            pick_ref[k:k + 1, :] = jnp.sum(jnp.where(hit, payload, 0), axis=0, keepdims=True)
        s = jnp.where(hit, -jnp.inf, s)


def _candidate_rows(a):
    need = PEER_TOPK // (a + 1)
    if need > SUBLANES:
        return PEER_TOPK
    return SUBLANES if need > 1 else 1


def _route_kernel(q_ref, keys_ref, idx_ref, gate_ref, v0, i0, v1, i1, best_ref):
    _route_half(q_ref[0], keys_ref[0], v0, i0)
    _route_half(q_ref[1], keys_ref[1], v1, i1)
    _route_combine(idx_ref, gate_ref, v0, i0, v1, i1, best_ref)


def _route_half(q, keys, val_ref, idx_ref):
    scores = lax.dot_general(keys, q, (((1,), (1,)), ((), ())), preferred_element_type=F32)
    _topk_rows(scores, val_ref, idx_ref)


def _route_combine(idx_ref, gate_ref, v0, i0, v1, i1, best_ref):
    first_single = min(a for a in range(PEER_TOPK) if _candidate_rows(a) == 1)
    cand_s, cand_i = [], []
    for a in range(first_single):
        nb = _candidate_rows(a)
        cand_s.append(v0[a:a + 1, :] + v1[0:nb, :])
        cand_i.append(i0[a:a + 1, :] * PEER_NKEYS + i1[0:nb, :])
    cand_s.append(v0[first_single:, :] + v1[0:1, :])
    cand_i.append(i0[first_single:, :] * PEER_NKEYS + i1[0:1, :])
    _topk_rows(jnp.concatenate(cand_s, axis=0), best_ref, idx_ref, jnp.concatenate(cand_i, axis=0))
    best_s = best_ref[...]
    e = jnp.exp(best_s - best_s[0:1, :])
    gate_ref[...] = e / jnp.sum(e, axis=0, keepdims=True)


def _route_scratch(tm):
    return [pltpu.VMEM((PEER_TOPK, tm), F32), pltpu.VMEM((PEER_TOPK, tm), jnp.int32),
            pltpu.VMEM((PEER_TOPK, tm), F32), pltpu.VMEM((PEER_TOPK, tm), jnp.int32),
            pltpu.VMEM((PEER_TOPK, tm), F32)]


def _peer_route(qp, sub_keys, t):
    tm = ROUTE_TM
    out_spec = pl.BlockSpec((None, PEER_TOPK, tm), lambda i, h: (h, 0, i))
    return pl.pallas_call(
        _route_kernel,
        grid=(t // tm, PEER_HEADS),
        in_specs=[
            pl.BlockSpec((2, tm, PEER_DKEY // 2), lambda i, h: (h, i, 0)),
            pl.BlockSpec((2, PEER_NKEYS, PEER_DKEY // 2), lambda i, h: (0, 0, 0)),
        ],
        out_specs=[out_spec, out_spec],
        out_shape=[
            jax.ShapeDtypeStruct((PEER_HEADS, PEER_TOPK, t), jnp.int32),
            jax.ShapeDtypeStruct((PEER_HEADS, PEER_TOPK, t), F32),
        ],
        scratch_shapes=_route_scratch(tm),
        compiler_params=_cparams("parallel", "parallel"),
        name="peer_route",
    )(qp, sub_keys)


def _gelu_exact(a):
    return 0.5 * a * (1.0 + lax.erf(a * (2.0 ** -0.5)))


def _pack_expert_tables(tables_u, tables_v, layer):
    _, e, d = tables_u.shape
    te = PACK_ROWS
    n_steps = e // te
    layer_rows = pl.BlockSpec((None, te, d), lambda i: (layer, i, 0))
    return pl.pallas_call(
        _pack_kernel,
        grid=(n_steps,),
        in_specs=[layer_rows, layer_rows],
        out_specs=pl.BlockSpec(memory_space=pl.ANY),
        out_shape=jax.ShapeDtypeStruct((e, 1, d), jnp.uint32),
        scratch_shapes=[pltpu.VMEM((2, te, d), jnp.uint32), pltpu.SemaphoreType.DMA((2,))],
        compiler_params=_cparams("arbitrary"),
        name="pack_expert_tables",
    )(tables_u, tables_v)


PACK_ROWS = 128


def _pack_kernel(u_ref, v_ref, out_hbm, stage, sems):
    i = pl.program_id(0)
    n_steps = pl.num_programs(0)
    te, d = u_ref.shape
    half = d // 2
    slot = i % 2

    def bf16_high_bits(x):
        return lax.bitcast_convert_type(x.astype(BF16).astype(F32), jnp.uint32)

    def pack(ref):
        return bf16_high_bits(ref[:, half:]) | (bf16_high_bits(ref[:, :half]) >> 16)

    def wait_slot(sl):
        pltpu.make_async_copy(stage.at[sl], stage.at[sl], sems.at[sl]).wait()

    @pl.when(i >= 2)
    def _():
        wait_slot(slot)

    stage[slot] = jnp.concatenate([pack(u_ref), pack(v_ref)], axis=1)
    for r in range(te):
        pltpu.make_async_copy(stage.at[slot, pl.ds(r, 1)], out_hbm.at[i * te + r],
                              sems.at[slot]).start(priority=r % 2)

    @pl.when(i == n_steps - 1)
    def _():
        wait_slot(slot)

        @pl.when(n_steps > 1)
        def _():
            wait_slot(1 - slot)


def _unpack_words(w):
    lo = lax.bitcast_convert_type(w << 16, F32)
    hi = lax.bitcast_convert_type(w & jnp.uint32(0xFFFF0000), F32)
    return lo, hi


PEER_NBUF = 4
PEER_LOOKAHEAD = 2
ROUTE_AHEAD_BUFS = 3


def _expert_kernel(idx0_ref, gates0_ref, qnext_ref, keys_ref, x_ref, gain_ref, sc_ref, sh_ref, g2_ref,
                   tab_hbm, o_ref, *scratch):
    bufs, sems = scratch[:PEER_NBUF], scratch[PEER_NBUF]
    (idx_smem, gates_v, idx_t, gate_t, stage, route_sem), route_scratch = (
        scratch[PEER_NBUF + 1:PEER_NBUF + 7], scratch[PEER_NBUF + 7:])
    s = pl.program_id(0)
    n_steps = pl.num_programs(0)
    tp = x_ref.shape[0]
    gt = tp // PEER_NBUF
    d = x_ref.shape[1]
    ngroup = PEER_PICKS // SUBLANES
    nword = d // 2 // LANES
    spt = ROUTE_TM // tp
    r = s % spt
    par = (s // spt) % ROUTE_AHEAD_BUFS
    gain, sc, sh, g2 = gain_ref[...], sc_ref[...], sh_ref[...], g2_ref[...]
    eye = (lax.broadcasted_iota(jnp.int32, (PEER_PICKS, PEER_PICKS), 0)
           == lax.broadcasted_iota(jnp.int32, (PEER_PICKS, PEER_PICKS), 1))

    def park_routing(idx_rows, to_par):
        cp = pltpu.make_async_copy(idx_rows, idx_smem.at[to_par], route_sem.at[0])
        cp.start()
        cp.wait()

    @pl.when(s == 0)
    def _():
        for tile in range(ROUTE_AHEAD_BUFS - 1):
            park_routing(idx0_ref.at[pl.ds(tile * ROUTE_TM, ROUTE_TM)], tile)
            gates_v[tile] = gates0_ref[pl.ds(tile * ROUTE_TM, ROUTE_TM), :]

    head_rows = pl.ds(pl.multiple_of(r * PEER_TOPK, PEER_TOPK), PEER_TOPK)
    rv0, ri0, rv1, ri1, rbest = route_scratch
    route_stages = (
        lambda: _route_half(qnext_ref[2 * r], keys_ref[0], rv0, ri0),
        lambda: _route_half(qnext_ref[2 * r + 1], keys_ref[1], rv1, ri1),
        lambda: _route_combine(idx_t.at[head_rows], gate_t.at[head_rows], rv0, ri0, rv1, ri1, rbest),
    )

    def start_token(tile_par, row, j, ti):
        for g in range(ngroup):
            for k in range(SUBLANES):
                e = idx_smem[tile_par, row, g * SUBLANES + k]
                pltpu.make_async_copy(
                    tab_hbm.at[e],
                    bufs[j].at[pl.ds((ti * ngroup + g) * SUBLANES + k, 1)],
                    sems.at[j]).start(priority=k % 2)

    def wait_group(j):
        pltpu.make_async_copy(bufs[j], bufs[j], sems.at[j]).wait()

    def gated_activation(buf, ti, tok):
        h = _modulated_norm(x_ref[pl.ds(tok, 1), :], gain, sc, sh)
        hb = jnp.broadcast_to(h, (SUBLANES, d))
        parts = []
        for g in range(ngroup):
            acc = None
            for c in range(nword):
                lo, hi = _unpack_words(buf[pl.ds((ti * ngroup + g) * SUBLANES, SUBLANES),
                                           c * LANES:(c + 1) * LANES])
                term = (lo * hb[:, c * LANES:(c + 1) * LANES]
                        + hi * hb[:, (nword + c) * LANES:(nword + c + 1) * LANES])
                acc = term if acc is None else acc + term
            parts.append(acc)
        act = jnp.sum(jnp.concatenate(parts, axis=0), axis=-1, keepdims=True)
        gate_row = gates_v[par, pl.ds(r * tp + tok, 1), :]
        gate_col = jnp.sum(jnp.where(eye, gate_row, 0.0), axis=-1, keepdims=True)
        return gate_col * _gelu_exact(act)

    def weighted_values(buf, ti, tok, a):
        ylo = [None] * nword
        yhi = [None] * nword
        for g in range(ngroup):
            ag = a[g * SUBLANES:(g + 1) * SUBLANES, :]
            for c in range(nword):
                lo, hi = _unpack_words(buf[pl.ds((ti * ngroup + g) * SUBLANES, SUBLANES),
                                           (nword + c) * LANES:(nword + c + 1) * LANES])
                ylo[c] = lo * ag if ylo[c] is None else ylo[c] + lo * ag
                yhi[c] = hi * ag if yhi[c] is None else yhi[c] + hi * ag
        y = jnp.sum(jnp.concatenate(ylo + yhi, axis=1), axis=0, keepdims=True)
        o_ref[pl.ds(tok, 1), :] = x_ref[pl.ds(tok, 1), :] + g2 * y

    @pl.when(s == 0)
    def _():
        for j in range(PEER_LOOKAHEAD):
            def body(ti, carry, j=j):
                start_token(0, j * gt + ti, j, ti)
                return carry
            lax.fori_loop(0, gt, body, 0)

    last_of_tile = r == spt - 1
    next_par = jnp.where(last_of_tile, (par + 1) % ROUTE_AHEAD_BUFS, par)
    next_row = jnp.where(last_of_tile, 0, (r + 1) * tp)

    pending = None
    for j in range(PEER_NBUF):
        wait_group(j)
        if j < len(route_stages):
            route_stages[j]()
        ahead = j + PEER_LOOKAHEAD
        for ti in range(gt):
            tok = j * gt + ti
            if ahead < PEER_NBUF:
                start_token(par, r * tp + ahead * gt + ti, ahead, ti)
            else:
                start_token(next_par, next_row + (ahead - PEER_NBUF) * gt + ti, ahead - PEER_NBUF, ti)
            a = gated_activation(bufs[j], ti, tok)
            if pending is not None:
                weighted_values(*pending)
            pending = (bufs[j], ti, tok, a)
    weighted_values(*pending)

    @pl.when(last_of_tile)
    def _():
        free_par = (par + ROUTE_AHEAD_BUFS - 1) % ROUTE_AHEAD_BUFS
        stage[...] = idx_t[...].T
        gates_v[free_par] = gate_t[...].T
        park_routing(stage, free_par)

    @pl.when(s == n_steps - 1)
    def _():
        for j in range(PEER_LOOKAHEAD):
            wait_group(j)


def _peer_experts(x2, gain, sc, sh, g2, idx0, gates0, qp, sub_keys, packed, seq):
    t, d = x2.shape
    tp = PEER_TP
    tm = ROUTE_TM
    ahead = ROUTE_AHEAD_BUFS - 1
    assert tm == PEER_HEADS * tp
    n_steps = t // tp
    n_tiles = t // tm
    spt = tm // tp
    per_b = seq // tp
    whole = lambda s: (0, 0)
    return pl.pallas_call(
        _expert_kernel,
        grid=(n_steps,),
        in_specs=[
            pl.BlockSpec((ahead * tm, PEER_PICKS), whole),
            pl.BlockSpec((ahead * tm, PEER_PICKS), whole),
            pl.BlockSpec((2 * PEER_HEADS, tm, PEER_DKEY // 2),
                         lambda s: (0, jnp.minimum(s // spt + ahead, n_tiles - 1), 0)),
            pl.BlockSpec((2, PEER_NKEYS, PEER_DKEY // 2), lambda s: (0, 0, 0)),
            pl.BlockSpec((tp, d), lambda s: (s, 0)),
            pl.BlockSpec((1, d), lambda s: (0, 0)),
            pl.BlockSpec((None, 1, d), lambda s: (s // per_b, 0, 0)),
            pl.BlockSpec((None, 1, d), lambda s: (s // per_b, 0, 0)),
            pl.BlockSpec((None, 1, d), lambda s: (s // per_b, 0, 0)),
            pl.BlockSpec(memory_space=pl.ANY),
        ],
        out_specs=pl.BlockSpec((tp, d), lambda s: (s, 0)),
        out_shape=jax.ShapeDtypeStruct((t, d), F32),
        scratch_shapes=(
            [pltpu.VMEM((tp // PEER_NBUF * PEER_PICKS, d), jnp.uint32)] * PEER_NBUF
            + [pltpu.SemaphoreType.DMA((PEER_NBUF,)),
               pltpu.SMEM((ROUTE_AHEAD_BUFS, tm, PEER_PICKS), jnp.int32),
               pltpu.VMEM((ROUTE_AHEAD_BUFS, tm, PEER_PICKS), F32),
               pltpu.VMEM((PEER_PICKS, tm), jnp.int32),
               pltpu.VMEM((PEER_PICKS, tm), F32),
               pltpu.VMEM((tm, PEER_PICKS), jnp.int32),
               pltpu.SemaphoreType.DMA((1,))]
            + _route_scratch(tm)),
        compiler_params=_cparams("arbitrary"),
        name="peer_experts",
    )(idx0, gates0, qp, sub_keys, x2, gain.reshape(1, d), sc, sh, g2, packed)


def _peer_ffn(x2, gain, sc, sh, g2, w_query, sub_keys, tables_u, tables_v, layer, seq):
    no_gain = jnp.ones((1, w_query.shape[1]), F32)
    qp = _norm_proj(x2, gain, sc, sh, w_query.astype(BF16), no_gain, (), seq)
    keys = sub_keys.astype(BF16)
    lead = (ROUTE_AHEAD_BUFS - 1) * ROUTE_TM
    idx0, gates0 = _peer_route(qp, keys, lead)
    idx0 = idx0.transpose(2, 0, 1).reshape(lead, PEER_PICKS)
    gates0 = gates0.transpose(2, 0, 1).reshape(lead, PEER_PICKS)
    packed = _pack_expert_tables(tables_u, tables_v, layer)
    return _peer_experts(x2, gain, sc, sh, g2, idx0, gates0, qp, keys, packed, seq)


def _tile_gain(gain, heads, scale=1.0):
    return jnp.tile(gain * scale, heads)


def _even_layer(x2, mods, norm_gain, w_in, b_forget, fox_qk_gain, diff_qk_gain, diff_lambda,
                diff_subln_gain, w_out, diff_bias, lam_init, bsz, seq):
    sh1, sc1, g1 = mods
    w_main = jnp.concatenate([w_in[:, :3 * FOX_W], w_in[:, 3 * FOX_W + FOX_HEADS:]], axis=1).astype(BF16)
    w_forget = jnp.zeros((D_MODEL, LANES), F32).at[:, :FOX_HEADS].set(
        w_in[:, 3 * FOX_W:3 * FOX_W + FOX_HEADS]).astype(BF16)
    ones = jnp.ones((FOX_W,), F32)
    head_gain = jnp.concatenate([
        _tile_gain(fox_qk_gain[0], FOX_HEADS, ATTN_SCALE), _tile_gain(fox_qk_gain[1], FOX_HEADS), ones,
        _tile_gain(diff_qk_gain[0], 2 * DIFF_HEADS, ATTN_SCALE), _tile_gain(diff_qk_gain[1], 2 * DIFF_HEADS),
        ones]).reshape(1, -1)
    proj, f_side = _norm_proj(x2, norm_gain, sc1, sh1, w_main, head_gain, (0, 1, 3, 4), seq,
                              w_side=w_forget)
    fcol, frow = _forget_cumsum(f_side, b_forget, bsz, seq)
    fox_o = _fox_attention(proj, frow, fcol, bsz, seq)
    bias_tiles = _causal_bias_tiles(diff_bias, seq, ATTN_T)
    diff_o = _diff_attention(proj, bias_tiles, diff_lambda, diff_subln_gain, lam_init, bsz, seq)
    return _out_proj([fox_o, diff_o], w_out.astype(BF16), x2, g1, seq)


def _odd_layer(x2, mods, norm_gain, w_qkv, qk_gain, w_out, bias_table, bsz, seq):
    sh1, sc1, g1 = mods
    ones = jnp.ones((D_MODEL,), F32)
    head_gain = jnp.concatenate([
        _tile_gain(qk_gain[0], DIL_HEADS, ATTN_SCALE), _tile_gain(qk_gain[1], DIL_HEADS), ones]).reshape(1, -1)
    qkv = _norm_proj(x2, norm_gain, sc1, sh1, w_qkv.astype(BF16), head_gain, (0, 1, 2, 3), seq)
    mixed = _dilated_attention(qkv, _dilated_bias_tiles(bias_table), bsz, seq)
    return _out_proj([mixed], w_out.astype(BF16), x2, g1, seq)


def kernel(x, c, rel_bias, norm_gain, w_ada, b_ada, even_w_in, even_b_forget, even_fox_qk_gain,
           even_diff_qk_gain, even_diff_lambda, even_diff_subln_gain, even_w_out, odd_w_qkv,
           odd_qk_gain, odd_w_out, peer_w_query, peer_sub_keys, peer_u, peer_v):
    bsz, seq, d = x.shape
    x2 = x.reshape(bsz * seq, d)
    mod = _adaln(c, w_ada, b_ada)
    for i in range(DEPTH):
        sh1, sc1, g1, sh2, sc2, g2 = [m.reshape(bsz, 1, d) for m in jnp.split(mod[i], 6, axis=-1)]
        j = i // 2
        if i % 2 == 0:
            lam_init = 0.8 - 0.6 * math.exp(-0.3 * i)
            x2 = _even_layer(x2, (sh1, sc1, g1), norm_gain[i, 0], even_w_in[j], even_b_forget[j],
                             even_fox_qk_gain[j], even_diff_qk_gain[j], even_diff_lambda[j],
                             even_diff_subln_gain[j], even_w_out[j], rel_bias[:, :DIFF_HEADS],
                             lam_init, bsz, seq)
        else:
            x2 = _odd_layer(x2, (sh1, sc1, g1), norm_gain[i, 0], odd_w_qkv[j], odd_qk_gain[j],
                            odd_w_out[j], rel_bias, bsz, seq)
        x2 = _peer_ffn(x2, norm_gain[i, 1], sc2, sh2, g2, peer_w_query[i], peer_sub_keys[i],
                       peer_u, peer_v, i, seq)
    return x2.reshape(bsz, seq, d)
```

```python
import functools
import math

import numpy as np
import jax
import jax.numpy as jnp
from jax import lax
from jax.experimental import pallas as pl
from jax.experimental.pallas import tpu as pltpu

F32 = jnp.float32
BF16 = jnp.bfloat16

D_MODEL = 2048
DEPTH = 2
HEAD_DIM = 128
FOX_HEADS = 8
DIFF_HEADS = 4
DIFF_V_DIM = 2 * HEAD_DIM
DIL_HEADS = D_MODEL // HEAD_DIM
DIL_PATTERNS = ((128, 1), (512, 4), (2048, 16))
DIL_BLOCK = 128
N_BUCKETS = 32
BUCKET_MAX_EXACT = 16
BUCKET_MAX_DIST = 2048
PEER_HEADS = 8
PEER_NKEYS = 128
PEER_EXPERTS = PEER_NKEYS * PEER_NKEYS
PEER_DKEY = 256
PEER_TOPK = 16
PEER_PICKS = PEER_HEADS * PEER_TOPK
NORM_EPS = 1e-6
FOX_W = FOX_HEADS * HEAD_DIM
DIFF_QK_W = DIFF_HEADS * 2 * HEAD_DIM
DIFF_V_W = DIFF_HEADS * DIFF_V_DIM
ATTN_SCALE = HEAD_DIM ** -0.5
MASKED = -1e30

LANES = 128
SUBLANES = 8
VMEM_LIMIT = 48 * 1024 * 1024

PROJ_TM = 512
PROJ_TN = 1024
ATTN_T = 1024
ROUTE_TM = 128
PEER_TP = 16


def _cparams(*sem):
    return pltpu.CompilerParams(dimension_semantics=sem, vmem_limit_bytes=VMEM_LIMIT)


def _adaln_kernel(c_ref, w_ref, b_ref, o_ref):
    c = c_ref[...]
    cond = c * jax.nn.sigmoid(c)
    o_ref[...] = jnp.dot(cond.astype(BF16), w_ref[...].astype(BF16),
                         preferred_element_type=F32) + b_ref[...]


def _adaln(c, w_ada, b_ada):
    depth, d, n = w_ada.shape
    bsz = c.shape[0]
    tn = 1024
    return pl.pallas_call(
        _adaln_kernel,
        grid=(depth, n // tn),
        in_specs=[
            pl.BlockSpec((bsz, d), lambda i, j: (0, 0)),
            pl.BlockSpec((None, d, tn), lambda i, j: (i, 0, j)),
            pl.BlockSpec((None, 1, tn), lambda i, j: (i, 0, j)),
        ],
        out_specs=pl.BlockSpec((None, bsz, tn), lambda i, j: (i, 0, j)),
        out_shape=jax.ShapeDtypeStruct((depth, bsz, n), F32),
        compiler_params=_cparams("parallel", "parallel"),
        name="adaln",
    )(c, w_ada, b_ada.reshape(depth, 1, n))


def _modulated_norm(x, gain, sc, sh):
    ms = jnp.mean(x * x, axis=-1, keepdims=True)
    y = x * lax.rsqrt(ms + NORM_EPS) * gain
    return y * (1.0 + sc) + sh


def _norm_proj_kernel(*refs, norm_tiles, has_side):
    if has_side:
        x_ref, gain_ref, sc_ref, sh_ref, w_ref, hg_ref, ws_ref, o_ref, side_ref, h_ref = refs
    else:
        x_ref, gain_ref, sc_ref, sh_ref, w_ref, hg_ref, o_ref, h_ref = refs
    j = pl.program_id(1)

    @pl.when(j == 0)
    def _():
        h = _modulated_norm(x_ref[...], gain_ref[...], sc_ref[...], sh_ref[...])
        h_ref[...] = h.astype(BF16)
        if has_side:
            side_ref[...] = jnp.dot(h_ref[...], ws_ref[...], preferred_element_type=F32)

    acc = jnp.dot(h_ref[...], w_ref[...], preferred_element_type=F32)
    tn = acc.shape[1]

    def write_normed():
        for g in range(tn // HEAD_DIM):
            sl = slice(g * HEAD_DIM, (g + 1) * HEAD_DIM)
            blk = acc[:, sl]
            ms = jnp.mean(blk * blk, axis=-1, keepdims=True)
            o_ref[g] = (blk * lax.rsqrt(ms + NORM_EPS) * hg_ref[:, sl]).astype(o_ref.dtype)

    def write_raw():
        for g in range(tn // HEAD_DIM):
            o_ref[g] = acc[:, g * HEAD_DIM:(g + 1) * HEAD_DIM].astype(o_ref.dtype)

    if not norm_tiles:
        write_raw()
    else:
        is_norm = functools.reduce(jnp.logical_or, [j == t for t in norm_tiles])
        pl.when(is_norm)(write_normed)
        pl.when(jnp.logical_not(is_norm))(write_raw)


def _norm_proj(x2, gain, sc, sh, w, head_gain, norm_tiles, seq, w_side=None):
    t, d = x2.shape
    n = w.shape[1]
    tm, tn = PROJ_TM, PROJ_TN
    per_b = seq // tm
    has_side = w_side is not None
    gpt = tn // HEAD_DIM
    in_specs = [
        pl.BlockSpec((tm, d), lambda i, j: (i, 0)),
        pl.BlockSpec((1, d), lambda i, j: (0, 0)),
        pl.BlockSpec((None, 1, d), lambda i, j: (i // per_b, 0, 0)),
        pl.BlockSpec((None, 1, d), lambda i, j: (i // per_b, 0, 0)),
        pl.BlockSpec((d, tn), lambda i, j: (0, j)),
        pl.BlockSpec((1, tn), lambda i, j: (0, j)),
    ]
    args = [x2, gain.reshape(1, d), sc, sh, w, head_gain]
    out_specs = pl.BlockSpec((gpt, tm, HEAD_DIM), lambda i, j: (j, i, 0))
    out_shape = jax.ShapeDtypeStruct((n // HEAD_DIM, t, HEAD_DIM), BF16)
    if has_side:
        ns = w_side.shape[1]
        in_specs.append(pl.BlockSpec((d, ns), lambda i, j: (0, 0)))
        args.append(w_side)
        out_specs = [out_specs, pl.BlockSpec((tm, ns), lambda i, j: (i, 0))]
        out_shape = [out_shape, jax.ShapeDtypeStruct((t, ns), F32)]
    return pl.pallas_call(
        functools.partial(_norm_proj_kernel, norm_tiles=tuple(norm_tiles), has_side=has_side),
        grid=(t // tm, n // tn),
        in_specs=in_specs,
        out_specs=out_specs,
        out_shape=out_shape,
        scratch_shapes=[pltpu.VMEM((tm, d), BF16)],
        compiler_params=_cparams("parallel", "arbitrary"),
        name="norm_proj",
    )(*args)


def _out_proj_kernel(*refs):
    *a_refs, w_ref, x_ref, g_ref, o_ref = refs
    a = jnp.concatenate([a_ref[g] for a_ref in a_refs for g in range(a_ref.shape[0])], axis=1)
    y = jnp.dot(a, w_ref[...], preferred_element_type=F32)
    o_ref[...] = x_ref[...] + g_ref[...] * y


def _out_proj(heads_major, w, x2, g, seq):
    t = x2.shape[0]
    k, n = w.shape
    assert sum(a.shape[0] for a in heads_major) * HEAD_DIM == k
    tm, tn = PROJ_TM, PROJ_TN
    per_b = seq // tm
    return pl.pallas_call(
        _out_proj_kernel,
        grid=(t // tm, n // tn),
        in_specs=[pl.BlockSpec((a.shape[0], tm, HEAD_DIM), lambda i, j: (0, i, 0)) for a in heads_major] + [
            pl.BlockSpec((k, tn), lambda i, j: (0, j)),
            pl.BlockSpec((tm, tn), lambda i, j: (i, j)),
            pl.BlockSpec((None, 1, tn), lambda i, j: (i // per_b, 0, j)),
        ],
        out_specs=pl.BlockSpec((tm, tn), lambda i, j: (i, j)),
        out_shape=jax.ShapeDtypeStruct((t, n), F32),
        compiler_params=_cparams("parallel", "parallel"),
        name="out_proj",
    )(*heads_major, w, x2, g)


def _forget_kernel(f_ref, b_ref, col_ref, row_ref):
    z = f_ref[...] + b_ref[...]
    lf = jnp.minimum(z, 0.0) - jnp.log1p(jnp.exp(-jnp.abs(z)))
    s = lf.shape[0]
    pos = lax.broadcasted_iota(jnp.int32, lf.shape, 0)
    d = 1
    while d < s:
        lf = lf + jnp.where(pos >= d, pltpu.roll(lf, d, axis=0), 0.0)
        d *= 2
    col_ref[...] = lf
    row_ref[...] = lf.T[:FOX_HEADS, :]


def _forget_cumsum(f_side, b_forget, bsz, seq):
    bpad = jnp.zeros((1, LANES), F32).at[0, :FOX_HEADS].set(b_forget)
    col, row = pl.pallas_call(
        _forget_kernel,
        grid=(bsz,),
        in_specs=[
            pl.BlockSpec((None, seq, LANES), lambda b: (b, 0, 0)),
            pl.BlockSpec((1, LANES), lambda b: (0, 0)),
        ],
        out_specs=[
            pl.BlockSpec((None, seq, LANES), lambda b: (b, 0, 0)),
            pl.BlockSpec((None, FOX_HEADS, seq), lambda b: (b, 0, 0)),
        ],
        out_shape=[
            jax.ShapeDtypeStruct((bsz, seq, LANES), F32),
            jax.ShapeDtypeStruct((bsz, FOX_HEADS, seq), F32),
        ],
        compiler_params=_cparams("parallel"),
        name="forget_cumsum",
    )(f_side.reshape(bsz, seq, LANES), bpad)
    return col, row.reshape(bsz, FOX_HEADS, 1, seq)


def _online_softmax_step(s, v, m_ref, l_ref, acc_ref):
    m_prev = m_ref[...]
    m_new = jnp.maximum(m_prev, jnp.max(s, axis=-1, keepdims=True))
    alpha = jnp.exp(m_prev - m_new)
    p = jnp.exp(s - m_new)
    l_ref[...] = alpha * l_ref[...] + jnp.sum(p, axis=-1, keepdims=True)
    acc_ref[...] = alpha * acc_ref[...] + jnp.dot(p.astype(v.dtype), v, preferred_element_type=F32)
    m_ref[...] = m_new


def _fox_kernel(q_ref, k_ref, v_ref, frow_ref, fcol_ref, o_ref, m_ref, acc_ref):
    h = pl.program_id(1)
    qi = pl.program_id(2)
    tt = q_ref.shape[0]
    m_ref[...] = jnp.full_like(m_ref, MASKED)
    acc_ref[...] = jnp.zeros_like(acc_ref)
    ones = jnp.ones((tt, HEAD_DIM), BF16)
    lane = lax.broadcasted_iota(jnp.int32, fcol_ref.shape, 1)
    fq = jnp.sum(jnp.where(lane == h, fcol_ref[...], 0.0), axis=-1, keepdims=True)
    q = q_ref[...]

    def kv_block(j, diagonal):
        rows = pl.ds(pl.multiple_of(j * tt, tt), tt)
        s = lax.dot_general(q, k_ref[rows, :], (((1,), (1,)), ((), ())), preferred_element_type=F32)
        s = s + fq - frow_ref[j]
        if diagonal:
            keep = (lax.broadcasted_iota(jnp.int32, s.shape, 0)
                    >= lax.broadcasted_iota(jnp.int32, s.shape, 1))
            s = jnp.where(keep, s, MASKED)
        m_prev = m_ref[...]
        m_new = jnp.maximum(m_prev, jnp.max(s, axis=-1, keepdims=True))
        p = jnp.exp(s - m_new).astype(BF16)
        v1 = jnp.concatenate([v_ref[rows, :], ones], axis=1)
        acc_ref[...] = (jnp.exp(m_prev - m_new) * acc_ref[...]
                        + jnp.dot(p, v1, preferred_element_type=F32))
        m_ref[...] = m_new

    def below_diagonal(j, carry):
        kv_block(j, False)
        return carry

    lax.fori_loop(0, qi, below_diagonal, 0)
    kv_block(qi, True)
    acc = acc_ref[...]
    o_ref[...] = (acc[:, :HEAD_DIM] / acc[:, HEAD_DIM:HEAD_DIM + 1]).astype(o_ref.dtype)


def _fox_attention(proj, frow, fcol, bsz, seq):
    t = proj.shape[1]
    tt = ATTN_T
    nb = seq // tt
    kcol = FOX_W // HEAD_DIM
    blk = (None, tt, HEAD_DIM)
    whole = (None, seq, HEAD_DIM)
    return pl.pallas_call(
        _fox_kernel,
        grid=(bsz, FOX_HEADS, nb),
        in_specs=[
            pl.BlockSpec(blk, lambda b, h, qi: (h, b * nb + qi, 0)),
            pl.BlockSpec(whole, lambda b, h, qi: (kcol + h, b, 0)),
            pl.BlockSpec(whole, lambda b, h, qi: (2 * kcol + h, b, 0)),
            pl.BlockSpec((None, None, nb, 1, tt), lambda b, h, qi: (b, h, 0, 0, 0)),
            pl.BlockSpec((None, tt, LANES), lambda b, h, qi: (b, qi, 0)),
        ],
        out_specs=pl.BlockSpec(blk, lambda b, h, qi: (h, b * nb + qi, 0)),
        out_shape=jax.ShapeDtypeStruct((FOX_HEADS, t, HEAD_DIM), BF16),
        scratch_shapes=[
            pltpu.VMEM((tt, 1), F32), pltpu.VMEM((tt, 2 * HEAD_DIM), F32),
        ],
        compiler_params=_cparams("parallel", "parallel", "parallel"),
        name="fox_attention",
    )(proj, proj, proj, frow.reshape(bsz, FOX_HEADS, nb, 1, tt), fcol)


def _diff_kernel(q_ref, k_ref, v_ref, bias_ref, lam_ref, sg_ref, o_ref,
                 m0, l0, a0, m1, l1, a1, *, lam_init):
    qi = pl.program_id(2)
    tt = q_ref.shape[1]
    for m_ref, l_ref, a_ref in ((m0, l0, a0), (m1, l1, a1)):
        m_ref[...] = jnp.full_like(m_ref, MASKED)
        l_ref[...] = jnp.zeros_like(l_ref)
        a_ref[...] = jnp.zeros_like(a_ref)

    def kv_block(j, carry):
        rows = pl.ds(pl.multiple_of(j * tt, tt), tt)
        bias = bias_ref[qi - j]
        v = jnp.concatenate([v_ref[0, rows, :], v_ref[1, rows, :]], axis=1)
        for mi, (m_ref, l_ref, a_ref) in enumerate(((m0, l0, a0), (m1, l1, a1))):
            s = lax.dot_general(q_ref[mi], k_ref[mi, rows, :], (((1,), (1,)), ((), ())),
                                preferred_element_type=F32) + bias
            _online_softmax_step(s, v, m_ref, l_ref, a_ref)
        return carry

    lax.fori_loop(0, qi + 1, kv_block, 0)

    lf = lam_ref[...]
    lam = (jnp.exp(jnp.sum(lf[0:1] * lf[1:2], axis=-1, keepdims=True))
           - jnp.exp(jnp.sum(lf[2:3] * lf[3:4], axis=-1, keepdims=True)) + lam_init)
    o = a0[...] / l0[...] - lam * (a1[...] / l1[...])
    ms = jnp.mean(o * o, axis=-1, keepdims=True)
    o = (o * lax.rsqrt(ms + NORM_EPS) * sg_ref[...] * (1.0 - lam_init)).astype(o_ref.dtype)
    o_ref[0] = o[:, :HEAD_DIM]
    o_ref[1] = o[:, HEAD_DIM:]


def _diff_attention(proj, bias_tiles, lam_param, subln_gain, lam_init, bsz, seq):
    t = proj.shape[1]
    tt = ATTN_T
    nb = seq // tt
    qcol = 3 * FOX_W // DIFF_V_DIM
    kcol = qcol + DIFF_QK_W // DIFF_V_DIM
    vcol = kcol + DIFF_QK_W // DIFF_V_DIM
    blk = (2, tt, HEAD_DIM)
    whole = (2, seq, HEAD_DIM)
    return pl.pallas_call(
        functools.partial(_diff_kernel, lam_init=lam_init),
        grid=(bsz, DIFF_HEADS, nb),
        in_specs=[
            pl.BlockSpec(blk, lambda b, h, qi: (qcol + h, b * nb + qi, 0)),
            pl.BlockSpec(whole, lambda b, h, qi: (kcol + h, b, 0)),
            pl.BlockSpec(whole, lambda b, h, qi: (vcol + h, b, 0)),
            pl.BlockSpec((None, nb, tt, tt), lambda b, h, qi: (h, 0, 0, 0)),
            pl.BlockSpec((4, HEAD_DIM), lambda b, h, qi: (0, 0)),
            pl.BlockSpec((1, DIFF_V_DIM), lambda b, h, qi: (0, 0)),
        ],
        out_specs=pl.BlockSpec(blk, lambda b, h, qi: (h, b * nb + qi, 0)),
        out_shape=jax.ShapeDtypeStruct((2 * DIFF_HEADS, t, HEAD_DIM), BF16),
        scratch_shapes=[
            pltpu.VMEM((tt, 1), F32), pltpu.VMEM((tt, 1), F32), pltpu.VMEM((tt, DIFF_V_DIM), F32),
            pltpu.VMEM((tt, 1), F32), pltpu.VMEM((tt, 1), F32), pltpu.VMEM((tt, DIFF_V_DIM), F32),
        ],
        compiler_params=_cparams("parallel", "parallel", "parallel"),
        name="diff_attention",
    )(proj, proj, proj, bias_tiles, lam_param, subln_gain.reshape(1, DIFF_V_DIM))


def _t5_bucket_np(dist):
    n = np.maximum(dist, 0)
    nf = np.maximum(n, 1).astype(np.float32)
    large = BUCKET_MAX_EXACT + (np.log(nf / np.float32(BUCKET_MAX_EXACT))
                                / np.float32(math.log(BUCKET_MAX_DIST / BUCKET_MAX_EXACT))
                                * np.float32(N_BUCKETS - BUCKET_MAX_EXACT)).astype(np.int32)
    large = np.minimum(large, N_BUCKETS - 1)
    return np.where(n < BUCKET_MAX_EXACT, n, large).astype(np.int32)


def _bucket_thresholds():
    buckets = _t5_bucket_np(np.arange(BUCKET_MAX_DIST + 1))
    assert np.all(np.diff(buckets) >= 0)
    return [int(np.argmax(buckets >= b)) for b in range(N_BUCKETS)]


_BUCKET_THRESHOLDS = _bucket_thresholds()


def _bias_from_distance(dist, valid, table_ref, h):
    val = jnp.full(dist.shape, table_ref[0, h], F32)
    for b in range(1, N_BUCKETS):
        val = jnp.where(dist >= _BUCKET_THRESHOLDS[b], table_ref[b, h], val)
    return jnp.where(valid, val, MASKED)


def _causal_bias_kernel(table_ref, o_ref):
    h, delta = pl.program_id(0), pl.program_id(1)
    tt = o_ref.shape[0]
    dist = (delta * tt + lax.broadcasted_iota(jnp.int32, o_ref.shape, 0)
            - lax.broadcasted_iota(jnp.int32, o_ref.shape, 1))
    o_ref[...] = _bias_from_distance(dist, dist >= 0, table_ref, h)


def _causal_bias_tiles(table, seq, tt):
    nb = seq // tt
    heads = table.shape[1]
    return pl.pallas_call(
        _causal_bias_kernel,
        grid=(heads, nb),
        in_specs=[pl.BlockSpec(memory_space=pltpu.SMEM)],
        out_specs=pl.BlockSpec((None, None, tt, tt), lambda h, dlt: (h, dlt, 0, 0)),
        out_shape=jax.ShapeDtypeStruct((heads, nb, tt, tt), F32),
        compiler_params=_cparams("parallel", "parallel"),
        name="causal_bias_tiles",
    )(table)


def _dilated_bias_kernel(table_ref, o_ref):
    h = pl.program_id(0)
    span = DIL_BLOCK
    shape = o_ref.shape[1:]
    step = lax.broadcasted_iota(jnp.int32, shape, 0) + span - lax.broadcasted_iota(jnp.int32, shape, 1)
    valid = jnp.logical_and(step >= 0, step <= span)
    for p, (_, dil) in enumerate(DIL_PATTERNS):
        o_ref[p] = _bias_from_distance(step * dil, valid, table_ref, h)


def _dilated_bias_tiles(table):
    heads = table.shape[1]
    npat = len(DIL_PATTERNS)
    return pl.pallas_call(
        _dilated_bias_kernel,
        grid=(heads,),
        in_specs=[pl.BlockSpec(memory_space=pltpu.SMEM)],
        out_specs=pl.BlockSpec((npat, None, DIL_BLOCK, 2 * DIL_BLOCK), lambda h: (0, h, 0, 0)),
        out_shape=jax.ShapeDtypeStruct((npat, heads, DIL_BLOCK, 2 * DIL_BLOCK), F32),
        compiler_params=_cparams("parallel"),
        name="dilated_bias_tiles",
    )(table)


DIL_GROUP = 8


def _dilated_kernel(q_ref, k_ref, v_ref, bias_ref, o_ref, qf, kf, vf, o_scr, lse_scr):
    seq = q_ref.shape[0]
    blk = DIL_BLOCK
    qf[...] = q_ref[...].astype(F32)
    for src, dst in ((k_ref, kf), (v_ref, vf)):
        dst[0:blk, :] = jnp.zeros((blk, HEAD_DIM), F32)
        dst[blk:, :] = src[...].astype(F32)
    dn_qk = (((2,), (2,)), ((0,), (0,)))
    dn_pv = (((2,), (1,)), ((0,), (0,)))

    def rows(ref, start, dil):
        if dil == 1:
            return ref[pl.ds(start, blk), :]
        return ref[pl.ds(start, blk, stride=dil), :]

    for p, (_, dil) in enumerate(DIL_PATTERNS):
        nblk = seq // dil // blk
        blocks = [(jb, r) for jb in range(nblk) for r in range(dil)]
        bias = bias_ref[p]
        for g0 in range(0, len(blocks), DIL_GROUP):
            group = blocks[g0:g0 + DIL_GROUP]
            starts = [jb * blk * dil + r for jb, r in group]
            has_prev = [jb > 0 for jb, _ in group]
            q = jnp.stack([rows(qf, st, dil) for st in starts]).astype(BF16)
            kc = jnp.stack([rows(kf, blk + st, dil) for st in starts]).astype(BF16)
            vc = jnp.stack([rows(vf, blk + st, dil) for st in starts]).astype(BF16)
            s_cur = lax.dot_general(q, kc, dn_qk, preferred_element_type=F32) + bias[:, blk:]
            m = jnp.max(s_cur, axis=-1, keepdims=True)
            if any(has_prev):
                prev_starts = [max(blk + st - blk * dil, 0) for st in starts]
                kp = jnp.stack([rows(kf, st, dil) for st in prev_starts]).astype(BF16)
                vp = jnp.stack([rows(vf, st, dil) for st in prev_starts]).astype(BF16)
                s_prev = lax.dot_general(q, kp, dn_qk, preferred_element_type=F32) + bias[:, :blk]
                which = lax.broadcasted_iota(jnp.int32, s_prev.shape, 0)
                for gi, ok in enumerate(has_prev):
                    if not ok:
                        s_prev = jnp.where(which == gi, MASKED, s_prev)
                m = jnp.maximum(m, jnp.max(s_prev, axis=-1, keepdims=True))
                p_prev = jnp.exp(s_prev - m)
            p_cur = jnp.exp(s_cur - m)
            l = jnp.sum(p_cur, axis=-1, keepdims=True)
            acc = lax.dot_general(p_cur.astype(BF16), vc, dn_pv, preferred_element_type=F32)
            if any(has_prev):
                l = l + jnp.sum(p_prev, axis=-1, keepdims=True)
                acc = acc + lax.dot_general(p_prev.astype(BF16), vp, dn_pv, preferred_element_type=F32)
            out = acc / l
            lse = jnp.broadcast_to(m + jnp.log(l), out.shape)
            for gi, st in enumerate(starts):
                if dil == 1:
                    o_scr[p, pl.ds(st, blk), :] = out[gi]
                    lse_scr[p, pl.ds(st, blk), :] = lse[gi]
                else:
                    o_scr[p, pl.ds(st, blk, stride=dil), :] = out[gi]
                    lse_scr[p, pl.ds(st, blk, stride=dil), :] = lse[gi]

    lses = [lse_scr[p] for p in range(len(DIL_PATTERNS))]
    top = functools.reduce(jnp.maximum, lses)
    es = [jnp.exp(x - top) for x in lses]
    den = functools.reduce(lambda a, b: a + b, es)
    mixed = functools.reduce(lambda a, b: a + b, [(e / den) * o_scr[p] for p, e in enumerate(es)])
    o_ref[...] = mixed.astype(o_ref.dtype)


def _dilated_attention(qkv, bias_tiles, bsz, seq):
    t = qkv.shape[1]
    h16 = DIL_HEADS
    npat = len(DIL_PATTERNS)
    blk = (None, seq, HEAD_DIM)
    return pl.pallas_call(
        _dilated_kernel,
        grid=(bsz, DIL_HEADS),
        in_specs=[
            pl.BlockSpec(blk, lambda b, h: (h, b, 0)),
            pl.BlockSpec(blk, lambda b, h: (h16 + h, b, 0)),
            pl.BlockSpec(blk, lambda b, h: (2 * h16 + h, b, 0)),
            pl.BlockSpec((npat, None, DIL_BLOCK, 2 * DIL_BLOCK), lambda b, h: (0, h, 0, 0)),
        ],
        out_specs=pl.BlockSpec(blk, lambda b, h: (h, b, 0)),
        out_shape=jax.ShapeDtypeStruct((DIL_HEADS, t, HEAD_DIM), BF16),
        scratch_shapes=[
            pltpu.VMEM((seq, HEAD_DIM), F32),
            pltpu.VMEM((DIL_BLOCK + seq, HEAD_DIM), F32),
            pltpu.VMEM((DIL_BLOCK + seq, HEAD_DIM), F32),
            pltpu.VMEM((npat, seq, HEAD_DIM), F32),
            pltpu.VMEM((npat, seq, HEAD_DIM), F32),
        ],
        compiler_params=_cparams("parallel", "parallel"),
        name="dilated_attention",
    )(qkv, qkv, qkv, bias_tiles)


def _topk_rows(s, val_ref, pick_ref, payload=None):
    neg_rows = -lax.broadcasted_iota(jnp.int32, s.shape, 0).astype(F32)
    for k in range(PEER_TOPK):
        m = jnp.max(s, axis=0, keepdims=True)
        am = jnp.max(jnp.where(s == m, neg_rows, -jnp.inf), axis=0, keepdims=True)
        hit = neg_rows == am
        val_ref[k:k + 1, :] = m
        if payload is None:
            pick_ref[k:k + 1, :] = (-am).astype(jnp.int32)
        else:
            pick_ref[k:k + 1, :] = jnp.sum(jnp.where(hit, payload, 0), axis=0, keepdims=True)
        s = jnp.where(hit, -jnp.inf, s)


def _candidate_rows(a):
    need = PEER_TOPK // (a + 1)
    if need > SUBLANES:
        return PEER_TOPK
    return SUBLANES if need > 1 else 1


def _route_kernel(q_ref, keys_ref, idx_ref, gate_ref, v0, i0, v1, i1, best_ref):
    _route_half(q_ref[0], keys_ref[0], v0, i0)
    _route_half(q_ref[1], keys_ref[1], v1, i1)
    _route_combine(idx_ref, gate_ref, v0, i0, v1, i1, best_ref)


def _route_half(q, keys, val_ref, idx_ref):
    scores = lax.dot_general(keys, q, (((1,), (1,)), ((), ())), preferred_element_type=F32)
    _topk_rows(scores, val_ref, idx_ref)


def _route_combine(idx_ref, gate_ref, v0, i0, v1, i1, best_ref):
    first_single = min(a for a in range(PEER_TOPK) if _candidate_rows(a) == 1)
    cand_s, cand_i = [], []
    for a in range(first_single):
        nb = _candidate_rows(a)
        cand_s.append(v0[a:a + 1, :] + v1[0:nb, :])
        cand_i.append(i0[a:a + 1, :] * PEER_NKEYS + i1[0:nb, :])
    cand_s.append(v0[first_single:, :] + v1[0:1, :])
    cand_i.append(i0[first_single:, :] * PEER_NKEYS + i1[0:1, :])
    _topk_rows(jnp.concatenate(cand_s, axis=0), best_ref, idx_ref, jnp.concatenate(cand_i, axis=0))
    best_s = best_ref[...]
    e = jnp.exp(best_s - best_s[0:1, :])
    gate_ref[...] = e / jnp.sum(e, axis=0, keepdims=True)


def _route_scratch(tm):
    return [pltpu.VMEM((PEER_TOPK, tm), F32), pltpu.VMEM((PEER_TOPK, tm), jnp.int32),
            pltpu.VMEM((PEER_TOPK, tm), F32), pltpu.VMEM((PEER_TOPK, tm), jnp.int32),
            pltpu.VMEM((PEER_TOPK, tm), F32)]


def _peer_route(qp, sub_keys, t):
    tm = ROUTE_TM
    out_spec = pl.BlockSpec((None, PEER_TOPK, tm), lambda i, h: (h, 0, i))
    return pl.pallas_call(
        _route_kernel,
        grid=(t // tm, PEER_HEADS),
        in_specs=[
            pl.BlockSpec((2, tm, PEER_DKEY // 2), lambda i, h: (h, i, 0)),
            pl.BlockSpec((2, PEER_NKEYS, PEER_DKEY // 2), lambda i, h: (0, 0, 0)),
        ],
        out_specs=[out_spec, out_spec],
        out_shape=[
            jax.ShapeDtypeStruct((PEER_HEADS, PEER_TOPK, t), jnp.int32),
            jax.ShapeDtypeStruct((PEER_HEADS, PEER_TOPK, t), F32),
        ],
        scratch_shapes=_route_scratch(tm),
        compiler_params=_cparams("parallel", "parallel"),
        name="peer_route",
    )(qp, sub_keys)


def _gelu_exact(a):
    return 0.5 * a * (1.0 + lax.erf(a * (2.0 ** -0.5)))


def _pack_expert_tables(tables_u, tables_v, layer):
    _, e, d = tables_u.shape
    te = PACK_ROWS
    n_steps = e // te
    layer_rows = pl.BlockSpec((None, te, d), lambda i: (layer, i, 0))
    return pl.pallas_call(
        _pack_kernel,
        grid=(n_steps,),
        in_specs=[layer_rows, layer_rows],
        out_specs=pl.BlockSpec(memory_space=pl.ANY),
        out_shape=jax.ShapeDtypeStruct((e, 1, d), jnp.uint32),
        scratch_shapes=[pltpu.VMEM((2, te, d), jnp.uint32), pltpu.SemaphoreType.DMA((2,))],
        compiler_params=_cparams("arbitrary"),
        name="pack_expert_tables",
    )(tables_u, tables_v)


PACK_ROWS = 128


def _pack_kernel(u_ref, v_ref, out_hbm, stage, sems):
    i = pl.program_id(0)
    n_steps = pl.num_programs(0)
    te, d = u_ref.shape
    half = d // 2
    slot = i % 2

    def bf16_high_bits(x):
        return lax.bitcast_convert_type(x.astype(BF16).astype(F32), jnp.uint32)

    def pack(ref):
        return bf16_high_bits(ref[:, half:]) | (bf16_high_bits(ref[:, :half]) >> 16)

    def wait_slot(sl):
        pltpu.make_async_copy(stage.at[sl], stage.at[sl], sems.at[sl]).wait()

    @pl.when(i >= 2)
    def _():
        wait_slot(slot)

    stage[slot] = jnp.concatenate([pack(u_ref), pack(v_ref)], axis=1)
    for r in range(te):
        pltpu.make_async_copy(stage.at[slot, pl.ds(r, 1)], out_hbm.at[i * te + r],
                              sems.at[slot]).start(priority=r % 2)

    @pl.when(i == n_steps - 1)
    def _():
        wait_slot(slot)

        @pl.when(n_steps > 1)
        def _():
            wait_slot(1 - slot)


def _unpack_words(w):
    lo = lax.bitcast_convert_type(w << 16, F32)
    hi = lax.bitcast_convert_type(w & jnp.uint32(0xFFFF0000), F32)
    return lo, hi


PEER_NBUF = 4
PEER_LOOKAHEAD = 2
ROUTE_AHEAD_BUFS = 3


def _expert_kernel(idx0_ref, gates0_ref, qnext_ref, keys_ref, x_ref, gain_ref, sc_ref, sh_ref, g2_ref,
                   tab_hbm, o_ref, *scratch):
    bufs, sems = scratch[:PEER_NBUF], scratch[PEER_NBUF]
    (idx_smem, gates_v, idx_t, gate_t, stage, route_sem), route_scratch = (
        scratch[PEER_NBUF + 1:PEER_NBUF + 7], scratch[PEER_NBUF + 7:])
    s = pl.program_id(0)
    n_steps = pl.num_programs(0)
    tp = x_ref.shape[0]
    gt = tp // PEER_NBUF
    d = x_ref.shape[1]
    ngroup = PEER_PICKS // SUBLANES
    nword = d // 2 // LANES
    spt = ROUTE_TM // tp
    r = s % spt
    par = (s // spt) % ROUTE_AHEAD_BUFS
    gain, sc, sh, g2 = gain_ref[...], sc_ref[...], sh_ref[...], g2_ref[...]
    eye = (lax.broadcasted_iota(jnp.int32, (PEER_PICKS, PEER_PICKS), 0)
           == lax.broadcasted_iota(jnp.int32, (PEER_PICKS, PEER_PICKS), 1))

    def park_routing(idx_rows, to_par):
        cp = pltpu.make_async_copy(idx_rows, idx_smem.at[to_par], route_sem.at[0])
        cp.start()
        cp.wait()

    @pl.when(s == 0)
    def _():
        for tile in range(ROUTE_AHEAD_BUFS - 1):
            park_routing(idx0_ref.at[pl.ds(tile * ROUTE_TM, ROUTE_TM)], tile)
            gates_v[tile] = gates0_ref[pl.ds(tile * ROUTE_TM, ROUTE_TM), :]

    head_rows = pl.ds(pl.multiple_of(r * PEER_TOPK, PEER_TOPK), PEER_TOPK)
    rv0, ri0, rv1, ri1, rbest = route_scratch
    route_stages = (
        lambda: _route_half(qnext_ref[2 * r], keys_ref[0], rv0, ri0),
        lambda: _route_half(qnext_ref[2 * r + 1], keys_ref[1], rv1, ri1),
        lambda: _route_combine(idx_t.at[head_rows], gate_t.at[head_rows], rv0, ri0, rv1, ri1, rbest),
    )

    def start_token(tile_par, row, j, ti):
        for g in range(ngroup):
            for k in range(SUBLANES):
                e = idx_smem[tile_par, row, g * SUBLANES + k]
                pltpu.make_async_copy(
                    tab_hbm.at[e],
                    bufs[j].at[pl.ds((ti * ngroup + g) * SUBLANES + k, 1)],
                    sems.at[j]).start(priority=k % 2)

    def wait_group(j):
        pltpu.make_async_copy(bufs[j], bufs[j], sems.at[j]).wait()

    def gated_activation(buf, ti, tok):
        h = _modulated_norm(x_ref[pl.ds(tok, 1), :], gain, sc, sh)
        hb = jnp.broadcast_to(h, (SUBLANES, d))
        parts = []
        for g in range(ngroup):
            acc = None
            for c in range(nword):
                lo, hi = _unpack_words(buf[pl.ds((ti * ngroup + g) * SUBLANES, SUBLANES),
                                           c * LANES:(c + 1) * LANES])
                term = (lo * hb[:, c * LANES:(c + 1) * LANES]
                        + hi * hb[:, (nword + c) * LANES:(nword + c + 1) * LANES])
                acc = term if acc is None else acc + term
            parts.append(acc)
        act = jnp.sum(jnp.concatenate(parts, axis=0), axis=-1, keepdims=True)
        gate_row = gates_v[par, pl.ds(r * tp + tok, 1), :]
        gate_col = jnp.sum(jnp.where(eye, gate_row, 0.0), axis=-1, keepdims=True)
        return gate_col * _gelu_exact(act)

    def weighted_values(buf, ti, tok, a):
        ylo = [None] * nword
        yhi = [None] * nword
        for g in range(ngroup):
            ag = a[g * SUBLANES:(g + 1) * SUBLANES, :]
            for c in range(nword):
                lo, hi = _unpack_words(buf[pl.ds((ti * ngroup + g) * SUBLANES, SUBLANES),
                                           (nword + c) * LANES:(nword + c + 1) * LANES])
                ylo[c] = lo * ag if ylo[c] is None else ylo[c] + lo * ag
                yhi[c] = hi * ag if yhi[c] is None else yhi[c] + hi * ag
        y = jnp.sum(jnp.concatenate(ylo + yhi, axis=1), axis=0, keepdims=True)
        o_ref[pl.ds(tok, 1), :] = x_ref[pl.ds(tok, 1), :] + g2 * y

    @pl.when(s == 0)
    def _():
        for j in range(PEER_LOOKAHEAD):
            def body(ti, carry, j=j):
                start_token(0, j * gt + ti, j, ti)
                return carry
            lax.fori_loop(0, gt, body, 0)

    last_of_tile = r == spt - 1
    next_par = jnp.where(last_of_tile, (par + 1) % ROUTE_AHEAD_BUFS, par)
    next_row = jnp.where(last_of_tile, 0, (r + 1) * tp)

    pending = None
    for j in range(PEER_NBUF):
        wait_group(j)
        if j < len(route_stages):
            route_stages[j]()
        ahead = j + PEER_LOOKAHEAD
        for ti in range(gt):
            tok = j * gt + ti
            if ahead < PEER_NBUF:
                start_token(par, r * tp + ahead * gt + ti, ahead, ti)
            else:
                start_token(next_par, next_row + (ahead - PEER_NBUF) * gt + ti, ahead - PEER_NBUF, ti)
            a = gated_activation(bufs[j], ti, tok)
            if pending is not None:
                weighted_values(*pending)
            pending = (bufs[j], ti, tok, a)
    weighted_values(*pending)

    @pl.when(last_of_tile)
    def _():
        free_par = (par + ROUTE_AHEAD_BUFS - 1) % ROUTE_AHEAD_BUFS
        stage[...] = idx_t[...].T
        gates_v[free_par] = gate_t[...].T
        park_routing(stage, free_par)

    @pl.when(s == n_steps - 1)
    def _():
        for j in range(PEER_LOOKAHEAD):
            wait_group(j)


def _peer_experts(x2, gain, sc, sh, g2, idx0, gates0, qp, sub_keys, packed, seq):
    t, d = x2.shape
    tp = PEER_TP
    tm = ROUTE_TM
    ahead = ROUTE_AHEAD_BUFS - 1
    assert tm == PEER_HEADS * tp
    n_steps = t // tp
    n_tiles = t // tm
    spt = tm // tp
    per_b = seq // tp
    whole = lambda s: (0, 0)
    return pl.pallas_call(
        _expert_kernel,
        grid=(n_steps,),
        in_specs=[
            pl.BlockSpec((ahead * tm, PEER_PICKS), whole),
            pl.BlockSpec((ahead * tm, PEER_PICKS), whole),
            pl.BlockSpec((2 * PEER_HEADS, tm, PEER_DKEY // 2),
                         lambda s: (0, jnp.minimum(s // spt + ahead, n_tiles - 1), 0)),
            pl.BlockSpec((2, PEER_NKEYS, PEER_DKEY // 2), lambda s: (0, 0, 0)),
            pl.BlockSpec((tp, d), lambda s: (s, 0)),
            pl.BlockSpec((1, d), lambda s: (0, 0)),
            pl.BlockSpec((None, 1, d), lambda s: (s // per_b, 0, 0)),
            pl.BlockSpec((None, 1, d), lambda s: (s // per_b, 0, 0)),
            pl.BlockSpec((None, 1, d), lambda s: (s // per_b, 0, 0)),
            pl.BlockSpec(memory_space=pl.ANY),
        ],
        out_specs=pl.BlockSpec((tp, d), lambda s: (s, 0)),
        out_shape=jax.ShapeDtypeStruct((t, d), F32),
        scratch_shapes=(
            [pltpu.VMEM((tp // PEER_NBUF * PEER_PICKS, d), jnp.uint32)] * PEER_NBUF
            + [pltpu.SemaphoreType.DMA((PEER_NBUF,)),
               pltpu.SMEM((ROUTE_AHEAD_BUFS, tm, PEER_PICKS), jnp.int32),
               pltpu.VMEM((ROUTE_AHEAD_BUFS, tm, PEER_PICKS), F32),
               pltpu.VMEM((PEER_PICKS, tm), jnp.int32),
               pltpu.VMEM((PEER_PICKS, tm), F32),
               pltpu.VMEM((tm, PEER_PICKS), jnp.int32),
               pltpu.SemaphoreType.DMA((1,))]
            + _route_scratch(tm)),
        compiler_params=_cparams("arbitrary"),
        name="peer_experts",
    )(idx0, gates0, qp, sub_keys, x2, gain.reshape(1, d), sc, sh, g2, packed)


def _peer_ffn(x2, gain, sc, sh, g2, w_query, sub_keys, tables_u, tables_v, layer, seq):
    no_gain = jnp.ones((1, w_query.shape[1]), F32)
    qp = _norm_proj(x2, gain, sc, sh, w_query.astype(BF16), no_gain, (), seq)
    keys = sub_keys.astype(BF16)
    lead = (ROUTE_AHEAD_BUFS - 1) * ROUTE_TM
    idx0, gates0 = _peer_route(qp, keys, lead)
    idx0 = idx0.transpose(2, 0, 1).reshape(lead, PEER_PICKS)
    gates0 = gates0.transpose(2, 0, 1).reshape(lead, PEER_PICKS)
    packed = _pack_expert_tables(tables_u, tables_v, layer)
    return _peer_experts(x2, gain, sc, sh, g2, idx0, gates0, qp, keys, packed, seq)


def _tile_gain(gain, heads, scale=1.0):
    return jnp.tile(gain * scale, heads)


def _even_layer(x2, mods, norm_gain, w_in, b_forget, fox_qk_gain, diff_qk_gain, diff_lambda,
                diff_subln_gain, w_out, diff_bias, lam_init, bsz, seq):
    sh1, sc1, g1 = mods
    w_main = jnp.concatenate([w_in[:, :3 * FOX_W], w_in[:, 3 * FOX_W + FOX_HEADS:]], axis=1).astype(BF16)
    w_forget = jnp.zeros((D_MODEL, LANES), F32).at[:, :FOX_HEADS].set(
        w_in[:, 3 * FOX_W:3 * FOX_W + FOX_HEADS]).astype(BF16)
    ones = jnp.ones((FOX_W,), F32)
    head_gain = jnp.concatenate([
        _tile_gain(fox_qk_gain[0], FOX_HEADS, ATTN_SCALE), _tile_gain(fox_qk_gain[1], FOX_HEADS), ones,
        _tile_gain(diff_qk_gain[0], 2 * DIFF_HEADS, ATTN_SCALE), _tile_gain(diff_qk_gain[1], 2 * DIFF_HEADS),
        ones]).reshape(1, -1)
    proj, f_side = _norm_proj(x2, norm_gain, sc1, sh1, w_main, head_gain, (0, 1, 3, 4), seq,
                              w_side=w_forget)
    fcol, frow = _forget_cumsum(f_side, b_forget, bsz, seq)
    fox_o = _fox_attention(proj, frow, fcol, bsz, seq)
    bias_tiles = _causal_bias_tiles(diff_bias, seq, ATTN_T)
    diff_o = _diff_attention(proj, bias_tiles, diff_lambda, diff_subln_gain, lam_init, bsz, seq)
    return _out_proj([fox_o, diff_o], w_out.astype(BF16), x2, g1, seq)


def _odd_layer(x2, mods, norm_gain, w_qkv, qk_gain, w_out, bias_table, bsz, seq):
    sh1, sc1, g1 = mods
    ones = jnp.ones((D_MODEL,), F32)
    head_gain = jnp.concatenate([
        _tile_gain(qk_gain[0], DIL_HEADS, ATTN_SCALE), _tile_gain(qk_gain[1], DIL_HEADS), ones]).reshape(1, -1)
    qkv = _norm_proj(x2, norm_gain, sc1, sh1, w_qkv.astype(BF16), head_gain, (0, 1, 2, 3), seq)
    mixed = _dilated_attention(qkv, _dilated_bias_tiles(bias_table), bsz, seq)
    return _out_proj([mixed], w_out.astype(BF16), x2, g1, seq)


def kernel(x, c, rel_bias, norm_gain, w_ada, b_ada, even_w_in, even_b_forget, even_fox_qk_gain,
           even_diff_qk_gain, even_diff_lambda, even_diff_subln_gain, even_w_out, odd_w_qkv,
           odd_qk_gain, odd_w_out, peer_w_query, peer_sub_keys, peer_u, peer_v):
    bsz, seq, d = x.shape
    x2 = x.reshape(bsz * seq, d)
    mod = _adaln(c, w_ada, b_ada)
    for i in range(DEPTH):
        sh1, sc1, g1, sh2, sc2, g2 = [m.reshape(bsz, 1, d) for m in jnp.split(mod[i], 6, axis=-1)]
        j = i // 2
        if i % 2 == 0:
            lam_init = 0.8 - 0.6 * math.exp(-0.3 * i)
            x2 = _even_layer(x2, (sh1, sc1, g1), norm_gain[i, 0], even_w_in[j], even_b_forget[j],
                             even_fox_qk_gain[j], even_diff_qk_gain[j], even_diff_lambda[j],
                             even_diff_subln_gain[j], even_w_out[j], rel_bias[:, :DIFF_HEADS],
                             lam_init, bsz, seq)
        else:
            x2 = _odd_layer(x2, (sh1, sc1, g1), norm_gain[i, 0], odd_w_qkv[j], odd_qk_gain[j],
                            odd_w_out[j], rel_bias, bsz, seq)
        x2 = _peer_ffn(x2, norm_gain[i, 1], sc2, sh2, g2, peer_w_query[i], peer_sub_keys[i],
                       peer_u, peer_v, i, seq)
    return x2.reshape(bsz, seq, d)
```
